```python
import math
import jax, jax.numpy as jnp
from jax import lax
import numpy as np

D_MODEL = 2048
BATCH = 4
SEQ = 4096
DEPTH = 2

HEAD_DIM = 128
Q_BLOCK = 128
NEG_INF = -1e30
RMS_EPS = 1e-6
N_EVEN = (DEPTH + 1) // 2
N_ODD = DEPTH // 2

NUM_BUCKETS = 32
T5_MAX_DIST = 2048
BIAS_HEADS = 8

NSA_HEADS = 8
NSA_KV_HEADS = 2
NSA_GROUP = NSA_HEADS // NSA_KV_HEADS
CMP_LEN = 32
CMP_STRIDE = 16
CMP_HIDDEN = HEAD_DIM
SLC_LEN = 64
N_SEL = 16
WIN = 512
SEL_QB = 64
FORCE_SCORE = 1e9

MLA_HEADS = 8
Q_LORA = 512
KV_LORA = 512
QK_NOPE = 128
QK_ROPE = 64
V_DIM = 128
ROPE_THETA = 10000.0

DIL_PATTERNS = ((128, 1), (512, 4), (2048, 16))
DIL_HEADS = 8

NSA_Q_W = NSA_HEADS * HEAD_DIM
NSA_KV_W = 6 * NSA_KV_HEADS * HEAD_DIM
NSA_G_W = 3 * NSA_HEADS
AB_IN_W = NSA_Q_W + NSA_KV_W + NSA_G_W + Q_LORA + KV_LORA + QK_ROPE
AB_OUT_W = NSA_HEADS * HEAD_DIM + MLA_HEADS * V_DIM
C_IN_W = len(DIL_PATTERNS) * 3 * DIL_HEADS * HEAD_DIM
C_OUT_W = DIL_HEADS * HEAD_DIM

D_FF = 5632
N_EXPERTS = 8
MOE_TOP_K = 2
D_FF_EXPERT = 7168
MOE_BLOCK = 256

kernel_name = 'hybrid_nsa_mla_dilated_moe_adaln'

F32 = jnp.float32


def rms_norm(x, g):
    xf = x.astype(F32)
    y = xf * lax.rsqrt(jnp.mean(xf * xf, axis=-1, keepdims=True) + RMS_EPS)
    return (y * g.astype(F32)).astype(x.dtype)


def t5_bucket(dist):
    n = jnp.maximum(dist, 0)
    max_exact = NUM_BUCKETS // 2
    nf = jnp.maximum(n, 1).astype(F32)
    large = max_exact + (jnp.log(nf / max_exact) / math.log(T5_MAX_DIST / max_exact)
                         * (NUM_BUCKETS - max_exact)).astype(jnp.int32)
    large = jnp.minimum(large, NUM_BUCKETS - 1)
    return jnp.where(n < max_exact, n, large)


def swiglu(x, w_gate, w_up, w_down):
    return jnp.dot(jax.nn.silu(jnp.dot(x, w_gate)) * jnp.dot(x, w_up), w_down)


def banded_attention(q, k, v, max_back, dist_scale, bias_tab):
    B, L, G, R, D = q.shape
    nb = -(-max_back // Q_BLOCK)
    nq = -(-L // Q_BLOCK)
    pad = nq * Q_BLOCK - L
    kw = (nb + 1) * Q_BLOCK
    qp = jnp.pad(q, ((0, 0), (0, pad), (0, 0), (0, 0), (0, 0))).reshape(B, nq, Q_BLOCK, G, R, D)
    kp = jnp.pad(k, ((0, 0), (nb * Q_BLOCK, pad), (0, 0), (0, 0)))
    vp = jnp.pad(v, ((0, 0), (nb * Q_BLOCK, pad), (0, 0), (0, 0)))
    kidx = jnp.arange(nq)[:, None] * Q_BLOCK + jnp.arange(kw)[None, :]
    kb = kp[:, kidx]
    vb = vp[:, kidx]
    rel = jnp.arange(Q_BLOCK)[:, None] + nb * Q_BLOCK - jnp.arange(kw)[None, :]
    mask = ((rel >= 0) & (rel <= max_back))[None] & ((kidx - nb * Q_BLOCK) >= 0)[:, None, :]
    bias = bias_tab[t5_bucket(rel * dist_scale)].astype(F32).reshape(Q_BLOCK, kw, G, R).transpose(2, 3, 0, 1)
    s = jnp.einsum('bnqgrd,bnkgd->bngrqk', qp, kb, preferred_element_type=F32) * (D ** -0.5) + bias
    s = jnp.where(mask[None, :, None, None], s, NEG_INF)
    m = jnp.max(s, axis=-1, keepdims=True)
    p = jnp.exp(s - m)
    l = jnp.sum(p, axis=-1, keepdims=True)
    o = jnp.einsum('bngrqk,bnkgd->bnqgrd', p / l, vb.astype(F32))
    lse = (m + jnp.log(l))[..., 0].transpose(0, 1, 4, 2, 3)
    o = o.reshape(B, nq * Q_BLOCK, G, R, D)[:, :L]
    lse = lse.reshape(B, nq * Q_BLOCK, G, R)[:, :L]
    return o, lse


def nsa_compress(blocks, pos, w1, w2):
    w1 = w1.reshape(CMP_LEN, HEAD_DIM, CMP_HIDDEN)
    hid = jax.nn.gelu(jnp.einsum('bnlgd,lde->bnge', blocks + pos[None, None, :, None, :], w1))
    return jnp.dot(hid, w2)


def nsa_mixer(q, kv, gate_logits, rel_bias, pos_k, w1_k, w2_k, pos_v, w1_v, w2_v):
    B, S, _ = q.shape
    G, R, D = NSA_KV_HEADS, NSA_GROUP, HEAD_DIM
    q = q.reshape(B, S, G, R, D)
    kv = kv.reshape(B, S, 6, G, D)
    k_cmp, v_cmp, k_slc, v_slc, k_win, v_win = (kv[:, :, i] for i in range(6))
    scale = D ** -0.5
    tpos = jnp.arange(S)

    n_cmp = (S - CMP_LEN) // CMP_STRIDE + 1
    cidx = jnp.arange(n_cmp)[:, None] * CMP_STRIDE + jnp.arange(CMP_LEN)[None, :]
    kc = nsa_compress(k_cmp[:, cidx], pos_k, w1_k, w2_k)
    vc = nsa_compress(v_cmp[:, cidx], pos_v, w1_v, w2_v)
    rel_c = tpos[:, None] - (jnp.arange(n_cmp) * CMP_STRIDE + CMP_LEN - 1)[None, :]
    bias_c = rel_bias[t5_bucket(rel_c)].astype(F32).reshape(S, n_cmp, G, R).transpose(0, 2, 3, 1)
    s_c = jnp.einsum('bsgrd,bngd->bsgrn', q, kc, preferred_element_type=F32) * scale + bias_c
    s_c = jnp.where((rel_c >= 0)[:, None, None, :], s_c, NEG_INF)
    has_c = (tpos >= CMP_LEN - 1)[None, :, None, None, None]
    p_c = jnp.where(has_c, jax.nn.softmax(s_c, axis=-1), 0.0)
    o_cmp = jnp.einsum('bsgrn,bngd->bsgrd', p_c, vc.astype(F32))

    n_slc = S // SLC_LEN
    ratio = SLC_LEN // CMP_STRIDE
    span = CMP_LEN // CMP_STRIDE
    imp_c = jnp.sum(p_c, axis=3)
    pad_front = span - 1
    pad_back = max(0, ratio * n_slc + ratio - pad_front - n_cmp)
    P = jnp.pad(imp_c, ((0, 0), (0, 0), (0, 0), (pad_front, pad_back)))
    imp_s = sum(P[..., (m - n + pad_front):(m - n + pad_front) + ratio * n_slc:ratio]
                for m in range(ratio) for n in range(span))
    blk_t = tpos // SLC_LEN
    jb = jnp.arange(n_slc)
    valid_s = jb[None, :] <= blk_t[:, None]
    forced = (jb[None, :] == 0) | (jb[None, :] == blk_t[:, None]) | (jb[None, :] == blk_t[:, None] - 1)
    score = jnp.where(forced[None, :, None, :], FORCE_SCORE,
                      jnp.where(valid_s[None, :, None, :], imp_s, -1.0))
    n_top = min(N_SEL, n_slc)
    top_v, top_i = lax.top_k(score, n_top)
    sel_ok = top_v > -0.5

    kb = k_slc.reshape(B, n_slc, SLC_LEN, G, D).transpose(0, 3, 1, 2, 4)
    vb = v_slc.reshape(B, n_slc, SLC_LEN, G, D).transpose(0, 3, 1, 2, 4)
    nqb = S // SEL_QB
    tbl = rel_bias.reshape(NUM_BUCKETS, G, R).transpose(1, 0, 2)
    bi = jnp.arange(B)[:, None, None, None]
    gi = jnp.arange(G)[None, None, :, None]

    def split_q(a):
        return a.reshape((B, nqb, SEL_QB) + a.shape[2:]).swapaxes(0, 1)

    def sel_block(args):
        qb, ib, okb, tb = args
        ks = kb[bi, gi, ib]
        vs = vb[bi, gi, ib]
        kpos = ib[..., None] * SLC_LEN + jnp.arange(SLC_LEN)
        rel = tb[None, :, None, None, None] - kpos
        mask = (rel >= 0) & okb[..., None]
        bias = tbl[gi[..., None], t5_bucket(rel)].astype(F32).transpose(0, 1, 2, 5, 3, 4)
        s = jnp.einsum('bqgrd,bqgnkd->bqgrnk', qb, ks, preferred_element_type=F32) * scale + bias
        s = jnp.where(mask[:, :, :, None], s, NEG_INF)
        shp = s.shape
        p = jax.nn.softmax(s.reshape(shp[:4] + (shp[4] * shp[5],)), axis=-1).reshape(shp)
        return jnp.einsum('bqgrnk,bqgnkd->bqgrd', p, vs.astype(F32))

    o_slc = lax.map(sel_block, (split_q(q), split_q(top_i), split_q(sel_ok), tpos.reshape(nqb, SEL_QB)))
    o_slc = o_slc.swapaxes(0, 1).reshape(B, S, G, R, D)

    o_win, _ = banded_attention(q, k_win, v_win, WIN - 1, 1, rel_bias)

    gates = jax.nn.sigmoid(gate_logits.astype(F32)).reshape(B, S, G, R, 3)
    o = gates[..., 0:1] * o_cmp + gates[..., 1:2] * o_slc + gates[..., 2:3] * o_win
    return o.reshape(B, S, G * R * D).astype(q.dtype)


def apply_rope(x, cos, sin):
    half = x.shape[-1] // 2
    x1, x2 = x[..., :half].astype(F32), x[..., half:].astype(F32)
    return jnp.concatenate([x1 * cos - x2 * sin, x2 * cos + x1 * sin], axis=-1).astype(x.dtype)


def mla_mixer(q_lat, kv_lat, k_pe, positions, q_norm_g, kv_norm_g, w_uq, w_ukv):
    B, S, _ = q_lat.shape
    q = jnp.dot(rms_norm(q_lat, q_norm_g), w_uq).reshape(B, S, MLA_HEADS, QK_NOPE + QK_ROPE)
    q_nope, q_pe = q[..., :QK_NOPE], q[..., QK_NOPE:]
    kvu = jnp.dot(rms_norm(kv_lat, kv_norm_g), w_ukv).reshape(B, S, MLA_HEADS, QK_NOPE + V_DIM)
    k_nope, v = kvu[..., :QK_NOPE], kvu[..., QK_NOPE:]
    inv_freq = ROPE_THETA ** (-jnp.arange(0, QK_ROPE, 2, dtype=F32) / QK_ROPE)
    ang = positions.astype(F32)[..., None] * inv_freq
    cos, sin = jnp.cos(ang), jnp.sin(ang)
    q_pe = apply_rope(q_pe, cos[:, :, None, :], sin[:, :, None, :])
    k_pe = apply_rope(k_pe, cos, sin)
    scale = (QK_NOPE + QK_ROPE) ** -0.5
    nqb = S // Q_BLOCK
    tpos = jnp.arange(S)

    def split_q(a):
        return a.reshape((B, nqb, Q_BLOCK) + a.shape[2:]).swapaxes(0, 1)

    def blk(args):
        qn, qp, tb = args
        s = (jnp.einsum('bqhd,bkhd->bhqk', qn, k_nope, preferred_element_type=F32)
             + jnp.einsum('bqhd,bkd->bhqk', qp, k_pe, preferred_element_type=F32)) * scale
        s = jnp.where(tpos[None, :] <= tb[:, None], s, NEG_INF)
        p = jax.nn.softmax(s, axis=-1)
        return jnp.einsum('bhqk,bkhd->bqhd', p, v.astype(F32))

    o = lax.map(blk, (split_q(q_nope), split_q(q_pe), tpos.reshape(nqb, Q_BLOCK)))
    return o.swapaxes(0, 1).reshape(B, S, MLA_HEADS * V_DIM).astype(q_lat.dtype)


def mixer_ab(h, positions, rel_bias, w_in, w_out, pos_k, w1_k, w2_k, pos_v, w1_v, w2_v,
             q_norm_g, kv_norm_g, w_uq, w_ukv):
    z = jnp.dot(h, w_in)
    cuts = np.cumsum([NSA_Q_W, NSA_KV_W, NSA_G_W, Q_LORA, KV_LORA]).tolist()
    nsa_q, nsa_kv, nsa_g, q_lat, kv_lat, k_pe = jnp.split(z, cuts, axis=-1)
    o_a = nsa_mixer(nsa_q, nsa_kv, nsa_g, rel_bias, pos_k, w1_k, w2_k, pos_v, w1_v, w2_v)
    o_b = mla_mixer(q_lat, kv_lat, k_pe, positions, q_norm_g, kv_norm_g, w_uq, w_ukv)
    return jnp.dot(jnp.concatenate([o_a, o_b], axis=-1), w_out)


def mixer_c(h, rel_bias, w_in, w_out):
    B, S, _ = h.shape
    z = jnp.dot(h, w_in).reshape(B, S, len(DIL_PATTERNS), 3, DIL_HEADS, HEAD_DIM)
    outs, lses = [], []
    for g, (win, dil) in enumerate(DIL_PATTERNS):
        L = S // dil

        def to_sub(a):
            return a.reshape(B, L, dil, DIL_HEADS, HEAD_DIM).swapaxes(1, 2).reshape(B * dil, L, DIL_HEADS, HEAD_DIM)

        o, lse = banded_attention(to_sub(z[:, :, g, 0])[:, :, :, None], to_sub(z[:, :, g, 1]),
                                  to_sub(z[:, :, g, 2]), win // dil, dil, rel_bias)
        outs.append(o.reshape(B, dil, L, DIL_HEADS, HEAD_DIM).swapaxes(1, 2).reshape(B, S, DIL_HEADS, HEAD_DIM))
        lses.append(lse.reshape(B, dil, L, DIL_HEADS).swapaxes(1, 2).reshape(B, S, DIL_HEADS))
    w = jax.nn.softmax(jnp.stack(lses, axis=0), axis=0)
    o = jnp.sum(w[..., None] * jnp.stack(outs, axis=0), axis=0)
    return jnp.dot(o.reshape(B, S, C_OUT_W).astype(h.dtype), w_out)


def moe_swiglu(h, w_router, w_gate, w_up, w_down):
    B, S, D = h.shape
    N = B * S
    A = N * MOE_TOP_K
    hf = h.reshape(N, D)
    logits = jnp.dot(hf, w_router, preferred_element_type=F32)
    top_v, top_e = lax.top_k(logits, MOE_TOP_K)
    gates = jax.nn.softmax(top_v, axis=-1)
    flat_e = top_e.reshape(A)
    flat_tok = jnp.repeat(jnp.arange(N, dtype=jnp.int32), MOE_TOP_K)
    order = jnp.argsort(flat_e)
    se, st, sg = flat_e[order], flat_tok[order], gates.reshape(A)[order]
    counts = jnp.bincount(flat_e, length=N_EXPERTS)
    starts = jnp.cumsum(counts) - counts
    pcounts = (counts + MOE_BLOCK - 1) // MOE_BLOCK * MOE_BLOCK
    pends = jnp.cumsum(pcounts)
    pstarts = pends - pcounts
    dest = pstarts[se] + jnp.arange(A) - starts[se]
    n_blocks = -(-A // MOE_BLOCK) + N_EXPERTS
    slot_tok = jnp.full((n_blocks * MOE_BLOCK,), N, jnp.int32).at[dest].set(st)
    block_e = jnp.minimum(jnp.searchsorted(pends, jnp.arange(n_blocks) * MOE_BLOCK, side='right'), N_EXPERTS - 1)
    xpad = jnp.concatenate([hf, jnp.zeros((1, D), hf.dtype)], axis=0)
    xb = xpad[slot_tok].reshape(n_blocks, MOE_BLOCK, D)

    def expert_block(args):
        xblk, e = args
        return swiglu(xblk, w_gate[e], w_up[e], w_down[e])

    yb = lax.map(expert_block, (xb, block_e)).reshape(n_blocks * MOE_BLOCK, D)
    y = jnp.zeros((N, D), yb.dtype).at[st].add(yb[dest] * sg[:, None].astype(yb.dtype))
    return y.reshape(B, S, D)


def setup_inputs(seed: int = 0) -> dict:
    key = jax.random.key(seed)
    ks = jax.random.split(key, 32)

    def nrm(i, shape, scale):
        return jax.random.normal(ks[i], shape, F32) * scale

    def gain(i, shape):
        return 1.0 + nrm(i, shape, 0.05)

    D = D_MODEL
    offs = jax.random.randint(ks[2], (BATCH, 1), 0, 1024)
    return {
        'x': nrm(0, (BATCH, SEQ, D), 1.0),
        'c': nrm(1, (BATCH, D), 1.0),
        'positions': (offs + jnp.arange(SEQ)[None, :]).astype(jnp.int32),
        'rel_bias': nrm(3, (NUM_BUCKETS, BIAS_HEADS), 0.2),
        'ada_w': nrm(4, (DEPTH, D, 6 * D), D ** -0.5),
        'ada_b': nrm(5, (DEPTH, 6 * D), 0.02),
        'mix_norm_g': gain(6, (DEPTH, D)),
        'ffn_norm_g': gain(7, (DEPTH, D)),
        'ab_w_in': nrm(8, (N_EVEN, D, AB_IN_W), D ** -0.5),
        'ab_w_out': nrm(9, (N_EVEN, AB_OUT_W, D), AB_OUT_W ** -0.5),
        'nsa_cmp_pos_k': nrm(10, (N_EVEN, CMP_LEN, HEAD_DIM), 0.1),
        'nsa_cmp_w1_k': nrm(11, (N_EVEN, CMP_LEN * HEAD_DIM, CMP_HIDDEN), (CMP_LEN * HEAD_DIM) ** -0.5),
        'nsa_cmp_w2_k': nrm(12, (N_EVEN, CMP_HIDDEN, HEAD_DIM), CMP_HIDDEN ** -0.5),
        'nsa_cmp_pos_v': nrm(13, (N_EVEN, CMP_LEN, HEAD_DIM), 0.1),
        'nsa_cmp_w1_v': nrm(14, (N_EVEN, CMP_LEN * HEAD_DIM, CMP_HIDDEN), (CMP_LEN * HEAD_DIM) ** -0.5),
        'nsa_cmp_w2_v': nrm(15, (N_EVEN, CMP_HIDDEN, HEAD_DIM), CMP_HIDDEN ** -0.5),
        'mla_q_norm_g': gain(16, (N_EVEN, Q_LORA)),
        'mla_kv_norm_g': gain(17, (N_EVEN, KV_LORA)),
        'mla_w_uq': nrm(18, (N_EVEN, Q_LORA, MLA_HEADS * (QK_NOPE + QK_ROPE)), Q_LORA ** -0.5),
        'mla_w_ukv': nrm(19, (N_EVEN, KV_LORA, MLA_HEADS * (QK_NOPE + V_DIM)), KV_LORA ** -0.5),
        'ffn_w_gate': nrm(20, (N_EVEN, D, D_FF), D ** -0.5),
        'ffn_w_up': nrm(21, (N_EVEN, D, D_FF), D ** -0.5),
        'ffn_w_down': nrm(22, (N_EVEN, D_FF, D), D_FF ** -0.5),
        'c_w_in': nrm(23, (N_ODD, D, C_IN_W), D ** -0.5),
        'c_w_out': nrm(24, (N_ODD, C_OUT_W, D), C_OUT_W ** -0.5),
        'moe_w_router': nrm(25, (N_ODD, D, N_EXPERTS), D ** -0.5),
        'moe_w_gate': nrm(26, (N_ODD, N_EXPERTS, D, D_FF_EXPERT), D ** -0.5),
        'moe_w_up': nrm(27, (N_ODD, N_EXPERTS, D, D_FF_EXPERT), D ** -0.5),
        'moe_w_down': nrm(28, (N_ODD, N_EXPERTS, D_FF_EXPERT, D), D_FF_EXPERT ** -0.5),
        'final_norm_g': gain(29, (D,)),
    }


def reference(x, c, positions, rel_bias, ada_w, ada_b, mix_norm_g, ffn_norm_g, ab_w_in, ab_w_out,
              nsa_cmp_pos_k, nsa_cmp_w1_k, nsa_cmp_w2_k, nsa_cmp_pos_v, nsa_cmp_w1_v, nsa_cmp_w2_v,
              mla_q_norm_g, mla_kv_norm_g, mla_w_uq, mla_w_ukv, ffn_w_gate, ffn_w_up, ffn_w_down,
              c_w_in, c_w_out, moe_w_router, moe_w_gate, moe_w_up, moe_w_down, final_norm_g):
    cs = jax.nn.silu(c)
    for i in range(DEPTH):
        j = i // 2
        mod = jnp.dot(cs, ada_w[i]) + ada_b[i]
        sh_m, sc_m, g_m, sh_f, sc_f, g_f = (t[:, None, :] for t in jnp.split(mod, 6, axis=-1))
        h = rms_norm(x, mix_norm_g[i]) * (1.0 + sc_m) + sh_m
        if i % 2 == 0:
            y = mixer_ab(h, positions, rel_bias, ab_w_in[j], ab_w_out[j],
                         nsa_cmp_pos_k[j], nsa_cmp_w1_k[j], nsa_cmp_w2_k[j],
                         nsa_cmp_pos_v[j], nsa_cmp_w1_v[j], nsa_cmp_w2_v[j],
                         mla_q_norm_g[j], mla_kv_norm_g[j], mla_w_uq[j], mla_w_ukv[j])
        else:
            y = mixer_c(h, rel_bias, c_w_in[j], c_w_out[j])
        x = x + g_m * y
        h = rms_norm(x, ffn_norm_g[i]) * (1.0 + sc_f) + sh_f
        if i % 2 == 0:
            y = swiglu(h, ffn_w_gate[j], ffn_w_up[j], ffn_w_down[j])
        else:
            y = moe_swiglu(h, moe_w_router[j], moe_w_gate[j], moe_w_up[j], moe_w_down[j])
        x = x + g_f * y
    return rms_norm(x, final_norm_g)
```

```python
import functools
import math

import numpy as np
import jax
import jax.numpy as jnp
from jax import lax
from jax.experimental import pallas as pl
from jax.experimental.pallas import tpu as pltpu

F32 = jnp.float32
BF16 = jnp.bfloat16

D_MODEL = 2048
HEAD_DIM = 128
NEG_INF = -1e30
RMS_EPS = 1e-6
NUM_BUCKETS = 32
T5_MAX_DIST = 2048
NSA_HEADS = 8
NSA_KV_HEADS = 2
NSA_GROUP = NSA_HEADS // NSA_KV_HEADS
CMP_LEN = 32
CMP_STRIDE = 16
SLC_LEN = 64
N_SEL = 16
WIN = 512
FORCE_SCORE = 1e9
MLA_HEADS = 8
Q_LORA = 512
KV_LORA = 512
QK_NOPE = 128
QK_ROPE = 64
V_DIM = 128
ROPE_THETA = 10000.0
DIL_PATTERNS = ((128, 1), (512, 4), (2048, 16))
DIL_HEADS = 8
D_FF = 5632
N_EXPERTS = 8
MOE_TOP_K = 2
D_FF_EXPERT = 7168

LANES = 128
MXU_DIM = 256
VMEM_LIMIT_BYTES = 56 * 1024 * 1024

Z0_Q = 0
Z0_KV = Z0_Q + NSA_HEADS * HEAD_DIM
Z0_KVLAT = Z0_KV + 6 * NSA_KV_HEADS * HEAD_DIM
Z0_MISC = Z0_KVLAT + KV_LORA
Z0_QLAT = Z0_MISC + LANES
Z0_W = Z0_QLAT + Q_LORA + LANES
LAT_BLOCK = KV_LORA + LANES
GATE_LANE0 = QK_ROPE


def _cparams(sem, vmem=VMEM_LIMIT_BYTES):
    return pltpu.CompilerParams(dimension_semantics=sem, vmem_limit_bytes=vmem)


def _dot(a, b):
    return jnp.dot(a, b, preferred_element_type=F32)


def _dot_nt(a, b):
    return lax.dot_general(a, b, (((1,), (1,)), ((), ())), preferred_element_type=F32)


def _sigmoid(x):
    return 1.0 / (1.0 + jnp.exp(-x))


def _t5_bucket(dist):
    n = jnp.maximum(dist, 0)
    max_exact = NUM_BUCKETS // 2
    nf = jnp.maximum(n, 1).astype(F32)
    large = max_exact + (jnp.log(nf / max_exact) / math.log(T5_MAX_DIST / max_exact)
                         * (NUM_BUCKETS - max_exact)).astype(jnp.int32)
    large = jnp.minimum(large, NUM_BUCKETS - 1)
    return jnp.where(n < max_exact, n, large)


def _ada_kernel(c_ref, w_ref, b_ref, o_ref):
    c = c_ref[...]
    cs = c * _sigmoid(c)
    o_ref[...] = _dot(cs.astype(BF16), w_ref[...].astype(BF16)) + b_ref[...]


def _ada_mod(c, ada_w, ada_b):
    depth, d, n = ada_w.shape
    b = c.shape[0]
    bp = 8
    cpad = jnp.zeros((bp, d), F32).at[:b].set(c)
    tn = 1024
    out = pl.pallas_call(
        _ada_kernel,
        out_shape=jax.ShapeDtypeStruct((depth, bp, n), F32),
        grid=(depth, n // tn),
        in_specs=[pl.BlockSpec((bp, d), lambda l, j: (0, 0)),
                  pl.BlockSpec((None, d, tn), lambda l, j: (l, 0, j)),
                  pl.BlockSpec((None, 1, tn), lambda l, j: (l, 0, j))],
        out_specs=pl.BlockSpec((None, bp, tn), lambda l, j: (l, 0, j)),
        compiler_params=_cparams(("parallel", "parallel")),
        name="ada_mod",
    )(cpad, ada_w, ada_b.reshape(depth, 1, n))
    return out[:, :b]


def _nmm_kernel(x_ref, g_ref, s_ref, w_ref, *rest, norm_cols, rope_tiles, tn):
    if rope_tiles:
        c_ref, s1_ref, s2_ref, o_ref, h_scr = rest
    else:
        o_ref, h_scr = rest
    j = pl.program_id(1)

    @pl.when(j == 0)
    def _():
        x = x_ref[...].astype(F32)
        xn = x[:, :norm_cols]
        ms = jnp.mean(xn * xn, axis=-1, keepdims=True)
        hn = xn * lax.rsqrt(ms + RMS_EPS) * g_ref[...] + s_ref[...]
        h_scr[:, :norm_cols] = hn.astype(BF16)
        if norm_cols < x.shape[1]:
            h_scr[:, norm_cols:] = x[:, norm_cols:].astype(BF16)

    acc = _dot(h_scr[...], w_ref[...])
    o_ref[...] = acc.astype(o_ref.dtype)
    if rope_tiles:
        @pl.when(j < rope_tiles)
        def _():
            for hh in range(tn // MXU_DIM):
                lo = hh * MXU_DIM + LANES
                y = acc[:, lo:lo + LANES]
                y2 = (y * c_ref[...] + pltpu.roll(y, LANES - 32, 1) * s1_ref[...]
                      + pltpu.roll(y, 32, 1) * s2_ref[...])
                o_ref[:, lo:lo + LANES] = y2.astype(o_ref.dtype)


def _nmm(x2d, geff, shift, w, *, seq, tm, tn, norm_cols, x_block, out_dtype, rope=None,
         rope_tiles=0, name):
    m = x2d.shape[0]
    k, n = w.shape
    tpb = seq // tm
    in_specs = [pl.BlockSpec((tm, k), lambda i, j: (i, x_block)),
                pl.BlockSpec((None, 1, norm_cols), lambda i, j: (i // tpb, 0, 0)),
                pl.BlockSpec((None, 1, norm_cols), lambda i, j: (i // tpb, 0, 0)),
                pl.BlockSpec((k, tn), lambda i, j: (0, j))]
    args = [x2d, geff, shift, w]
    if rope_tiles:
        for t in rope:
            in_specs.append(pl.BlockSpec((tm, LANES), lambda i, j: (i, 0)))
            args.append(t)
    return pl.pallas_call(
        functools.partial(_nmm_kernel, norm_cols=norm_cols, rope_tiles=rope_tiles, tn=tn),
        out_shape=jax.ShapeDtypeStruct((m, n), out_dtype),
        grid=(m // tm, n // tn),
        in_specs=in_specs,
        out_specs=pl.BlockSpec((tm, tn), lambda i, j: (i, j)),
        scratch_shapes=[pltpu.VMEM((tm, k), BF16)],
        compiler_params=_cparams(("parallel", "arbitrary")),
        name=name,
    )(*args)


def _cmp_kernel(x_ref, pos_ref, w1_ref, w2_ref, o_ref):
    half = w1_ref.shape[0] // 2
    x = x_ref[...]
    n16 = x.shape[0]
    a = _dot(x, w1_ref[:half, :])
    b = _dot(x, w1_ref[half:, :])
    c = _dot(pos_ref[...], w1_ref[...])
    hid = a + pltpu.roll(b, n16 - 1, 0) + c
    hid = jax.nn.gelu(hid, approximate=True)
    o_ref[...] = _dot(hid.astype(BF16), w2_ref[...]).astype(o_ref.dtype)


def _nsa_compress(x16, pos, w1, w2):
    b, nkv, n16, kk = x16.shape
    g = NSA_KV_HEADS
    return pl.pallas_call(
        _cmp_kernel,
        out_shape=jax.ShapeDtypeStruct((b, nkv, n16, HEAD_DIM), BF16),
        grid=(b, nkv),
        in_specs=[pl.BlockSpec((None, None, n16, kk), lambda i, j: (i, j, 0, 0)),
                  pl.BlockSpec((None, 1, 2 * kk), lambda i, j: (j // g, 0, 0)),
                  pl.BlockSpec((None, 2 * kk, HEAD_DIM), lambda i, j: (j // g, 0, 0)),
                  pl.BlockSpec((None, HEAD_DIM, HEAD_DIM), lambda i, j: (j // g, 0, 0))],
        out_specs=pl.BlockSpec((None, None, n16, HEAD_DIM), lambda i, j: (i, j, 0, 0)),
        compiler_params=_cparams(("parallel", "parallel")),
        name="nsa_compress",
    )(x16, pos, w1, w2)


def _cmpattn_kernel(q_ref, kv_ref, bias_ref, cmat_ref, misc_ref, o_ref, sel_ref, *, tq, n_slc,
                    n_top, scale):
    qi = pl.program_id(1)
    g_n, r_n = NSA_KV_HEADS, NSA_GROUP
    ncp = kv_ref.shape[1]
    trow = qi * tq + lax.broadcasted_iota(jnp.int32, (tq, 1), 0)
    has_c = trow >= (CMP_LEN - 1)
    blk_t = lax.shift_right_logical(trow, int(math.log2(SLC_LEN))).astype(F32)
    jb = lax.broadcasted_iota(jnp.int32, (tq, n_slc), 1).astype(F32)
    forced = (jb == 0.0) | (jb == blk_t) | (jb == blk_t - 1.0)
    valid = jb <= blk_t
    misc = misc_ref[...].astype(F32)
    for g in range(g_n):
        kc = kv_ref[g]
        vc = kv_ref[g_n + g]
        imp = jnp.zeros((tq, ncp), F32)
        for r in range(r_n):
            h = g * r_n + r
            q = q_ref[:, h * HEAD_DIM:(h + 1) * HEAD_DIM]
            s = _dot_nt(q, kc) * scale + bias_ref[h]
            m = jnp.max(s, axis=-1, keepdims=True)
            e = jnp.exp(s - m)
            p = e * (1.0 / jnp.sum(e, axis=-1, keepdims=True))
            p = jnp.where(has_c, p, 0.0)
            imp = imp + p
            o = _dot(p.astype(BF16), vc)
            c0 = GATE_LANE0 + h * 3
            gate = _sigmoid(misc[:, c0:c0 + 1])
            o_ref[:, h * HEAD_DIM:(h + 1) * HEAD_DIM] = gate * o
        imp_s = jnp.dot(imp, cmat_ref[...], precision=lax.Precision.HIGHEST,
                        preferred_element_type=F32)
        score = jnp.where(forced, FORCE_SCORE, jnp.where(valid, imp_s, -1.0))
        sel = jnp.zeros((tq, n_slc), F32)
        for _ in range(n_top):
            mx = jnp.max(score, axis=-1, keepdims=True)
            first = jnp.min(jnp.where(score == mx, jb, float(n_slc)), axis=-1, keepdims=True)
            hit = jb == first
            sel = jnp.where(hit, jnp.where(mx > -0.5, 1.0, 0.0), sel)
            score = jnp.where(hit, -3e38, score)
        sel_ref[g] = sel.astype(sel_ref.dtype)


def _nsa_cmp_attention(z0, kvc, bias_c, cmat, *, tq):
    b, s, _ = z0.shape
    ncp = kvc.shape[2]
    n_slc = s // SLC_LEN
    n_top = min(N_SEL, n_slc)
    qw = NSA_HEADS * HEAD_DIM
    return pl.pallas_call(
        functools.partial(_cmpattn_kernel, tq=tq, n_slc=n_slc, n_top=n_top, scale=HEAD_DIM ** -0.5),
        out_shape=(jax.ShapeDtypeStruct((b, s, qw), F32),
                   jax.ShapeDtypeStruct((b, NSA_KV_HEADS, s, n_slc), BF16)),
        grid=(b, s // tq),
        in_specs=[pl.BlockSpec((None, tq, qw), lambda i, j: (i, j, Z0_Q // qw)),
                  pl.BlockSpec((None, 2 * NSA_KV_HEADS, ncp, HEAD_DIM), lambda i, j: (i, 0, 0, 0)),
                  pl.BlockSpec((NSA_HEADS, tq, ncp), lambda i, j: (0, j, 0)),
                  pl.BlockSpec((ncp, n_slc), lambda i, j: (0, 0)),
                  pl.BlockSpec((None, tq, LANES), lambda i, j: (i, j, Z0_MISC // LANES))],
        out_specs=(pl.BlockSpec((None, tq, qw), lambda i, j: (i, j, 0)),
                   pl.BlockSpec((None, NSA_KV_HEADS, tq, n_slc), lambda i, j: (i, 0, j, 0))),
        compiler_params=_cparams(("parallel", "parallel")),
        name="nsa_cmp_attn",
    )(z0, kvc, bias_c, cmat, z0)


def _flash_kernel(qi_ref, ki_ref, q_ref, k_ref, v_ref, *rest, t, n_kvh, r_n, dk, scale, has_bias,
                  has_sel, gate_branch):
    rest = list(rest)
    bias_ref = rest.pop(0) if has_bias else None
    sel_ref = rest.pop(0) if has_sel else None
    if gate_branch is not None:
        misc_ref = rest.pop(0)
        prev_ref = rest.pop(0)
    o_ref, m_scr, l_scr, acc_scr = rest
    pidx = pl.program_id(1)
    qi = qi_ref[pidx]
    ki = ki_ref[pidx]

    @pl.when(ki == 0)
    def _():
        m_scr[...] = jnp.full(m_scr.shape, NEG_INF, F32)
        l_scr[...] = jnp.zeros(l_scr.shape, F32)
        acc_scr[...] = jnp.zeros(acc_scr.shape, F32)

    def step(diag):
        if has_sel:
            per = t // SLC_LEN
            erow = lax.broadcasted_iota(jnp.int32, (sel_ref.shape[2], t), 0)
            ecol = lax.broadcasted_iota(jnp.int32, (sel_ref.shape[2], t), 1)
            expand = jnp.where(erow == ki * per + lax.shift_right_logical(ecol, int(math.log2(SLC_LEN))),
                               1.0, 0.0).astype(BF16)
        if diag:
            causal = (lax.broadcasted_iota(jnp.int32, (t, t), 0)
                      >= lax.broadcasted_iota(jnp.int32, (t, t), 1))
        for kh in range(n_kvh):
            k = k_ref[:, kh * dk:(kh + 1) * dk]
            v = v_ref[:, kh * HEAD_DIM:(kh + 1) * HEAD_DIM]
            if has_sel:
                selm = _dot(sel_ref[kh], expand) > 0.5
            for r in range(r_n):
                h = kh * r_n + r
                q = q_ref[:, h * dk:(h + 1) * dk]
                s = _dot_nt(q, k) * scale
                if has_bias:
                    s = s + bias_ref[h]
                if has_sel:
                    s = jnp.where(selm, s, NEG_INF)
                if diag:
                    s = jnp.where(causal, s, NEG_INF)
                m_prev = m_scr[h]
                m_new = jnp.maximum(m_prev, jnp.max(s, axis=-1, keepdims=True))
                alpha = jnp.exp(m_prev - m_new)
                p = jnp.exp(s - m_new)
                l_scr[h] = alpha * l_scr[h] + jnp.sum(p, axis=-1, keepdims=True)
                acc_scr[h] = alpha * acc_scr[h] + _dot(p.astype(BF16), v)
                m_scr[h] = m_new

    @pl.when(ki < qi)
    def _():
        step(False)

    @pl.when(ki == qi)
    def _():
        step(True)
        if gate_branch is not None:
            misc = misc_ref[...].astype(F32)
        for h in range(n_kvh * r_n):
            o = acc_scr[h] * (1.0 / l_scr[h])
            sl = slice(h * HEAD_DIM, (h + 1) * HEAD_DIM)
            if gate_branch is not None:
                c0 = GATE_LANE0 + h * 3 + gate_branch
                o = prev_ref[:, sl] + _sigmoid(misc[:, c0:c0 + 1]) * o
            o_ref[:, sl] = o.astype(o_ref.dtype)


def _tri_pairs(nq):
    qi = np.concatenate([np.full(i + 1, i, np.int32) for i in range(nq)])
    ki = np.concatenate([np.arange(i + 1, dtype=np.int32) for i in range(nq)])
    return jnp.asarray(qi), jnp.asarray(ki)


def _flash(q_arr, k_arr, v_arr, *, t, n_kvh, r_n, dk, q_blk, k_blk, v_blk, scale, out_dtype,
           bias=None, sel=None, misc=None, misc_blk=0, prev=None, gate_branch=None, name):
    b, s, _ = q_arr.shape
    nq = s // t
    qi_a, ki_a = _tri_pairs(nq)
    nh = n_kvh * r_n
    ow = nh * HEAD_DIM
    in_specs = [pl.BlockSpec((None, t, nh * dk), lambda i, p, qa, ka: (i, qa[p], q_blk)),
                pl.BlockSpec((None, t, n_kvh * dk), lambda i, p, qa, ka: (i, ka[p], k_blk)),
                pl.BlockSpec((None, t, n_kvh * HEAD_DIM), lambda i, p, qa, ka: (i, ka[p], v_blk))]
    args = [q_arr, k_arr, v_arr]
    if bias is not None:
        nd = bias.shape[1]
        in_specs.append(pl.BlockSpec((nh, None, t, t),
                                     lambda i, p, qa, ka: (0, jnp.minimum(qa[p] - ka[p], nd - 1), 0, 0)))
        args.append(bias)
    if sel is not None:
        n_slc = sel.shape[-1]
        in_specs.append(pl.BlockSpec((None, n_kvh, t, n_slc), lambda i, p, qa, ka: (i, 0, qa[p], 0)))
        args.append(sel)
    io_alias = {}
    if gate_branch is not None:
        in_specs.append(pl.BlockSpec((None, t, LANES), lambda i, p, qa, ka: (i, qa[p], misc_blk)))
        args.append(misc)
        in_specs.append(pl.BlockSpec((None, t, ow), lambda i, p, qa, ka: (i, qa[p], 0)))
        args.append(prev)
        io_alias = {2 + len(args) - 1: 0}
    return pl.pallas_call(
        functools.partial(_flash_kernel, t=t, n_kvh=n_kvh, r_n=r_n, dk=dk, scale=scale,
                          has_bias=bias is not None, has_sel=sel is not None, gate_branch=gate_branch),
        out_shape=jax.ShapeDtypeStruct((b, s, ow), out_dtype),
        grid_spec=pltpu.PrefetchScalarGridSpec(
            num_scalar_prefetch=2,
            grid=(b, int(qi_a.shape[0])),
            in_specs=in_specs,
            out_specs=pl.BlockSpec((None, t, ow), lambda i, p, qa, ka: (i, qa[p], 0)),
            scratch_shapes=[pltpu.VMEM((nh, t, 1), F32), pltpu.VMEM((nh, t, 1), F32),
                            pltpu.VMEM((nh, t, HEAD_DIM), F32)]),
        input_output_aliases=io_alias,
        compiler_params=_cparams(("parallel", "arbitrary")),
        name=name,
    )(qi_a, ki_a, *args)


def _band_kernel(q_ref, *rest, tq, pb, npv, n_kvh, r_n, scale, gate_branch, want_lse, qi_axis):
    rest = list(rest)
    kp = [rest.pop(0) for _ in range(npv)]
    kc = rest.pop(0)
    vp = [rest.pop(0) for _ in range(npv)]
    vc = rest.pop(0)
    bias_ref = rest.pop(0)
    if gate_branch is not None:
        misc_ref = rest.pop(0)
        prev_ref = rest.pop(0)
    o_ref = rest.pop(0)
    lse_ref = rest.pop(0) if want_lse else None
    qi = pl.program_id(qi_axis)
    nblk = tq // pb
    kw_prev = npv * pb
    if gate_branch is not None:
        misc = misc_ref[...].astype(F32)
    if want_lse:
        lane = lax.broadcasted_iota(jnp.int32, (tq, LANES), 1)
        lse_tile = jnp.zeros((tq, LANES), F32)
    for kh in range(n_kvh):
        ksl = slice(kh * HEAD_DIM, (kh + 1) * HEAD_DIM)
        for r in range(r_n):
            h = kh * r_n + r
            hsl = slice(h * HEAD_DIM, (h + 1) * HEAD_DIM)
            q = q_ref[:, hsl]
            parts = []
            for n in range(npv):
                pen = jnp.where(qi * nblk - npv + n >= 0, 0.0, NEG_INF)
                parts.append(_dot_nt(q, kp[n][:, ksl]) * scale + bias_ref[h, :, n * pb:(n + 1) * pb] + pen)
            parts.append(_dot_nt(q, kc[:, ksl]) * scale + bias_ref[h, :, kw_prev:])
            m = parts[0].max(axis=-1, keepdims=True)
            for sp in parts[1:]:
                m = jnp.maximum(m, sp.max(axis=-1, keepdims=True))
            l = jnp.zeros((tq, 1), F32)
            o = jnp.zeros((tq, HEAD_DIM), F32)
            for n, sp in enumerate(parts):
                p = jnp.exp(sp - m)
                l = l + jnp.sum(p, axis=-1, keepdims=True)
                vv = vp[n][:, ksl] if n < npv else vc[:, ksl]
                o = o + _dot(p.astype(BF16), vv)
            o = o * (1.0 / l)
            if gate_branch is not None:
                c0 = GATE_LANE0 + h * 3 + gate_branch
                o = prev_ref[:, hsl] + _sigmoid(misc[:, c0:c0 + 1]) * o
            o_ref[:, hsl] = o.astype(o_ref.dtype)
            if want_lse:
                lse_tile = jnp.where(lane == h, m + jnp.log(l), lse_tile)
    if want_lse:
        lse_ref[...] = lse_tile


def _band(q_arr, kv_arr, bias, *, lead_grid, tq, pb, npv, n_kvh, r_n, qmap, kmap, vmap, omap,
          lmap=None, out_shape, lse_shape=None, misc=None, miscmap=None, prev=None, gate_branch=None,
          scale, name):
    nl = len(lead_grid)
    nq = q_arr.shape[1] // tq
    nblk = tq // pb
    nh = n_kvh * r_n
    qw = nh * HEAD_DIM
    kw = n_kvh * HEAD_DIM

    def rows_cur(fn):
        def im(*g):
            bb, cc = fn(*g)
            return (bb, g[nl], cc)
        return im

    def rows_prev(fn, n):
        def im(*g):
            bb, cc = fn(*g)
            return (bb, jnp.maximum(g[nl] * nblk - npv + n, 0), cc)
        return im

    in_specs = [pl.BlockSpec((None, tq, qw), rows_cur(qmap))]
    args = [q_arr]
    for fn in (kmap, vmap):
        for n in range(npv):
            in_specs.append(pl.BlockSpec((None, pb, kw), rows_prev(fn, n)))
            args.append(kv_arr)
        in_specs.append(pl.BlockSpec((None, tq, kw), rows_cur(fn)))
        args.append(kv_arr)
    in_specs.append(pl.BlockSpec(bias.shape, lambda *g: (0, 0, 0)))
    args.append(bias)
    io_alias = {}
    if gate_branch is not None:
        in_specs.append(pl.BlockSpec((None, tq, LANES), rows_cur(miscmap)))
        args.append(misc)
        in_specs.append(pl.BlockSpec((None, tq, qw), rows_cur(omap)))
        args.append(prev)
        io_alias = {len(args) - 1: 0}
    out_shapes = [out_shape]
    out_specs = [pl.BlockSpec((None, tq, qw), rows_cur(omap))]
    if lse_shape is not None:
        out_shapes.append(lse_shape)
        out_specs.append(pl.BlockSpec((None, tq, LANES), rows_cur(lmap)))
    res = pl.pallas_call(
        functools.partial(_band_kernel, tq=tq, pb=pb, npv=npv, n_kvh=n_kvh, r_n=r_n, scale=scale,
                          gate_branch=gate_branch, want_lse=lse_shape is not None, qi_axis=nl),
        out_shape=tuple(out_shapes),
        grid=tuple(lead_grid) + (nq,),
        in_specs=in_specs,
        out_specs=tuple(out_specs),
        input_output_aliases=io_alias,
        compiler_params=_cparams(("parallel",) * (nl + 1)),
        name=name,
    )(*args)
    return res


def _band_bias(tab, tq, kw_prev, max_back, dist_scale):
    kw = kw_prev + tq
    rel = jnp.arange(tq)[:, None] + kw_prev - jnp.arange(kw)[None, :]
    ok = (rel >= 0) & (rel <= max_back)
    bias = tab[_t5_bucket(rel * dist_scale)].astype(F32).transpose(2, 0, 1)
    return jnp.where(ok[None], bias, NEG_INF)


def _out_ab_kernel(oa_ref, ob_ref, wa_ref, wb_ref, x_ref, g_ref, o_ref):
    y = _dot(oa_ref[...].astype(BF16), wa_ref[...]) + _dot(ob_ref[...], wb_ref[...])
    o_ref[...] = x_ref[...] + g_ref[...] * y


def _out_ab(oa, ob, w, x2d, gate, *, seq, tm):
    m, d = x2d.shape
    ka = oa.shape[1]
    kb = ob.shape[1]
    tpb = seq // tm
    return pl.pallas_call(
        _out_ab_kernel,
        out_shape=jax.ShapeDtypeStruct((m, d), F32),
        grid=(m // tm,),
        in_specs=[pl.BlockSpec((tm, ka), lambda i: (i, 0)),
                  pl.BlockSpec((tm, kb), lambda i: (i, 0)),
                  pl.BlockSpec((ka, d), lambda i: (0, 0)),
                  pl.BlockSpec((kb, d), lambda i: (ka // kb, 0)),
                  pl.BlockSpec((tm, d), lambda i: (i, 0)),
                  pl.BlockSpec((None, 1, d), lambda i: (i // tpb, 0, 0))],
        out_specs=pl.BlockSpec((tm, d), lambda i: (i, 0)),
        compiler_params=_cparams(("parallel",)),
        name="out_proj_ab",
    )(oa, ob, w, w, x2d, gate)


def _out_c_kernel(o0_ref, o1_ref, o2_ref, l0_ref, l1_ref, l2_ref, w_ref, x_ref, g_ref, o_ref, mrg_scr):
    l0, l1, l2 = l0_ref[...], l1_ref[...], l2_ref[...]
    mx = jnp.maximum(jnp.maximum(l0, l1), l2)
    e0, e1, e2 = jnp.exp(l0 - mx), jnp.exp(l1 - mx), jnp.exp(l2 - mx)
    inv = 1.0 / (e0 + e1 + e2)
    w0, w1, w2 = e0 * inv, e1 * inv, e2 * inv
    for h in range(DIL_HEADS):
        sl = slice(h * HEAD_DIM, (h + 1) * HEAD_DIM)
        mg = (w0[:, h:h + 1] * o0_ref[:, sl] + w1[:, h:h + 1] * o1_ref[:, sl]
              + w2[:, h:h + 1] * o2_ref[:, sl])
        mrg_scr[:, sl] = mg.astype(BF16)
    o_ref[...] = x_ref[...] + g_ref[...] * _dot(mrg_scr[...], w_ref[...])


def _out_c(os_, lses, w, x2d, gate, *, seq, tm):
    m, d = x2d.shape
    kc = w.shape[0]
    tpb = seq // tm
    return pl.pallas_call(
        _out_c_kernel,
        out_shape=jax.ShapeDtypeStruct((m, d), F32),
        grid=(m // tm,),
        in_specs=[pl.BlockSpec((tm, kc), lambda i: (i, 0))] * 3
        + [pl.BlockSpec((tm, LANES), lambda i: (i, 0))] * 3
        + [pl.BlockSpec((kc, d), lambda i: (0, 0)),
           pl.BlockSpec((tm, d), lambda i: (i, 0)),
           pl.BlockSpec((None, 1, d), lambda i: (i // tpb, 0, 0))],
        out_specs=pl.BlockSpec((tm, d), lambda i: (i, 0)),
        scratch_shapes=[pltpu.VMEM((tm, kc), BF16)],
        compiler_params=_cparams(("parallel",)),
        name="out_proj_c",
    )(*os_, *lses, w, x2d, gate)


def _ffn_kernel(x_ref, g_ref, s_ref, wg_ref, wu_ref, wd_ref, gate_ref, o_ref, h_scr, acc_scr):
    f = pl.program_id(1)

    @pl.when(f == 0)
    def _():
        x = x_ref[...]
        ms = jnp.mean(x * x, axis=-1, keepdims=True)
        h_scr[...] = (x * lax.rsqrt(ms + RMS_EPS) * g_ref[...] + s_ref[...]).astype(BF16)
        acc_scr[...] = jnp.zeros(acc_scr.shape, F32)

    h = h_scr[...]
    a = _dot(h, wg_ref[...])
    b = _dot(h, wu_ref[...])
    hid = (a * _sigmoid(a) * b).astype(BF16)
    acc_scr[...] += _dot(hid, wd_ref[...])

    @pl.when(f == pl.num_programs(1) - 1)
    def _():
        o_ref[...] = x_ref[...] + gate_ref[...] * acc_scr[...]


def _ffn(x2d, geff, shift, wg, wu, wd, gate, *, seq, tm, tf):
    m, d = x2d.shape
    ff = wg.shape[1]
    tpb = seq // tm
    vec = pl.BlockSpec((None, 1, d), lambda i, f: (i // tpb, 0, 0))
    return pl.pallas_call(
        _ffn_kernel,
        out_shape=jax.ShapeDtypeStruct((m, d), F32),
        grid=(m // tm, ff // tf),
        in_specs=[pl.BlockSpec((tm, d), lambda i, f: (i, 0)), vec, vec,
                  pl.BlockSpec((d, tf), lambda i, f: (0, f)),
                  pl.BlockSpec((d, tf), lambda i, f: (0, f)),
                  pl.BlockSpec((tf, d), lambda i, f: (f, 0)),
                  vec],
        out_specs=pl.BlockSpec((tm, d), lambda i, f: (i, 0)),
        scratch_shapes=[pltpu.VMEM((tm, d), BF16), pltpu.VMEM((tm, d), F32)],
        compiler_params=_cparams(("parallel", "arbitrary")),
        name="ffn_swiglu",
    )(x2d, geff, shift, wg, wu, wd, gate)


def _moe_prep_kernel(x_ref, g_ref, s_ref, wr_ref, h_ref, route_ref):
    x = x_ref[...]
    ms = jnp.mean(x * x, axis=-1, keepdims=True)
    h = x * lax.rsqrt(ms + RMS_EPS) * g_ref[...] + s_ref[...]
    h_ref[...] = h
    logits = jnp.dot(h, wr_ref[...], precision=lax.Precision.HIGHEST, preferred_element_type=F32)
    lane = lax.broadcasted_iota(jnp.int32, logits.shape, 1).astype(F32)
    lg = jnp.where(lane < float(N_EXPERTS), logits, -3e38)
    m1 = jnp.max(lg, axis=-1, keepdims=True)
    i1 = jnp.min(jnp.where(lg == m1, lane, float(LANES)), axis=-1, keepdims=True)
    lg2 = jnp.where(lane == i1, -3e38, lg)
    m2 = jnp.max(lg2, axis=-1, keepdims=True)
    i2 = jnp.min(jnp.where(lg2 == m2, lane, float(LANES)), axis=-1, keepdims=True)
    e = jnp.exp(m2 - m1)
    inv = 1.0 / (1.0 + e)
    route = jnp.where(lane == 0.0, i1, jnp.where(lane == 1.0, i2, jnp.where(lane == 2.0, inv,
                      jnp.where(lane == 3.0, e * inv, 0.0))))
    route_ref[...] = route


def _moe_prep(x2d, geff, shift, w_router_pad, *, seq, tm):
    m, d = x2d.shape
    tpb = seq // tm
    vec = pl.BlockSpec((None, 1, d), lambda i: (i // tpb, 0, 0))
    return pl.pallas_call(
        _moe_prep_kernel,
        out_shape=(jax.ShapeDtypeStruct((m, d), F32), jax.ShapeDtypeStruct((m, LANES), F32)),
        grid=(m // tm,),
        in_specs=[pl.BlockSpec((tm, d), lambda i: (i, 0)), vec, vec,
                  pl.BlockSpec((d, LANES), lambda i: (0, 0))],
        out_specs=(pl.BlockSpec((tm, d), lambda i: (i, 0)), pl.BlockSpec((tm, LANES), lambda i: (i, 0))),
        compiler_params=_cparams(("parallel",)),
        name="moe_prep",
    )(x2d, geff, shift, w_router_pad)


def _row_copy(src_hbm, row, dst_vmem, slot, sem):
    return pltpu.make_async_copy(src_hbm.at[pl.ds(row, 1)], dst_vmem.at[pl.ds(slot, 1)], sem)


def _gather_kernel(idx_ref, h_hbm, o_ref, buf, sem, *, tg):
    t = pl.program_id(0)

    def start(r, c):
        _row_copy(h_hbm, idx_ref[t, r], buf, r, sem).start()
        return c

    def wait(r, c):
        _row_copy(h_hbm, 0, buf, r, sem).wait()
        return c

    lax.fori_loop(0, tg, start, 0)
    lax.fori_loop(0, tg, wait, 0)
    o_ref[...] = buf[...].astype(o_ref.dtype)


def _moe_gather(h2d, slot_tok, *, tg):
    n_tiles = slot_tok.shape[0]
    d = h2d.shape[1]
    return pl.pallas_call(
        functools.partial(_gather_kernel, tg=tg),
        out_shape=jax.ShapeDtypeStruct((n_tiles * tg, d), BF16),
        grid_spec=pltpu.PrefetchScalarGridSpec(
            num_scalar_prefetch=1,
            grid=(n_tiles,),
            in_specs=[pl.BlockSpec(memory_space=pl.ANY)],
            out_specs=pl.BlockSpec((tg, d), lambda t, idx: (t, 0)),
            scratch_shapes=[pltpu.VMEM((tg, d), F32), pltpu.SemaphoreType.DMA(())]),
        compiler_params=_cparams(("arbitrary",)),
        name="moe_gather",
    )(slot_tok, h2d)


def _expert_kernel(te_ref, nu_ref, x_ref, wg_ref, wu_ref, wd_ref, o_ref, acc_scr):
    t = pl.program_id(0)
    f = pl.program_id(1)
    live = t < nu_ref[0]

    @pl.when(live & (f == 0))
    def _():
        acc_scr[...] = jnp.zeros(acc_scr.shape, F32)

    @pl.when(live)
    def _():
        x = x_ref[...]
        a = _dot(x, wg_ref[...])
        b = _dot(x, wu_ref[...])
        hid = (a * _sigmoid(a) * b).astype(BF16)
        acc_scr[...] += _dot(hid, wd_ref[...])

    @pl.when(live & (f == pl.num_programs(1) - 1))
    def _():
        o_ref[...] = acc_scr[...]

    @pl.when(jnp.logical_not(live) & (f == pl.num_programs(1) - 1))
    def _():
        o_ref[...] = jnp.zeros(o_ref.shape, o_ref.dtype)


def _moe_experts(xg, tile_e, n_used, wg, wu, wd, *, tm, tf):
    n_slots, d = xg.shape
    n_tiles = n_slots // tm
    ff = wg.shape[2]
    nf = ff // tf

    def row_blk(t, f, te, nu):
        return (jnp.minimum(t, nu[0] - 1), 0)

    def f_blk(t, f, nu):
        return jnp.where(t < nu[0], f, nf - 1)

    return pl.pallas_call(
        _expert_kernel,
        out_shape=jax.ShapeDtypeStruct((n_slots, d), F32),
        grid_spec=pltpu.PrefetchScalarGridSpec(
            num_scalar_prefetch=2,
            grid=(n_tiles, nf),
            in_specs=[pl.BlockSpec((tm, d), row_blk),
                      pl.BlockSpec((None, d, tf), lambda t, f, te, nu: (te[t], 0, f_blk(t, f, nu))),
                      pl.BlockSpec((None, d, tf), lambda t, f, te, nu: (te[t], 0, f_blk(t, f, nu))),
                      pl.BlockSpec((None, tf, d), lambda t, f, te, nu: (te[t], f_blk(t, f, nu), 0))],
            out_specs=pl.BlockSpec((tm, d), lambda t, f, te, nu: (t, 0)),
            scratch_shapes=[pltpu.VMEM((tm, d), F32)]),
        compiler_params=_cparams(("arbitrary", "arbitrary")),
        name="moe_experts",
    )(tile_e, n_used, xg, wg, wu, wd)


def _combine_kernel(d1_ref, d2_ref, yb_hbm, route_ref, x_ref, gate_ref, fg_ref, o_ref, b1, b2, sem, *, tmc):
    t = pl.program_id(0)

    def start(r, c):
        _row_copy(yb_hbm, d1_ref[t, r], b1, r, sem).start()
        _row_copy(yb_hbm, d2_ref[t, r], b2, r, sem).start()
        return c

    def wait(r, c):
        _row_copy(yb_hbm, 0, b1, r, sem).wait()
        _row_copy(yb_hbm, 0, b2, r, sem).wait()
        return c

    lax.fori_loop(0, tmc, start, 0)
    lax.fori_loop(0, tmc, wait, 0)
    route = route_ref[...]
    y = route[:, 2:3] * b1[...] + route[:, 3:4] * b2[...]
    xo = x_ref[...] + gate_ref[...] * y
    ms = jnp.mean(xo * xo, axis=-1, keepdims=True)
    o_ref[...] = xo * lax.rsqrt(ms + RMS_EPS) * fg_ref[...]


def _moe_combine(yb, dest1, dest2, route, x2d, gate, final_g, *, seq, tmc):
    m, d = x2d.shape
    tpb = seq // tmc
    return pl.pallas_call(
        functools.partial(_combine_kernel, tmc=tmc),
        out_shape=jax.ShapeDtypeStruct((m, d), F32),
        grid_spec=pltpu.PrefetchScalarGridSpec(
            num_scalar_prefetch=2,
            grid=(m // tmc,),
            in_specs=[pl.BlockSpec(memory_space=pl.ANY),
                      pl.BlockSpec((tmc, LANES), lambda t, a, b: (t, 0)),
                      pl.BlockSpec((tmc, d), lambda t, a, b: (t, 0)),
                      pl.BlockSpec((None, 1, d), lambda t, a, b: (t // tpb, 0, 0)),
                      pl.BlockSpec((1, d), lambda t, a, b: (0, 0))],
            out_specs=pl.BlockSpec((tmc, d), lambda t, a, b: (t, 0)),
            scratch_shapes=[pltpu.VMEM((tmc, d), F32), pltpu.VMEM((tmc, d), F32),
                            pltpu.SemaphoreType.DMA(())]),
        compiler_params=_cparams(("arbitrary",)),
        name="moe_combine",
    )(dest1, dest2, yb, route, x2d, gate, final_g)


def _moe_plan(route, *, tm):
    n = route.shape[0]
    a = n * MOE_TOP_K
    flat_e = route[:, :MOE_TOP_K].astype(jnp.int32).reshape(a)
    onehot = (flat_e[:, None] == jnp.arange(N_EXPERTS, dtype=jnp.int32)[None, :]).astype(jnp.int32)
    csum = jnp.cumsum(onehot, axis=0)
    pos = jnp.sum(onehot * (csum - 1), axis=1)
    counts = csum[-1]
    pcounts = (counts + tm - 1) // tm * tm
    pends = jnp.cumsum(pcounts)
    pstarts = pends - pcounts
    dest = (pstarts[flat_e] + pos).astype(jnp.int32)
    n_tiles = a // tm + N_EXPERTS
    tok = jnp.arange(a, dtype=jnp.int32) // MOE_TOP_K
    slot_tok = jnp.zeros((n_tiles * tm,), jnp.int32).at[dest].set(tok)
    n_used = (pends[-1] // tm).astype(jnp.int32)
    tile_e = jnp.minimum(jnp.searchsorted(pends, jnp.arange(n_tiles, dtype=jnp.int32) * tm, side='right'),
                         N_EXPERTS - 1).astype(jnp.int32)
    tile_e = jnp.where(jnp.arange(n_tiles) < n_used, tile_e, tile_e[jnp.maximum(n_used - 1, 0)])
    return dest.reshape(n, MOE_TOP_K), slot_tok.reshape(n_tiles, tm), tile_e, n_used.reshape(1)


def _tile(n, pref):
    t = min(n, pref)
    assert n % t == 0, (n, pref)
    return t


def kernel(x, c, positions, rel_bias, ada_w, ada_b, mix_norm_g, ffn_norm_g, ab_w_in, ab_w_out, nsa_cmp_pos_k, nsa_cmp_w1_k, nsa_cmp_w2_k, nsa_cmp_pos_v, nsa_cmp_w1_v, nsa_cmp_w2_v, mla_q_norm_g, mla_kv_norm_g, mla_w_uq, mla_w_ukv, ffn_w_gate, ffn_w_up, ffn_w_down, c_w_in, c_w_out, moe_w_router, moe_w_gate, moe_w_up, moe_w_down, final_norm_g):
    b, s, d = x.shape
    assert ada_w.shape[0] == 2 and d == D_MODEL and s % 256 == 0
    m = b * s
    x2d = x.reshape(m, d)
    tm_big = _tile(s, 1024)
    tm_mid = _tile(s, 512)
    t_att = _tile(s, 256)

    mod = _ada_mod(c, ada_w, ada_b)
    mods = mod.reshape(2, b, 6, 1, d)

    def layer_mod(i):
        sh_m, sc_m, g_m, sh_f, sc_f, g_f = (mods[i, :, j] for j in range(6))
        return (mix_norm_g[i][None, None, :] * (1.0 + sc_m), sh_m, g_m,
                ffn_norm_g[i][None, None, :] * (1.0 + sc_f), sh_f, g_f)

    geff_m, sh_m, g_m, geff_f, sh_f, g_f = layer_mod(0)
    w0 = ab_w_in[0]
    c_q, c_kv, c_g, c_ql, c_kvl = np.cumsum([NSA_HEADS * HEAD_DIM, 6 * NSA_KV_HEADS * HEAD_DIM,
                                             3 * NSA_HEADS, Q_LORA, KV_LORA]).tolist()
    zpad = lambda n: jnp.zeros((d, n), w0.dtype)
    w_in0 = jnp.concatenate([w0[:, :c_kv], w0[:, c_ql:c_kvl], w0[:, c_kvl:], w0[:, c_kv:c_g],
                             zpad(LANES - QK_ROPE - 3 * NSA_HEADS), w0[:, c_g:c_ql], zpad(LANES)],
                            axis=1).astype(BF16)
    assert w_in0.shape[1] == Z0_W
    z0 = _nmm(x2d, geff_m, sh_m, w_in0, seq=s, tm=tm_big, tn=Z0_W // 3, norm_cols=d, x_block=0,
              out_dtype=BF16, name="proj_in_ab")
    z0_3d = z0.reshape(b, s, Z0_W)

    inv_freq = ROPE_THETA ** (-jnp.arange(0, QK_ROPE, 2, dtype=F32) / QK_ROPE)
    ang = positions.astype(F32)[..., None] * inv_freq
    cos, sin = jnp.cos(ang).reshape(m, -1), jnp.sin(ang).reshape(m, -1)
    hr = QK_ROPE // 2
    zr = lambda n: jnp.zeros((m, n), F32)
    rope_tabs = (jnp.concatenate([cos, cos, zr(LANES - 2 * hr)], axis=1),
                 jnp.concatenate([-sin, zr(LANES - hr)], axis=1),
                 jnp.concatenate([zr(hr), sin, zr(LANES - 2 * hr)], axis=1))
    hw = MXU_DIM
    wq3 = mla_w_uq[0].reshape(Q_LORA, MLA_HEADS, QK_NOPE + QK_ROPE)
    wq = jnp.concatenate([wq3, jnp.zeros((Q_LORA, MLA_HEADS, hw - QK_NOPE - QK_ROPE), F32)], axis=2)
    wq = jnp.concatenate([wq.reshape(Q_LORA, MLA_HEADS * hw), jnp.zeros((LANES, MLA_HEADS * hw), F32)],
                         axis=0).astype(BF16)
    wkv3 = mla_w_ukv[0].reshape(KV_LORA, MLA_HEADS, QK_NOPE + V_DIM)
    wk_top = jnp.concatenate([wkv3[:, :, :QK_NOPE], jnp.zeros((KV_LORA, MLA_HEADS, hw - QK_NOPE), F32)],
                             axis=2).reshape(KV_LORA, MLA_HEADS * hw)
    pe_pass = jnp.zeros((LANES, hw), F32).at[jnp.arange(QK_ROPE), QK_NOPE + jnp.arange(QK_ROPE)].set(1.0)
    wk_bot = jnp.tile(pe_pass, (1, MLA_HEADS))
    wv = jnp.concatenate([wkv3[:, :, QK_NOPE:].reshape(KV_LORA, MLA_HEADS * V_DIM),
                          jnp.zeros((LANES, MLA_HEADS * V_DIM), F32)], axis=0)
    wkv = jnp.concatenate([jnp.concatenate([wk_top, wk_bot], axis=0), wv], axis=1).astype(BF16)
    ones_b = lambda g: jnp.broadcast_to(g[None, None, :], (b, 1, g.shape[0]))
    zeros_lat = jnp.zeros((b, 1, Q_LORA), F32)
    q_mla = _nmm(z0, ones_b(mla_q_norm_g[0]), zeros_lat, wq, seq=s, tm=tm_big, tn=1024, norm_cols=Q_LORA,
                 x_block=Z0_QLAT // LAT_BLOCK, out_dtype=BF16, rope=rope_tabs, rope_tiles=2,
                 name="mla_q_up")
    kv_mla = _nmm(z0, ones_b(mla_kv_norm_g[0]), zeros_lat, wkv, seq=s, tm=tm_big, tn=1024,
                  norm_cols=KV_LORA, x_block=Z0_KVLAT // LAT_BLOCK, out_dtype=BF16, rope=rope_tabs,
                  rope_tiles=2, name="mla_kv_up")
    o_mla = _flash(q_mla.reshape(b, s, -1), kv_mla.reshape(b, s, -1), kv_mla.reshape(b, s, -1),
                   t=t_att, n_kvh=MLA_HEADS, r_n=1, dk=hw, q_blk=0, k_blk=0,
                   v_blk=(MLA_HEADS * hw) // (MLA_HEADS * V_DIM), scale=(QK_NOPE + QK_ROPE) ** -0.5,
                   out_dtype=BF16, name="mla_attn")

    n16 = s // CMP_STRIDE
    cmp_cols = z0_3d[:, :, Z0_KV:Z0_KV + 2 * NSA_KV_HEADS * HEAD_DIM]
    x16 = cmp_cols.reshape(b, n16, CMP_STRIDE, 2 * NSA_KV_HEADS, HEAD_DIM).transpose(0, 3, 1, 2, 4)
    x16 = x16.reshape(b, 2 * NSA_KV_HEADS, n16, CMP_STRIDE * HEAD_DIM)
    pos_kv = jnp.stack([nsa_cmp_pos_k[0], nsa_cmp_pos_v[0]]).reshape(2, 1, CMP_LEN * HEAD_DIM).astype(BF16)
    w1_kv = jnp.stack([nsa_cmp_w1_k[0], nsa_cmp_w1_v[0]]).astype(BF16)
    w2_kv = jnp.stack([nsa_cmp_w2_k[0], nsa_cmp_w2_v[0]]).astype(BF16)
    kvc = _nsa_compress(x16, pos_kv, w1_kv, w2_kv)

    n_cmp = (s - CMP_LEN) // CMP_STRIDE + 1
    n_slc = s // SLC_LEN
    ratio, span = SLC_LEN // CMP_STRIDE, CMP_LEN // CMP_STRIDE
    cm = np.zeros((n16, n_slc), np.float32)
    for j in range(n_slc):
        for mm in range(ratio):
            for nn in range(span):
                i = ratio * j + mm - nn
                if 0 <= i < n_cmp:
                    cm[i, j] += 1.0
    tpos = jnp.arange(s)
    rel_c = tpos[:, None] - (jnp.arange(n16) * CMP_STRIDE + CMP_LEN - 1)[None, :]
    bias_c = jnp.where((rel_c >= 0)[None], rel_bias[_t5_bucket(rel_c)].astype(F32).transpose(2, 0, 1), NEG_INF)
    o_nsa, sel = _nsa_cmp_attention(z0_3d, kvc, bias_c, jnp.asarray(cm), tq=t_att)

    nq = s // t_att
    nd = min(nq, -(-(T5_MAX_DIST + t_att - 1) // t_att) + 1)
    ii = jnp.arange(t_att)
    rel_d = (jnp.arange(nd)[:, None, None] * t_att + ii[None, :, None] - ii[None, None, :])
    bias_d = rel_bias[_t5_bucket(rel_d)].astype(F32).transpose(3, 0, 1, 2)
    kvw = NSA_KV_HEADS * HEAD_DIM
    o_nsa = _flash(z0_3d, z0_3d, z0_3d, t=t_att, n_kvh=NSA_KV_HEADS, r_n=NSA_GROUP, dk=HEAD_DIM,
                   q_blk=0, k_blk=(Z0_KV + 2 * kvw) // kvw, v_blk=(Z0_KV + 3 * kvw) // kvw,
                   scale=HEAD_DIM ** -0.5, out_dtype=F32, bias=bias_d, sel=sel, misc=z0_3d,
                   misc_blk=Z0_MISC // LANES, prev=o_nsa, gate_branch=1, name="nsa_slc_attn")

    npv_w = -(-(WIN - 1) // t_att)
    bias_w = _band_bias(rel_bias, t_att, npv_w * t_att, WIN - 1, 1)
    (o_nsa,) = _band(z0_3d, z0_3d, bias_w, lead_grid=(b,), tq=t_att, pb=t_att, npv=npv_w,
                     n_kvh=NSA_KV_HEADS, r_n=NSA_GROUP,
                     qmap=lambda i, j: (i, 0), kmap=lambda i, j: (i, (Z0_KV + 4 * kvw) // kvw),
                     vmap=lambda i, j: (i, (Z0_KV + 5 * kvw) // kvw), omap=lambda i, j: (i, 0),
                     out_shape=jax.ShapeDtypeStruct((b, s, NSA_HEADS * HEAD_DIM), F32),
                     misc=z0_3d, miscmap=lambda i, j: (i, Z0_MISC // LANES), prev=o_nsa, gate_branch=2,
                     scale=HEAD_DIM ** -0.5, name="nsa_win_attn")

    x2d = _out_ab(o_nsa.reshape(m, -1), o_mla.reshape(m, -1), ab_w_out[0].astype(BF16), x2d, g_m,
                  seq=s, tm=tm_mid)

    x2d = _ffn(x2d, geff_f, sh_f, ffn_w_gate[0].astype(BF16), ffn_w_up[0].astype(BF16),
               ffn_w_down[0].astype(BF16), g_f, seq=s, tm=tm_mid, tf=512)

    geff_m, sh_m, g_m, geff_f, sh_f, g_f = layer_mod(1)
    cw = c_w_in.shape[2]
    z1 = _nmm(x2d, geff_m, sh_m, c_w_in[0].astype(BF16), seq=s, tm=tm_big, tn=1024, norm_cols=d,
              x_block=0, out_dtype=BF16, name="proj_in_c")
    hw_c = DIL_HEADS * HEAD_DIM
    os_, lses = [], []
    for gidx, (win, dil) in enumerate(DIL_PATTERNS):
        ls = s // dil
        tq = _tile(ls, 256)
        pb = min(tq, 128)
        max_back = win // dil
        npv = -(-max_back // pb)
        bias_g = _band_bias(rel_bias, tq, npv * pb, max_back, dil)
        zv = z1.reshape(b, ls, dil * cw)
        cpr = cw // hw_c
        og, lg = _band(zv, zv, bias_g, lead_grid=(b, dil), tq=tq, pb=pb, npv=npv, n_kvh=DIL_HEADS, r_n=1,
                       qmap=lambda i, r, j, gi=gidx: (i, r * cpr + gi * 3),
                       kmap=lambda i, r, j, gi=gidx: (i, r * cpr + gi * 3 + 1),
                       vmap=lambda i, r, j, gi=gidx: (i, r * cpr + gi * 3 + 2),
                       omap=lambda i, r, j: (i, r), lmap=lambda i, r, j: (i, r),
                       out_shape=jax.ShapeDtypeStruct((b, ls, dil * hw_c), F32),
                       lse_shape=jax.ShapeDtypeStruct((b, ls, dil * LANES), F32),
                       scale=HEAD_DIM ** -0.5, name=f"dil_attn_{gidx}")
        os_.append(og.reshape(m, hw_c))
        lses.append(lg.reshape(m, LANES))
    x2d = _out_c(os_, lses, c_w_out[0].astype(BF16), x2d, g_m, seq=s, tm=tm_mid)

    wr = jnp.concatenate([moe_w_router[0], jnp.zeros((d, LANES - N_EXPERTS), F32)], axis=1)
    h_moe, route = _moe_prep(x2d, geff_f, sh_f, wr, seq=s, tm=tm_mid)
    tm_e = 512
    dest, slot_tok, tile_e, n_used = _moe_plan(route, tm=tm_e)
    xg = _moe_gather(h_moe, slot_tok, tg=tm_e)
    yb = _moe_experts(xg, tile_e, n_used, moe_w_gate[0].astype(BF16), moe_w_up[0].astype(BF16),
                      moe_w_down[0].astype(BF16), tm=tm_e, tf=512)
    tmc = _tile(s, 256)
    out = _moe_combine(yb, dest[:, 0].reshape(m // tmc, tmc), dest[:, 1].reshape(m // tmc, tmc), route,
                       x2d, g_f, final_norm_g.reshape(1, d), seq=s, tmc=tmc)
    return out.reshape(b, s, d)
```

```python
import functools
import math

import numpy as np
import jax
import jax.numpy as jnp
from jax import lax
from jax.experimental import pallas as pl
from jax.experimental.pallas import tpu as pltpu

F32 = jnp.float32
BF16 = jnp.bfloat16

D_MODEL = 2048
HEAD_DIM = 128
NEG_INF = -1e30
RMS_EPS = 1e-6
NUM_BUCKETS = 32
T5_MAX_DIST = 2048
NSA_HEADS = 8
NSA_KV_HEADS = 2
NSA_GROUP = NSA_HEADS // NSA_KV_HEADS
CMP_LEN = 32
CMP_STRIDE = 16
SLC_LEN = 64
N_SEL = 16
WIN = 512
FORCE_SCORE = 1e9
MLA_HEADS = 8
Q_LORA = 512
KV_LORA = 512
QK_NOPE = 128
QK_ROPE = 64
V_DIM = 128
ROPE_THETA = 10000.0
DIL_PATTERNS = ((128, 1), (512, 4), (2048, 16))
DIL_HEADS = 8
D_FF = 5632
N_EXPERTS = 8
MOE_TOP_K = 2
D_FF_EXPERT = 7168

LANES = 128
MXU_DIM = 256
VMEM_LIMIT_BYTES = 56 * 1024 * 1024
ROW_DMA_UNROLL = 8

Z0_Q = 0
Z0_KV = Z0_Q + NSA_HEADS * HEAD_DIM
Z0_KVLAT = Z0_KV + 6 * NSA_KV_HEADS * HEAD_DIM
Z0_MISC = Z0_KVLAT + KV_LORA
Z0_QLAT = Z0_MISC + LANES
Z0_W = Z0_QLAT + Q_LORA + LANES
LAT_BLOCK = KV_LORA + LANES
GATE_LANE0 = QK_ROPE


def _cparams(sem, vmem=VMEM_LIMIT_BYTES):
    return pltpu.CompilerParams(dimension_semantics=sem, vmem_limit_bytes=vmem)


def _dot(a, b):
    return jnp.dot(a, b, preferred_element_type=F32)


def _dot_nt(a, b):
    return lax.dot_general(a, b, (((1,), (1,)), ((), ())), preferred_element_type=F32)


def _sigmoid(x):
    return 1.0 / (1.0 + jnp.exp(-x))


def _t5_bucket(dist):
    n = jnp.maximum(dist, 0)
    max_exact = NUM_BUCKETS // 2
    nf = jnp.maximum(n, 1).astype(F32)
    large = max_exact + (jnp.log(nf / max_exact) / math.log(T5_MAX_DIST / max_exact)
                         * (NUM_BUCKETS - max_exact)).astype(jnp.int32)
    large = jnp.minimum(large, NUM_BUCKETS - 1)
    return jnp.where(n < max_exact, n, large)


def _ada_kernel(c_ref, w_ref, b_ref, o_ref):
    c = c_ref[...]
    cs = c * _sigmoid(c)
    o_ref[...] = _dot(cs.astype(BF16), w_ref[...].astype(BF16)) + b_ref[...]


def _ada_mod(c, ada_w, ada_b):
    depth, d, n = ada_w.shape
    b = c.shape[0]
    bp = 8
    cpad = jnp.zeros((bp, d), F32).at[:b].set(c)
    tn = 1024
    out = pl.pallas_call(
        _ada_kernel,
        out_shape=jax.ShapeDtypeStruct((depth, bp, n), F32),
        grid=(depth, n // tn),
        in_specs=[pl.BlockSpec((bp, d), lambda l, j: (0, 0)),
                  pl.BlockSpec((None, d, tn), lambda l, j: (l, 0, j)),
                  pl.BlockSpec((None, 1, tn), lambda l, j: (l, 0, j))],
        out_specs=pl.BlockSpec((None, bp, tn), lambda l, j: (l, 0, j)),
        compiler_params=_cparams(("parallel", "parallel")),
        name="ada_mod",
    )(cpad, ada_w, ada_b.reshape(depth, 1, n))
    return out[:, :b]


def _nmm_kernel(x_ref, g_ref, s_ref, w_ref, *rest, norm_cols, rope_tiles, tn):
    if rope_tiles:
        c_ref, s1_ref, s2_ref, o_ref, h_scr = rest
    else:
        o_ref, h_scr = rest
    j = pl.program_id(1)

    @pl.when(j == 0)
    def _():
        x = x_ref[...].astype(F32)
        xn = x[:, :norm_cols]
        ms = jnp.mean(xn * xn, axis=-1, keepdims=True)
        hn = xn * lax.rsqrt(ms + RMS_EPS) * g_ref[...] + s_ref[...]
        h_scr[:, :norm_cols] = hn.astype(BF16)
        if norm_cols < x.shape[1]:
            h_scr[:, norm_cols:] = x[:, norm_cols:].astype(BF16)

    acc = _dot(h_scr[...], w_ref[...])
    o_ref[...] = acc.astype(o_ref.dtype)
    if rope_tiles:
        @pl.when(j < rope_tiles)
        def _():
            for hh in range(tn // MXU_DIM):
                lo = hh * MXU_DIM + LANES
                y = acc[:, lo:lo + LANES]
                y2 = (y * c_ref[...] + pltpu.roll(y, LANES - 32, 1) * s1_ref[...]
                      + pltpu.roll(y, 32, 1) * s2_ref[...])
                o_ref[:, lo:lo + LANES] = y2.astype(o_ref.dtype)


def _nmm(x2d, geff, shift, w, *, seq, tm, tn, norm_cols, x_block, out_dtype, rope=None,
         rope_tiles=0, name):
    m = x2d.shape[0]
    k, n = w.shape
    tpb = seq // tm
    in_specs = [pl.BlockSpec((tm, k), lambda i, j: (i, x_block)),
                pl.BlockSpec((None, 1, norm_cols), lambda i, j: (i // tpb, 0, 0)),
                pl.BlockSpec((None, 1, norm_cols), lambda i, j: (i // tpb, 0, 0)),
                pl.BlockSpec((k, tn), lambda i, j: (0, j))]
    args = [x2d, geff, shift, w]
    if rope_tiles:
        for t in rope:
            in_specs.append(pl.BlockSpec((tm, LANES), lambda i, j: (i, 0)))
            args.append(t)
    return pl.pallas_call(
        functools.partial(_nmm_kernel, norm_cols=norm_cols, rope_tiles=rope_tiles, tn=tn),
        out_shape=jax.ShapeDtypeStruct((m, n), out_dtype),
        grid=(m // tm, n // tn),
        in_specs=in_specs,
        out_specs=pl.BlockSpec((tm, tn), lambda i, j: (i, j)),
        scratch_shapes=[pltpu.VMEM((tm, k), BF16)],
        compiler_params=_cparams(("parallel", "arbitrary")),
        name=name,
    )(*args)


def _cmp_kernel(x_ref, pos_ref, w1_ref, w2_ref, o_ref):
    half = w1_ref.shape[0] // 2
    x = x_ref[...]
    n16 = x.shape[0]
    a = _dot(x, w1_ref[:half, :])
    b = _dot(x, w1_ref[half:, :])
    c = _dot(pos_ref[...], w1_ref[...])
    hid = a + pltpu.roll(b, n16 - 1, 0) + c
    hid = jax.nn.gelu(hid, approximate=True)
    o_ref[...] = _dot(hid.astype(BF16), w2_ref[...]).astype(o_ref.dtype)


def _nsa_compress(x16, pos, w1, w2):
    b, nkv, n16, kk = x16.shape
    g = NSA_KV_HEADS
    return pl.pallas_call(
        _cmp_kernel,
        out_shape=jax.ShapeDtypeStruct((b, nkv, n16, HEAD_DIM), BF16),
        grid=(b, nkv),
        in_specs=[pl.BlockSpec((None, None, n16, kk), lambda i, j: (i, j, 0, 0)),
                  pl.BlockSpec((None, 1, 2 * kk), lambda i, j: (j // g, 0, 0)),
                  pl.BlockSpec((None, 2 * kk, HEAD_DIM), lambda i, j: (j // g, 0, 0)),
                  pl.BlockSpec((None, HEAD_DIM, HEAD_DIM), lambda i, j: (j // g, 0, 0))],
        out_specs=pl.BlockSpec((None, None, n16, HEAD_DIM), lambda i, j: (i, j, 0, 0)),
        compiler_params=_cparams(("parallel", "parallel")),
        name="nsa_compress",
    )(x16, pos, w1, w2)


def _cmpattn_kernel(q_ref, kv_ref, bias_ref, cmat_ref, misc_ref, o_ref, sel_ref, *, tq, n_slc,
                    n_top, scale):
    qi = pl.program_id(1)
    g_n, r_n = NSA_KV_HEADS, NSA_GROUP
    ncp = kv_ref.shape[1]
    trow = qi * tq + lax.broadcasted_iota(jnp.int32, (tq, 1), 0)
    has_c = trow >= (CMP_LEN - 1)
    blk_t = lax.shift_right_logical(trow, int(math.log2(SLC_LEN))).astype(F32)
    jb = lax.broadcasted_iota(jnp.int32, (tq, n_slc), 1).astype(F32)
    forced = (jb == 0.0) | (jb == blk_t) | (jb == blk_t - 1.0)
    valid = jb <= blk_t
    misc = misc_ref[...].astype(F32)
    for g in range(g_n):
        kc = kv_ref[g]
        vc = kv_ref[g_n + g]
        imp = jnp.zeros((tq, ncp), F32)
        for r in range(r_n):
            h = g * r_n + r
            q = q_ref[:, h * HEAD_DIM:(h + 1) * HEAD_DIM]
            s = _dot_nt(q, kc) * scale + bias_ref[h]
            m = jnp.max(s, axis=-1, keepdims=True)
            e = jnp.exp(s - m)
            p = e * (1.0 / jnp.sum(e, axis=-1, keepdims=True))
            p = jnp.where(has_c, p, 0.0)
            imp = imp + p
            o = _dot(p.astype(BF16), vc)
            c0 = GATE_LANE0 + h * 3
            gate = _sigmoid(misc[:, c0:c0 + 1])
            o_ref[:, h * HEAD_DIM:(h + 1) * HEAD_DIM] = gate * o
        imp_s = jnp.dot(imp, cmat_ref[...], precision=lax.Precision.HIGHEST,
                        preferred_element_type=F32)
        score = jnp.where(forced, FORCE_SCORE, jnp.where(valid, imp_s, -1.0))
        sel = jnp.zeros((tq, n_slc), F32)
        for _ in range(n_top):
            mx = jnp.max(score, axis=-1, keepdims=True)
            first = jnp.min(jnp.where(score == mx, jb, float(n_slc)), axis=-1, keepdims=True)
            hit = jb == first
            sel = jnp.where(hit, jnp.where(mx > -0.5, 1.0, 0.0), sel)
            score = jnp.where(hit, -3e38, score)
        sel_ref[g] = sel.astype(sel_ref.dtype)


def _nsa_cmp_attention(z0, kvc, bias_c, cmat, *, tq):
    b, s, _ = z0.shape
    ncp = kvc.shape[2]
    n_slc = s // SLC_LEN
    n_top = min(N_SEL, n_slc)
    qw = NSA_HEADS * HEAD_DIM
    return pl.pallas_call(
        functools.partial(_cmpattn_kernel, tq=tq, n_slc=n_slc, n_top=n_top, scale=HEAD_DIM ** -0.5),
        out_shape=(jax.ShapeDtypeStruct((b, s, qw), F32),
                   jax.ShapeDtypeStruct((b, NSA_KV_HEADS, s, n_slc), BF16)),
        grid=(b, s // tq),
        in_specs=[pl.BlockSpec((None, tq, qw), lambda i, j: (i, j, Z0_Q // qw)),
                  pl.BlockSpec((None, 2 * NSA_KV_HEADS, ncp, HEAD_DIM), lambda i, j: (i, 0, 0, 0)),
                  pl.BlockSpec((NSA_HEADS, tq, ncp), lambda i, j: (0, j, 0)),
                  pl.BlockSpec((ncp, n_slc), lambda i, j: (0, 0)),
                  pl.BlockSpec((None, tq, LANES), lambda i, j: (i, j, Z0_MISC // LANES))],
        out_specs=(pl.BlockSpec((None, tq, qw), lambda i, j: (i, j, 0)),
                   pl.BlockSpec((None, NSA_KV_HEADS, tq, n_slc), lambda i, j: (i, 0, j, 0))),
        compiler_params=_cparams(("parallel", "parallel")),
        name="nsa_cmp_attn",
    )(z0, kvc, bias_c, cmat, z0)


def _flash_kernel(qi_ref, ki_ref, q_ref, k_ref, v_ref, *rest, t, n_kvh, r_n, dk, scale, has_bias,
                  has_sel, gate_branch):
    rest = list(rest)
    bias_ref = rest.pop(0) if has_bias else None
    sel_ref = rest.pop(0) if has_sel else None
    if gate_branch is not None:
        misc_ref = rest.pop(0)
        prev_ref = rest.pop(0)
    o_ref, m_scr, l_scr, acc_scr = rest
    pidx = pl.program_id(1)
    qi = qi_ref[pidx]
    ki = ki_ref[pidx]

    @pl.when(ki == 0)
    def _():
        m_scr[...] = jnp.full(m_scr.shape, NEG_INF, F32)
        l_scr[...] = jnp.zeros(l_scr.shape, F32)
        acc_scr[...] = jnp.zeros(acc_scr.shape, F32)

    def step(diag):
        if has_sel:
            per = t // SLC_LEN
            erow = lax.broadcasted_iota(jnp.int32, (sel_ref.shape[2], t), 0)
            ecol = lax.broadcasted_iota(jnp.int32, (sel_ref.shape[2], t), 1)
            expand = jnp.where(erow == ki * per + lax.shift_right_logical(ecol, int(math.log2(SLC_LEN))),
                               1.0, 0.0).astype(BF16)
        if diag:
            causal = (lax.broadcasted_iota(jnp.int32, (t, t), 0)
                      >= lax.broadcasted_iota(jnp.int32, (t, t), 1))
        for kh in range(n_kvh):
            k = k_ref[:, kh * dk:(kh + 1) * dk]
            v = v_ref[:, kh * HEAD_DIM:(kh + 1) * HEAD_DIM]
            if has_sel:
                selm = _dot(sel_ref[kh], expand) > 0.5
            for r in range(r_n):
                h = kh * r_n + r
                q = q_ref[:, h * dk:(h + 1) * dk]
                s = _dot_nt(q, k) * scale
                if has_bias:
                    s = s + bias_ref[h]
                if has_sel:
                    s = jnp.where(selm, s, NEG_INF)
                if diag:
                    s = jnp.where(causal, s, NEG_INF)
                m_prev = m_scr[h]
                m_new = jnp.maximum(m_prev, jnp.max(s, axis=-1, keepdims=True))
                alpha = jnp.exp(m_prev - m_new)
                p = jnp.exp(s - m_new)
                l_scr[h] = alpha * l_scr[h] + jnp.sum(p, axis=-1, keepdims=True)
                acc_scr[h] = alpha * acc_scr[h] + _dot(p.astype(BF16), v)
                m_scr[h] = m_new

    @pl.when(ki < qi)
    def _():
        step(False)

    @pl.when(ki == qi)
    def _():
        step(True)
        if gate_branch is not None:
            misc = misc_ref[...].astype(F32)
        for h in range(n_kvh * r_n):
            o = acc_scr[h] * (1.0 / l_scr[h])
            sl = slice(h * HEAD_DIM, (h + 1) * HEAD_DIM)
            if gate_branch is not None:
                c0 = GATE_LANE0 + h * 3 + gate_branch
                o = prev_ref[:, sl] + _sigmoid(misc[:, c0:c0 + 1]) * o
            o_ref[:, sl] = o.astype(o_ref.dtype)


def _tri_pairs(nq):
    qi = np.concatenate([np.full(i + 1, i, np.int32) for i in range(nq)])
    ki = np.concatenate([np.arange(i + 1, dtype=np.int32) for i in range(nq)])
    return jnp.asarray(qi), jnp.asarray(ki)


def _flash(q_arr, k_arr, v_arr, *, t, n_kvh, r_n, dk, q_blk, k_blk, v_blk, scale, out_dtype,
           bias=None, sel=None, misc=None, misc_blk=0, prev=None, gate_branch=None, name):
    b, s, _ = q_arr.shape
    nq = s // t
    qi_a, ki_a = _tri_pairs(nq)
    nh = n_kvh * r_n
    ow = nh * HEAD_DIM
    in_specs = [pl.BlockSpec((None, t, nh * dk), lambda i, p, qa, ka: (i, qa[p], q_blk)),
                pl.BlockSpec((None, t, n_kvh * dk), lambda i, p, qa, ka: (i, ka[p], k_blk)),
                pl.BlockSpec((None, t, n_kvh * HEAD_DIM), lambda i, p, qa, ka: (i, ka[p], v_blk))]
    args = [q_arr, k_arr, v_arr]
    if bias is not None:
        nd = bias.shape[1]
        in_specs.append(pl.BlockSpec((nh, None, t, t),
                                     lambda i, p, qa, ka: (0, jnp.minimum(qa[p] - ka[p], nd - 1), 0, 0)))
        args.append(bias)
    if sel is not None:
        n_slc = sel.shape[-1]
        in_specs.append(pl.BlockSpec((None, n_kvh, t, n_slc), lambda i, p, qa, ka: (i, 0, qa[p], 0)))
        args.append(sel)
    io_alias = {}
    if gate_branch is not None:
        in_specs.append(pl.BlockSpec((None, t, LANES), lambda i, p, qa, ka: (i, qa[p], misc_blk)))
        args.append(misc)
        in_specs.append(pl.BlockSpec((None, t, ow), lambda i, p, qa, ka: (i, qa[p], 0)))
        args.append(prev)
        io_alias = {2 + len(args) - 1: 0}
    return pl.pallas_call(
        functools.partial(_flash_kernel, t=t, n_kvh=n_kvh, r_n=r_n, dk=dk, scale=scale,
                          has_bias=bias is not None, has_sel=sel is not None, gate_branch=gate_branch),
        out_shape=jax.ShapeDtypeStruct((b, s, ow), out_dtype),
        grid_spec=pltpu.PrefetchScalarGridSpec(
            num_scalar_prefetch=2,
            grid=(b, int(qi_a.shape[0])),
            in_specs=in_specs,
            out_specs=pl.BlockSpec((None, t, ow), lambda i, p, qa, ka: (i, qa[p], 0)),
            scratch_shapes=[pltpu.VMEM((nh, t, 1), F32), pltpu.VMEM((nh, t, 1), F32),
                            pltpu.VMEM((nh, t, HEAD_DIM), F32)]),
        input_output_aliases=io_alias,
        compiler_params=_cparams(("parallel", "arbitrary")),
        name=name,
    )(qi_a, ki_a, *args)


def _band_kernel(q_ref, *rest, tq, pb, npv, n_kvh, r_n, scale, gate_branch, want_lse, qi_axis):
    rest = list(rest)
    kp = [rest.pop(0) for _ in range(npv)]
    kc = rest.pop(0)
    vp = [rest.pop(0) for _ in range(npv)]
    vc = rest.pop(0)
    bias_ref = rest.pop(0)
    if gate_branch is not None:
        misc_ref = rest.pop(0)
        prev_ref = rest.pop(0)
    o_ref = rest.pop(0)
    lse_ref = rest.pop(0) if want_lse else None
    qi = pl.program_id(qi_axis)
    nblk = tq // pb
    kw_prev = npv * pb
    if gate_branch is not None:
        misc = misc_ref[...].astype(F32)
    if want_lse:
        lane = lax.broadcasted_iota(jnp.int32, (tq, LANES), 1)
        lse_tile = jnp.zeros((tq, LANES), F32)
    for kh in range(n_kvh):
        ksl = slice(kh * HEAD_DIM, (kh + 1) * HEAD_DIM)
        for r in range(r_n):
            h = kh * r_n + r
            hsl = slice(h * HEAD_DIM, (h + 1) * HEAD_DIM)
            q = q_ref[:, hsl]
            parts = []
            for n in range(npv):
                pen = jnp.where(qi * nblk - npv + n >= 0, 0.0, NEG_INF)
                parts.append(_dot_nt(q, kp[n][:, ksl]) * scale + bias_ref[h, :, n * pb:(n + 1) * pb] + pen)
            parts.append(_dot_nt(q, kc[:, ksl]) * scale + bias_ref[h, :, kw_prev:])
            m = parts[0].max(axis=-1, keepdims=True)
            for sp in parts[1:]:
                m = jnp.maximum(m, sp.max(axis=-1, keepdims=True))
            l = jnp.zeros((tq, 1), F32)
            o = jnp.zeros((tq, HEAD_DIM), F32)
            for n, sp in enumerate(parts):
                p = jnp.exp(sp - m)
                l = l + jnp.sum(p, axis=-1, keepdims=True)
                vv = vp[n][:, ksl] if n < npv else vc[:, ksl]
                o = o + _dot(p.astype(BF16), vv)
            o = o * (1.0 / l)
            if gate_branch is not None:
                c0 = GATE_LANE0 + h * 3 + gate_branch
                o = prev_ref[:, hsl] + _sigmoid(misc[:, c0:c0 + 1]) * o
            o_ref[:, hsl] = o.astype(o_ref.dtype)
            if want_lse:
                lse_tile = jnp.where(lane == h, m + jnp.log(l), lse_tile)
    if want_lse:
        lse_ref[...] = lse_tile


def _band(q_arr, kv_arr, bias, *, lead_grid, tq, pb, npv, n_kvh, r_n, qmap, kmap, vmap, omap,
          lmap=None, out_shape, lse_shape=None, misc=None, miscmap=None, prev=None, gate_branch=None,
          scale, name):
    nl = len(lead_grid)
    nq = q_arr.shape[1] // tq
    nblk = tq // pb
    nh = n_kvh * r_n
    qw = nh * HEAD_DIM
    kw = n_kvh * HEAD_DIM

    def rows_cur(fn):
        def im(*g):
            bb, cc = fn(*g)
            return (bb, g[nl], cc)
        return im

    def rows_prev(fn, n):
        def im(*g):
            bb, cc = fn(*g)
            return (bb, jnp.maximum(g[nl] * nblk - npv + n, 0), cc)
        return im

    in_specs = [pl.BlockSpec((None, tq, qw), rows_cur(qmap))]
    args = [q_arr]
    for fn in (kmap, vmap):
        for n in range(npv):
            in_specs.append(pl.BlockSpec((None, pb, kw), rows_prev(fn, n)))
            args.append(kv_arr)
        in_specs.append(pl.BlockSpec((None, tq, kw), rows_cur(fn)))
        args.append(kv_arr)
    in_specs.append(pl.BlockSpec(bias.shape, lambda *g: (0, 0, 0)))
    args.append(bias)
    io_alias = {}
    if gate_branch is not None:
        in_specs.append(pl.BlockSpec((None, tq, LANES), rows_cur(miscmap)))
        args.append(misc)
        in_specs.append(pl.BlockSpec((None, tq, qw), rows_cur(omap)))
        args.append(prev)
        io_alias = {len(args) - 1: 0}
    out_shapes = [out_shape]
    out_specs = [pl.BlockSpec((None, tq, qw), rows_cur(omap))]
    if lse_shape is not None:
        out_shapes.append(lse_shape)
        out_specs.append(pl.BlockSpec((None, tq, LANES), rows_cur(lmap)))
    res = pl.pallas_call(
        functools.partial(_band_kernel, tq=tq, pb=pb, npv=npv, n_kvh=n_kvh, r_n=r_n, scale=scale,
                          gate_branch=gate_branch, want_lse=lse_shape is not None, qi_axis=nl),
        out_shape=tuple(out_shapes),
        grid=tuple(lead_grid) + (nq,),
        in_specs=in_specs,
        out_specs=tuple(out_specs),
        input_output_aliases=io_alias,
        compiler_params=_cparams(("parallel",) * (nl + 1)),
        name=name,
    )(*args)
    return res


def _dist_table(rel_bias, n_dist, dist_scale=1):
    tab = rel_bias[_t5_bucket(jnp.arange(n_dist) * dist_scale)].astype(F32)
    return jnp.concatenate([tab, jnp.full((1, tab.shape[1]), NEG_INF, F32)], axis=0)


def _toeplitz_tiles(tab, d0s, t, lo, hi, mult=1):
    masked = tab.shape[0] - 1
    u = np.arange(2 * t)
    i_minus_j = np.where(u < t, -u, 2 * t - u)
    dist = np.asarray(d0s)[:, None] + mult * i_minus_j[None, :]
    idx = np.where((dist >= lo) & (dist <= hi) & (u != t)[None, :], dist, masked)
    w = tab[jnp.asarray(idx, jnp.int32)].transpose(2, 0, 1)
    nh, nc = w.shape[0], w.shape[1]
    flat = jnp.tile(w, (1, 1, t))[:, :, :t * (2 * t - 1)]
    return flat.reshape(nh, nc, t, 2 * t - 1)[:, :, :, :t]


def _band_bias(rel_bias, tq, pb, npv, max_back, dist_scale):
    tab = _dist_table(rel_bias, max_back + 1, dist_scale)
    d0s = [(npv - n) * pb for n in range(npv)] + [0]
    tiles = _toeplitz_tiles(tab, d0s, tq, 0, max_back)
    parts = [tiles[:, n, :, :pb] for n in range(npv)] + [tiles[:, npv]]
    return jnp.concatenate(parts, axis=-1)


def _out_ab_kernel(oa_ref, ob_ref, wa_ref, wb_ref, x_ref, g_ref, o_ref):
    y = _dot(oa_ref[...].astype(BF16), wa_ref[...]) + _dot(ob_ref[...], wb_ref[...])
    o_ref[...] = x_ref[...] + g_ref[...] * y


def _out_ab(oa, ob, w, x2d, gate, *, seq, tm):
    m, d = x2d.shape
    ka = oa.shape[1]
    kb = ob.shape[1]
    tpb = seq // tm
    return pl.pallas_call(
        _out_ab_kernel,
        out_shape=jax.ShapeDtypeStruct((m, d), F32),
        grid=(m // tm,),
        in_specs=[pl.BlockSpec((tm, ka), lambda i: (i, 0)),
                  pl.BlockSpec((tm, kb), lambda i: (i, 0)),
                  pl.BlockSpec((ka, d), lambda i: (0, 0)),
                  pl.BlockSpec((kb, d), lambda i: (ka // kb, 0)),
                  pl.BlockSpec((tm, d), lambda i: (i, 0)),
                  pl.BlockSpec((None, 1, d), lambda i: (i // tpb, 0, 0))],
        out_specs=pl.BlockSpec((tm, d), lambda i: (i, 0)),
        compiler_params=_cparams(("parallel",)),
        name="out_proj_ab",
    )(oa, ob, w, w, x2d, gate)


def _out_c_kernel(o0_ref, o1_ref, o2_ref, l0_ref, l1_ref, l2_ref, w_ref, x_ref, g_ref, o_ref, mrg_scr):
    l0, l1, l2 = l0_ref[...], l1_ref[...], l2_ref[...]
    mx = jnp.maximum(jnp.maximum(l0, l1), l2)
    e0, e1, e2 = jnp.exp(l0 - mx), jnp.exp(l1 - mx), jnp.exp(l2 - mx)
    inv = 1.0 / (e0 + e1 + e2)
    w0, w1, w2 = e0 * inv, e1 * inv, e2 * inv
    for h in range(DIL_HEADS):
        sl = slice(h * HEAD_DIM, (h + 1) * HEAD_DIM)
        mg = (w0[:, h:h + 1] * o0_ref[:, sl] + w1[:, h:h + 1] * o1_ref[:, sl]
              + w2[:, h:h + 1] * o2_ref[:, sl])
        mrg_scr[:, sl] = mg.astype(BF16)
    o_ref[...] = x_ref[...] + g_ref[...] * _dot(mrg_scr[...], w_ref[...])


def _out_c(os_, lses, w, x2d, gate, *, seq, tm):
    m, d = x2d.shape
    kc = w.shape[0]
    tpb = seq // tm
    return pl.pallas_call(
        _out_c_kernel,
        out_shape=jax.ShapeDtypeStruct((m, d), F32),
        grid=(m // tm,),
        in_specs=[pl.BlockSpec((tm, kc), lambda i: (i, 0))] * 3
        + [pl.BlockSpec((tm, LANES), lambda i: (i, 0))] * 3
        + [pl.BlockSpec((kc, d), lambda i: (0, 0)),
           pl.BlockSpec((tm, d), lambda i: (i, 0)),
           pl.BlockSpec((None, 1, d), lambda i: (i // tpb, 0, 0))],
        out_specs=pl.BlockSpec((tm, d), lambda i: (i, 0)),
        scratch_shapes=[pltpu.VMEM((tm, kc), BF16)],
        compiler_params=_cparams(("parallel",)),
        name="out_proj_c",
    )(*os_, *lses, w, x2d, gate)


def _ffn_kernel(x_ref, g_ref, s_ref, wg_ref, wu_ref, wd_ref, gate_ref, o_ref, h_scr, acc_scr):
    f = pl.program_id(1)

    @pl.when(f == 0)
    def _():
        x = x_ref[...]
        ms = jnp.mean(x * x, axis=-1, keepdims=True)
        h_scr[...] = (x * lax.rsqrt(ms + RMS_EPS) * g_ref[...] + s_ref[...]).astype(BF16)
        acc_scr[...] = jnp.zeros(acc_scr.shape, F32)

    h = h_scr[...]
    a = _dot(h, wg_ref[...])
    b = _dot(h, wu_ref[...])
    hid = (a * _sigmoid(a) * b).astype(BF16)
    acc_scr[...] += _dot(hid, wd_ref[...])

    @pl.when(f == pl.num_programs(1) - 1)
    def _():
        o_ref[...] = x_ref[...] + gate_ref[...] * acc_scr[...]


def _ffn(x2d, geff, shift, wg, wu, wd, gate, *, seq, tm, tf):
    m, d = x2d.shape
    ff = wg.shape[1]
    tpb = seq // tm
    vec = pl.BlockSpec((None, 1, d), lambda i, f: (i // tpb, 0, 0))
    return pl.pallas_call(
        _ffn_kernel,
        out_shape=jax.ShapeDtypeStruct((m, d), F32),
        grid=(m // tm, ff // tf),
        in_specs=[pl.BlockSpec((tm, d), lambda i, f: (i, 0)), vec, vec,
                  pl.BlockSpec((d, tf), lambda i, f: (0, f)),
                  pl.BlockSpec((d, tf), lambda i, f: (0, f)),
                  pl.BlockSpec((tf, d), lambda i, f: (f, 0)),
                  vec],
        out_specs=pl.BlockSpec((tm, d), lambda i, f: (i, 0)),
        scratch_shapes=[pltpu.VMEM((tm, d), BF16), pltpu.VMEM((tm, d), F32)],
        compiler_params=_cparams(("parallel", "arbitrary")),
        name="ffn_swiglu",
    )(x2d, geff, shift, wg, wu, wd, gate)


def _moe_prep_kernel(x_ref, g_ref, s_ref, wr_ref, h_ref, route_ref):
    x = x_ref[...]
    ms = jnp.mean(x * x, axis=-1, keepdims=True)
    h = x * lax.rsqrt(ms + RMS_EPS) * g_ref[...] + s_ref[...]
    h_ref[...] = h
    logits = jnp.dot(h, wr_ref[...], precision=lax.Precision.HIGHEST, preferred_element_type=F32)
    lane = lax.broadcasted_iota(jnp.int32, logits.shape, 1).astype(F32)
    lg = jnp.where(lane < float(N_EXPERTS), logits, -3e38)
    m1 = jnp.max(lg, axis=-1, keepdims=True)
    i1 = jnp.min(jnp.where(lg == m1, lane, float(LANES)), axis=-1, keepdims=True)
    lg2 = jnp.where(lane == i1, -3e38, lg)
    m2 = jnp.max(lg2, axis=-1, keepdims=True)
    i2 = jnp.min(jnp.where(lg2 == m2, lane, float(LANES)), axis=-1, keepdims=True)
    e = jnp.exp(m2 - m1)
    inv = 1.0 / (1.0 + e)
    route = jnp.where(lane == 0.0, i1, jnp.where(lane == 1.0, i2, jnp.where(lane == 2.0, inv,
                      jnp.where(lane == 3.0, e * inv, 0.0))))
    route_ref[...] = route


def _moe_prep(x2d, geff, shift, w_router_pad, *, seq, tm):
    m, d = x2d.shape
    tpb = seq // tm
    vec = pl.BlockSpec((None, 1, d), lambda i: (i // tpb, 0, 0))
    return pl.pallas_call(
        _moe_prep_kernel,
        out_shape=(jax.ShapeDtypeStruct((m, d), F32), jax.ShapeDtypeStruct((m, LANES), F32)),
        grid=(m // tm,),
        in_specs=[pl.BlockSpec((tm, d), lambda i: (i, 0)), vec, vec,
                  pl.BlockSpec((d, LANES), lambda i: (0, 0))],
        out_specs=(pl.BlockSpec((tm, d), lambda i: (i, 0)), pl.BlockSpec((tm, LANES), lambda i: (i, 0))),
        compiler_params=_cparams(("parallel",)),
        name="moe_prep",
    )(x2d, geff, shift, w_router_pad)


def _row_copy(src_hbm, row, dst_vmem, slot, sem):
    return pltpu.make_async_copy(src_hbm.at[pl.ds(row, 1)], dst_vmem.at[pl.ds(slot, 1)], sem)


def _gather_kernel(idx_ref, h_hbm, o_ref, buf, sem, *, tg):
    t = pl.program_id(0)

    def start(r, c):
        _row_copy(h_hbm, idx_ref[t, r], buf, r, sem).start()
        return c

    def wait(r, c):
        _row_copy(h_hbm, 0, buf, r, sem).wait()
        return c

    lax.fori_loop(0, tg, start, 0, unroll=ROW_DMA_UNROLL)
    lax.fori_loop(0, tg, wait, 0, unroll=ROW_DMA_UNROLL)
    o_ref[...] = buf[...].astype(o_ref.dtype)


def _moe_gather(h2d, slot_tok, *, tg):
    n_tiles = slot_tok.shape[0]
    d = h2d.shape[1]
    return pl.pallas_call(
        functools.partial(_gather_kernel, tg=tg),
        out_shape=jax.ShapeDtypeStruct((n_tiles * tg, d), BF16),
        grid_spec=pltpu.PrefetchScalarGridSpec(
            num_scalar_prefetch=1,
            grid=(n_tiles,),
            in_specs=[pl.BlockSpec(memory_space=pl.ANY)],
            out_specs=pl.BlockSpec((tg, d), lambda t, idx: (t, 0)),
            scratch_shapes=[pltpu.VMEM((tg, d), F32), pltpu.SemaphoreType.DMA(())]),
        compiler_params=_cparams(("arbitrary",)),
        name="moe_gather",
    )(slot_tok, h2d)


def _expert_kernel(te_ref, nu_ref, x_ref, wg_ref, wu_ref, wd_ref, o_ref, acc_scr):
    t = pl.program_id(0)
    f = pl.program_id(1)
    live = t < nu_ref[0]

    @pl.when(live & (f == 0))
    def _():
        acc_scr[...] = jnp.zeros(acc_scr.shape, F32)

    @pl.when(live)
    def _():
        x = x_ref[...]
        a = _dot(x, wg_ref[...])
        b = _dot(x, wu_ref[...])
        hid = (a * _sigmoid(a) * b).astype(BF16)
        acc_scr[...] += _dot(hid, wd_ref[...])

    @pl.when(live & (f == pl.num_programs(1) - 1))
    def _():
        o_ref[...] = acc_scr[...]

    @pl.when(jnp.logical_not(live) & (f == pl.num_programs(1) - 1))
    def _():
        o_ref[...] = jnp.zeros(o_ref.shape, o_ref.dtype)


def _moe_experts(xg, tile_e, n_used, wg, wu, wd, *, tm, tf):
    n_slots, d = xg.shape
    n_tiles = n_slots // tm
    ff = wg.shape[2]
    nf = ff // tf

    def row_blk(t, f, te, nu):
        return (jnp.minimum(t, nu[0] - 1), 0)

    def f_blk(t, f, nu):
        return jnp.where(t < nu[0], f, nf - 1)

    return pl.pallas_call(
        _expert_kernel,
        out_shape=jax.ShapeDtypeStruct((n_slots, d), F32),
        grid_spec=pltpu.PrefetchScalarGridSpec(
            num_scalar_prefetch=2,
            grid=(n_tiles, nf),
            in_specs=[pl.BlockSpec((tm, d), row_blk),
                      pl.BlockSpec((None, d, tf), lambda t, f, te, nu: (te[t], 0, f_blk(t, f, nu))),
                      pl.BlockSpec((None, d, tf), lambda t, f, te, nu: (te[t], 0, f_blk(t, f, nu))),
                      pl.BlockSpec((None, tf, d), lambda t, f, te, nu: (te[t], f_blk(t, f, nu), 0))],
            out_specs=pl.BlockSpec((tm, d), lambda t, f, te, nu: (t, 0)),
            scratch_shapes=[pltpu.VMEM((tm, d), F32)]),
        compiler_params=_cparams(("arbitrary", "arbitrary")),
        name="moe_experts",
    )(tile_e, n_used, xg, wg, wu, wd)


def _combine_kernel(d1_ref, d2_ref, yb_hbm, route_ref, x_ref, gate_ref, fg_ref, o_ref, b1, b2, sem, *, tmc):
    t = pl.program_id(0)

    def start(r, c):
        _row_copy(yb_hbm, d1_ref[t, r], b1, r, sem).start()
        _row_copy(yb_hbm, d2_ref[t, r], b2, r, sem).start()
        return c

    def wait(r, c):
        _row_copy(yb_hbm, 0, b1, r, sem).wait()
        _row_copy(yb_hbm, 0, b2, r, sem).wait()
        return c

    lax.fori_loop(0, tmc, start, 0, unroll=ROW_DMA_UNROLL)
    lax.fori_loop(0, tmc, wait, 0, unroll=ROW_DMA_UNROLL)
    route = route_ref[...]
    y = route[:, 2:3] * b1[...] + route[:, 3:4] * b2[...]
    xo = x_ref[...] + gate_ref[...] * y
    ms = jnp.mean(xo * xo, axis=-1, keepdims=True)
    o_ref[...] = xo * lax.rsqrt(ms + RMS_EPS) * fg_ref[...]


def _moe_combine(yb, dest1, dest2, route, x2d, gate, final_g, *, seq, tmc):
    m, d = x2d.shape
    tpb = seq // tmc
    return pl.pallas_call(
        functools.partial(_combine_kernel, tmc=tmc),
        out_shape=jax.ShapeDtypeStruct((m, d), F32),
        grid_spec=pltpu.PrefetchScalarGridSpec(
            num_scalar_prefetch=2,
            grid=(m // tmc,),
            in_specs=[pl.BlockSpec(memory_space=pl.ANY),
                      pl.BlockSpec((tmc, LANES), lambda t, a, b: (t, 0)),
                      pl.BlockSpec((tmc, d), lambda t, a, b: (t, 0)),
                      pl.BlockSpec((None, 1, d), lambda t, a, b: (t // tpb, 0, 0)),
                      pl.BlockSpec((1, d), lambda t, a, b: (0, 0))],
            out_specs=pl.BlockSpec((tmc, d), lambda t, a, b: (t, 0)),
            scratch_shapes=[pltpu.VMEM((tmc, d), F32), pltpu.VMEM((tmc, d), F32),
                            pltpu.SemaphoreType.DMA(())]),
        compiler_params=_cparams(("arbitrary",)),
        name="moe_combine",
    )(dest1, dest2, yb, route, x2d, gate, final_g)


def _moe_plan(route, *, tm):
    n = route.shape[0]
    a = n * MOE_TOP_K
    flat_e = route[:, :MOE_TOP_K].astype(jnp.int32).reshape(a)
    onehot = (flat_e[:, None] == jnp.arange(N_EXPERTS, dtype=jnp.int32)[None, :]).astype(jnp.int32)
    csum = jnp.cumsum(onehot, axis=0)
    pos = jnp.sum(onehot * (csum - 1), axis=1)
    counts = csum[-1]
    pcounts = (counts + tm - 1) // tm * tm
    pends = jnp.cumsum(pcounts)
    pstarts = pends - pcounts
    dest = (pstarts[flat_e] + pos).astype(jnp.int32)
    n_tiles = a // tm + N_EXPERTS
    tok = jnp.arange(a, dtype=jnp.int32) // MOE_TOP_K
    slot_tok = jnp.zeros((n_tiles * tm,), jnp.int32).at[dest].set(tok)
    n_used = (pends[-1] // tm).astype(jnp.int32)
    tile_e = jnp.minimum(jnp.searchsorted(pends, jnp.arange(n_tiles, dtype=jnp.int32) * tm, side='right'),
                         N_EXPERTS - 1).astype(jnp.int32)
    tile_e = jnp.where(jnp.arange(n_tiles) < n_used, tile_e, tile_e[jnp.maximum(n_used - 1, 0)])
    return dest.reshape(n, MOE_TOP_K), slot_tok.reshape(n_tiles, tm), tile_e, n_used.reshape(1)


def _tile(n, pref):
    t = min(n, pref)
    assert n % t == 0, (n, pref)
    return t


def kernel(x, c, positions, rel_bias, ada_w, ada_b, mix_norm_g, ffn_norm_g, ab_w_in, ab_w_out, nsa_cmp_pos_k, nsa_cmp_w1_k, nsa_cmp_w2_k, nsa_cmp_pos_v, nsa_cmp_w1_v, nsa_cmp_w2_v, mla_q_norm_g, mla_kv_norm_g, mla_w_uq, mla_w_ukv, ffn_w_gate, ffn_w_up, ffn_w_down, c_w_in, c_w_out, moe_w_router, moe_w_gate, moe_w_up, moe_w_down, final_norm_g):
    b, s, d = x.shape
    assert ada_w.shape[0] == 2 and d == D_MODEL and s % 256 == 0
    m = b * s
    x2d = x.reshape(m, d)
    tm_big = _tile(s, 1024)
    tm_mid = _tile(s, 512)
    t_att = _tile(s, 256)

    mod = _ada_mod(c, ada_w, ada_b)
    mods = mod.reshape(2, b, 6, 1, d)

    def layer_mod(i):
        sh_m, sc_m, g_m, sh_f, sc_f, g_f = (mods[i, :, j] for j in range(6))
        return (mix_norm_g[i][None, None, :] * (1.0 + sc_m), sh_m, g_m,
                ffn_norm_g[i][None, None, :] * (1.0 + sc_f), sh_f, g_f)

    geff_m, sh_m, g_m, geff_f, sh_f, g_f = layer_mod(0)
    w0 = ab_w_in[0]
    c_q, c_kv, c_g, c_ql, c_kvl = np.cumsum([NSA_HEADS * HEAD_DIM, 6 * NSA_KV_HEADS * HEAD_DIM,
                                             3 * NSA_HEADS, Q_LORA, KV_LORA]).tolist()
    zpad = lambda n: jnp.zeros((d, n), w0.dtype)
    w_in0 = jnp.concatenate([w0[:, :c_kv], w0[:, c_ql:c_kvl], w0[:, c_kvl:], w0[:, c_kv:c_g],
                             zpad(LANES - QK_ROPE - 3 * NSA_HEADS), w0[:, c_g:c_ql], zpad(LANES)],
                            axis=1).astype(BF16)
    assert w_in0.shape[1] == Z0_W
    z0 = _nmm(x2d, geff_m, sh_m, w_in0, seq=s, tm=tm_big, tn=Z0_W // 3, norm_cols=d, x_block=0,
              out_dtype=BF16, name="proj_in_ab")
    z0_3d = z0.reshape(b, s, Z0_W)

    inv_freq = ROPE_THETA ** (-jnp.arange(0, QK_ROPE, 2, dtype=F32) / QK_ROPE)
    ang = positions.astype(F32)[..., None] * inv_freq
    cos, sin = jnp.cos(ang).reshape(m, -1), jnp.sin(ang).reshape(m, -1)
    hr = QK_ROPE // 2
    zr = lambda n: jnp.zeros((m, n), F32)
    rope_tabs = (jnp.concatenate([cos, cos, zr(LANES - 2 * hr)], axis=1),
                 jnp.concatenate([-sin, zr(LANES - hr)], axis=1),
                 jnp.concatenate([zr(hr), sin, zr(LANES - 2 * hr)], axis=1))
    hw = MXU_DIM
    wq3 = mla_w_uq[0].reshape(Q_LORA, MLA_HEADS, QK_NOPE + QK_ROPE)
    wq = jnp.concatenate([wq3, jnp.zeros((Q_LORA, MLA_HEADS, hw - QK_NOPE - QK_ROPE), F32)], axis=2)
    wq = jnp.concatenate([wq.reshape(Q_LORA, MLA_HEADS * hw), jnp.zeros((LANES, MLA_HEADS * hw), F32)],
                         axis=0).astype(BF16)
    wkv3 = mla_w_ukv[0].reshape(KV_LORA, MLA_HEADS, QK_NOPE + V_DIM)
    wk_top = jnp.concatenate([wkv3[:, :, :QK_NOPE], jnp.zeros((KV_LORA, MLA_HEADS, hw - QK_NOPE), F32)],
                             axis=2).reshape(KV_LORA, MLA_HEADS * hw)
    pe_pass = jnp.zeros((LANES, hw), F32).at[jnp.arange(QK_ROPE), QK_NOPE + jnp.arange(QK_ROPE)].set(1.0)
    wk_bot = jnp.tile(pe_pass, (1, MLA_HEADS))
    wv = jnp.concatenate([wkv3[:, :, QK_NOPE:].reshape(KV_LORA, MLA_HEADS * V_DIM),
                          jnp.zeros((LANES, MLA_HEADS * V_DIM), F32)], axis=0)
    wkv = jnp.concatenate([jnp.concatenate([wk_top, wk_bot], axis=0), wv], axis=1).astype(BF16)
    ones_b = lambda g: jnp.broadcast_to(g[None, None, :], (b, 1, g.shape[0]))
    zeros_lat = jnp.zeros((b, 1, Q_LORA), F32)
    q_mla = _nmm(z0, ones_b(mla_q_norm_g[0]), zeros_lat, wq, seq=s, tm=tm_big, tn=1024, norm_cols=Q_LORA,
                 x_block=Z0_QLAT // LAT_BLOCK, out_dtype=BF16, rope=rope_tabs, rope_tiles=2,
                 name="mla_q_up")
    kv_mla = _nmm(z0, ones_b(mla_kv_norm_g[0]), zeros_lat, wkv, seq=s, tm=tm_big, tn=1024,
                  norm_cols=KV_LORA, x_block=Z0_KVLAT // LAT_BLOCK, out_dtype=BF16, rope=rope_tabs,
                  rope_tiles=2, name="mla_kv_up")
    o_mla = _flash(q_mla.reshape(b, s, -1), kv_mla.reshape(b, s, -1), kv_mla.reshape(b, s, -1),
                   t=t_att, n_kvh=MLA_HEADS, r_n=1, dk=hw, q_blk=0, k_blk=0,
                   v_blk=(MLA_HEADS * hw) // (MLA_HEADS * V_DIM), scale=(QK_NOPE + QK_ROPE) ** -0.5,
                   out_dtype=BF16, name="mla_attn")

    n16 = s // CMP_STRIDE
    cmp_cols = z0_3d[:, :, Z0_KV:Z0_KV + 2 * NSA_KV_HEADS * HEAD_DIM]
    x16 = cmp_cols.reshape(b, n16, CMP_STRIDE, 2 * NSA_KV_HEADS, HEAD_DIM).transpose(0, 3, 1, 2, 4)
    x16 = x16.reshape(b, 2 * NSA_KV_HEADS, n16, CMP_STRIDE * HEAD_DIM)
    pos_kv = jnp.stack([nsa_cmp_pos_k[0], nsa_cmp_pos_v[0]]).reshape(2, 1, CMP_LEN * HEAD_DIM).astype(BF16)
    w1_kv = jnp.stack([nsa_cmp_w1_k[0], nsa_cmp_w1_v[0]]).astype(BF16)
    w2_kv = jnp.stack([nsa_cmp_w2_k[0], nsa_cmp_w2_v[0]]).astype(BF16)
    kvc = _nsa_compress(x16, pos_kv, w1_kv, w2_kv)

    n_cmp = (s - CMP_LEN) // CMP_STRIDE + 1
    n_slc = s // SLC_LEN
    ratio, span = SLC_LEN // CMP_STRIDE, CMP_LEN // CMP_STRIDE
    cm = np.zeros((n16, n_slc), np.float32)
    for j in range(n_slc):
        for mm in range(ratio):
            for nn in range(span):
                i = ratio * j + mm - nn
                if 0 <= i < n_cmp:
                    cm[i, j] += 1.0
    tab_s = _dist_table(rel_bias, s)
    bias_c = _toeplitz_tiles(tab_s, [r - (CMP_LEN - 1) for r in range(CMP_STRIDE)], n16, 0, s - 1,
                             mult=CMP_STRIDE)
    bias_c = bias_c.transpose(0, 2, 1, 3).reshape(NSA_HEADS, s, n16)
    o_nsa, sel = _nsa_cmp_attention(z0_3d, kvc, bias_c, jnp.asarray(cm), tq=t_att)

    nq = s // t_att
    nd = min(nq, -(-(T5_MAX_DIST + t_att - 1) // t_att) + 1)
    bias_d = _toeplitz_tiles(_dist_table(rel_bias, nd * t_att), [dd * t_att for dd in range(nd)], t_att,
                             0, nd * t_att - 1)
    kvw = NSA_KV_HEADS * HEAD_DIM
    o_nsa = _flash(z0_3d, z0_3d, z0_3d, t=t_att, n_kvh=NSA_KV_HEADS, r_n=NSA_GROUP, dk=HEAD_DIM,
                   q_blk=0, k_blk=(Z0_KV + 2 * kvw) // kvw, v_blk=(Z0_KV + 3 * kvw) // kvw,
                   scale=HEAD_DIM ** -0.5, out_dtype=F32, bias=bias_d, sel=sel, misc=z0_3d,
                   misc_blk=Z0_MISC // LANES, prev=o_nsa, gate_branch=1, name="nsa_slc_attn")

    npv_w = -(-(WIN - 1) // t_att)
    bias_w = _band_bias(rel_bias, t_att, t_att, npv_w, WIN - 1, 1)
    (o_nsa,) = _band(z0_3d, z0_3d, bias_w, lead_grid=(b,), tq=t_att, pb=t_att, npv=npv_w,
                     n_kvh=NSA_KV_HEADS, r_n=NSA_GROUP,
                     qmap=lambda i, j: (i, 0), kmap=lambda i, j: (i, (Z0_KV + 4 * kvw) // kvw),
                     vmap=lambda i, j: (i, (Z0_KV + 5 * kvw) // kvw), omap=lambda i, j: (i, 0),
                     out_shape=jax.ShapeDtypeStruct((b, s, NSA_HEADS * HEAD_DIM), F32),
                     misc=z0_3d, miscmap=lambda i, j: (i, Z0_MISC // LANES), prev=o_nsa, gate_branch=2,
                     scale=HEAD_DIM ** -0.5, name="nsa_win_attn")

    x2d = _out_ab(o_nsa.reshape(m, -1), o_mla.reshape(m, -1), ab_w_out[0].astype(BF16), x2d, g_m,
                  seq=s, tm=tm_mid)

    x2d = _ffn(x2d, geff_f, sh_f, ffn_w_gate[0].astype(BF16), ffn_w_up[0].astype(BF16),
               ffn_w_down[0].astype(BF16), g_f, seq=s, tm=tm_mid, tf=512)

    geff_m, sh_m, g_m, geff_f, sh_f, g_f = layer_mod(1)
    cw = c_w_in.shape[2]
    z1 = _nmm(x2d, geff_m, sh_m, c_w_in[0].astype(BF16), seq=s, tm=tm_big, tn=1024, norm_cols=d,
              x_block=0, out_dtype=BF16, name="proj_in_c")
    hw_c = DIL_HEADS * HEAD_DIM
    os_, lses = [], []
    for gidx, (win, dil) in enumerate(DIL_PATTERNS):
        ls = s // dil
        tq = _tile(ls, 256)
        pb = min(tq, 128)
        max_back = win // dil
        npv = -(-max_back // pb)
        bias_g = _band_bias(rel_bias, tq, pb, npv, max_back, dil)
        zv = z1.reshape(b, ls, dil * cw)
        cpr = cw // hw_c
        og, lg = _band(zv, zv, bias_g, lead_grid=(b, dil), tq=tq, pb=pb, npv=npv, n_kvh=DIL_HEADS, r_n=1,
                       qmap=lambda i, r, j, gi=gidx: (i, r * cpr + gi * 3),
                       kmap=lambda i, r, j, gi=gidx: (i, r * cpr + gi * 3 + 1),
                       vmap=lambda i, r, j, gi=gidx: (i, r * cpr + gi * 3 + 2),
                       omap=lambda i, r, j: (i, r), lmap=lambda i, r, j: (i, r),
                       out_shape=jax.ShapeDtypeStruct((b, ls, dil * hw_c), F32),
                       lse_shape=jax.ShapeDtypeStruct((b, ls, dil * LANES), F32),
                       scale=HEAD_DIM ** -0.5, name=f"dil_attn_{gidx}")
        os_.append(og.reshape(m, hw_c))
        lses.append(lg.reshape(m, LANES))
    x2d = _out_c(os_, lses, c_w_out[0].astype(BF16), x2d, g_m, seq=s, tm=tm_mid)

    wr = jnp.concatenate([moe_w_router[0], jnp.zeros((d, LANES - N_EXPERTS), F32)], axis=1)
    h_moe, route = _moe_prep(x2d, geff_f, sh_f, wr, seq=s, tm=tm_mid)
    tm_e = 512
    dest, slot_tok, tile_e, n_used = _moe_plan(route, tm=tm_e)
    xg = _moe_gather(h_moe, slot_tok, tg=tm_e)
    yb = _moe_experts(xg, tile_e, n_used, moe_w_gate[0].astype(BF16), moe_w_up[0].astype(BF16),
                      moe_w_down[0].astype(BF16), tm=tm_e, tf=512)
    tmc = _tile(s, 256)
    out = _moe_combine(yb, dest[:, 0].reshape(m // tmc, tmc), dest[:, 1].reshape(m // tmc, tmc), route,
                       x2d, g_f, final_norm_g.reshape(1, d), seq=s, tmc=tmc)
    return out.reshape(b, s, d)
```

```python
import functools
import math

import numpy as np
import jax
import jax.numpy as jnp
from jax import lax
from jax.experimental import pallas as pl
from jax.experimental.pallas import tpu as pltpu

F32 = jnp.float32
BF16 = jnp.bfloat16

D_MODEL = 2048
HEAD_DIM = 128
NEG_INF = -1e30
LOG2E = 1.4426950408889634
RMS_EPS = 1e-6
NUM_BUCKETS = 32
T5_MAX_DIST = 2048
NSA_HEADS = 8
NSA_KV_HEADS = 2
NSA_GROUP = NSA_HEADS // NSA_KV_HEADS
CMP_LEN = 32
CMP_STRIDE = 16
SLC_LEN = 64
N_SEL = 16
WIN = 512
FORCE_SCORE = 1e9
MLA_HEADS = 8
Q_LORA = 512
KV_LORA = 512
QK_NOPE = 128
QK_ROPE = 64
V_DIM = 128
ROPE_THETA = 10000.0
DIL_PATTERNS = ((128, 1), (512, 4), (2048, 16))
DIL_HEADS = 8
D_FF = 5632
N_EXPERTS = 8
MOE_TOP_K = 2
D_FF_EXPERT = 7168

LANES = 128
MXU_DIM = 256
VMEM_LIMIT_BYTES = 56 * 1024 * 1024
ROW_DMA_UNROLL = 8

Z0_Q = 0
Z0_KV = Z0_Q + NSA_HEADS * HEAD_DIM
Z0_KVLAT = Z0_KV + 6 * NSA_KV_HEADS * HEAD_DIM
Z0_MISC = Z0_KVLAT + KV_LORA
Z0_QLAT = Z0_MISC + LANES
Z0_W = Z0_QLAT + Q_LORA + LANES
LAT_BLOCK = KV_LORA + LANES
GATE_LANE0 = QK_ROPE


def _cparams(sem, vmem=VMEM_LIMIT_BYTES):
    return pltpu.CompilerParams(dimension_semantics=sem, vmem_limit_bytes=vmem)


def _dot(a, b):
    return jnp.dot(a, b, preferred_element_type=F32)


def _dot_nt(a, b):
    return lax.dot_general(a, b, (((1,), (1,)), ((), ())), preferred_element_type=F32)


def _sigmoid(x):
    return 1.0 / (1.0 + jnp.exp(-x))


def _t5_bucket(dist):
    n = jnp.maximum(dist, 0)
    max_exact = NUM_BUCKETS // 2
    nf = jnp.maximum(n, 1).astype(F32)
    large = max_exact + (jnp.log(nf / max_exact) / math.log(T5_MAX_DIST / max_exact)
                         * (NUM_BUCKETS - max_exact)).astype(jnp.int32)
    large = jnp.minimum(large, NUM_BUCKETS - 1)
    return jnp.where(n < max_exact, n, large)


def _ada_kernel(c_ref, w_ref, b_ref, o_ref):
    c = c_ref[...]
    cs = c * _sigmoid(c)
    o_ref[...] = _dot(cs.astype(BF16), w_ref[...].astype(BF16)) + b_ref[...]


def _ada_mod(c, ada_w, ada_b):
    depth, d, n = ada_w.shape
    b = c.shape[0]
    bp = 8
    cpad = jnp.zeros((bp, d), F32).at[:b].set(c)
    tn = 1024
    out = pl.pallas_call(
        _ada_kernel,
        out_shape=jax.ShapeDtypeStruct((depth, bp, n), F32),
        grid=(depth, n // tn),
        in_specs=[pl.BlockSpec((bp, d), lambda l, j: (0, 0)),
                  pl.BlockSpec((None, d, tn), lambda l, j: (l, 0, j)),
                  pl.BlockSpec((None, 1, tn), lambda l, j: (l, 0, j))],
        out_specs=pl.BlockSpec((None, bp, tn), lambda l, j: (l, 0, j)),
        compiler_params=_cparams(("parallel", "parallel")),
        name="ada_mod",
    )(cpad, ada_w, ada_b.reshape(depth, 1, n))
    return out[:, :b]


def _nmm_kernel(x_ref, g_ref, s_ref, w_ref, *rest, norm_cols, rope_tiles, tn):
    if rope_tiles:
        c_ref, s1_ref, s2_ref, o_ref, h_scr = rest
    else:
        o_ref, h_scr = rest
    j = pl.program_id(1)

    @pl.when(j == 0)
    def _():
        x = x_ref[...].astype(F32)
        xn = x[:, :norm_cols]
        ms = jnp.mean(xn * xn, axis=-1, keepdims=True)
        hn = xn * lax.rsqrt(ms + RMS_EPS) * g_ref[...] + s_ref[...]
        h_scr[:, :norm_cols] = hn.astype(BF16)
        if norm_cols < x.shape[1]:
            h_scr[:, norm_cols:] = x[:, norm_cols:].astype(BF16)

    acc = _dot(h_scr[...], w_ref[...])
    o_ref[...] = acc.astype(o_ref.dtype)
    if rope_tiles:
        @pl.when(j < rope_tiles)
        def _():
            for hh in range(tn // MXU_DIM):
                lo = hh * MXU_DIM + LANES
                y = acc[:, lo:lo + LANES]
                y2 = (y * c_ref[...] + pltpu.roll(y, LANES - 32, 1) * s1_ref[...]
                      + pltpu.roll(y, 32, 1) * s2_ref[...])
                o_ref[:, lo:lo + LANES] = y2.astype(o_ref.dtype)


def _nmm(x2d, geff, shift, w, *, seq, tm, tn, norm_cols, x_block, out_dtype, rope=None,
         rope_tiles=0, name):
    m = x2d.shape[0]
    k, n = w.shape
    tpb = seq // tm
    in_specs = [pl.BlockSpec((tm, k), lambda i, j: (i, x_block)),
                pl.BlockSpec((None, 1, norm_cols), lambda i, j: (i // tpb, 0, 0)),
                pl.BlockSpec((None, 1, norm_cols), lambda i, j: (i // tpb, 0, 0)),
                pl.BlockSpec((k, tn), lambda i, j: (0, j))]
    args = [x2d, geff, shift, w]
    if rope_tiles:
        for t in rope:
            in_specs.append(pl.BlockSpec((tm, LANES), lambda i, j: (i, 0)))
            args.append(t)
    return pl.pallas_call(
        functools.partial(_nmm_kernel, norm_cols=norm_cols, rope_tiles=rope_tiles, tn=tn),
        out_shape=jax.ShapeDtypeStruct((m, n), out_dtype),
        grid=(m // tm, n // tn),
        in_specs=in_specs,
        out_specs=pl.BlockSpec((tm, tn), lambda i, j: (i, j)),
        scratch_shapes=[pltpu.VMEM((tm, k), BF16)],
        compiler_params=_cparams(("parallel", "arbitrary")),
        name=name,
    )(*args)


def _cmp_kernel(x_ref, pos_ref, w1_ref, w2_ref, o_ref):
    half = w1_ref.shape[0] // 2
    x = x_ref[...]
    n16 = x.shape[0]
    a = _dot(x, w1_ref[:half, :])
    b = _dot(x, w1_ref[half:, :])
    c = _dot(pos_ref[...], w1_ref[...])
    hid = a + pltpu.roll(b, n16 - 1, 0) + c
    hid = jax.nn.gelu(hid, approximate=True)
    o_ref[...] = _dot(hid.astype(BF16), w2_ref[...]).astype(o_ref.dtype)


def _nsa_compress(x16, pos, w1, w2):
    b, nkv, n16, kk = x16.shape
    g = NSA_KV_HEADS
    return pl.pallas_call(
        _cmp_kernel,
        out_shape=jax.ShapeDtypeStruct((b, nkv, n16, HEAD_DIM), BF16),
        grid=(b, nkv),
        in_specs=[pl.BlockSpec((None, None, n16, kk), lambda i, j: (i, j, 0, 0)),
                  pl.BlockSpec((None, 1, 2 * kk), lambda i, j: (j // g, 0, 0)),
                  pl.BlockSpec((None, 2 * kk, HEAD_DIM), lambda i, j: (j // g, 0, 0)),
                  pl.BlockSpec((None, HEAD_DIM, HEAD_DIM), lambda i, j: (j // g, 0, 0))],
        out_specs=pl.BlockSpec((None, None, n16, HEAD_DIM), lambda i, j: (i, j, 0, 0)),
        compiler_params=_cparams(("parallel", "parallel")),
        name="nsa_compress",
    )(x16, pos, w1, w2)


def _cmpattn_kernel(q_ref, kv_ref, bias_ref, cmat_ref, misc_ref, o_ref, sel_ref, *, tq, n_slc,
                    n_top, scale):
    qi = pl.program_id(1)
    g_n, r_n = NSA_KV_HEADS, NSA_GROUP
    ncp = kv_ref.shape[1]
    trow = qi * tq + lax.broadcasted_iota(jnp.int32, (tq, 1), 0)
    has_c = trow >= (CMP_LEN - 1)
    blk_t = lax.shift_right_logical(trow, int(math.log2(SLC_LEN))).astype(F32)
    jb = lax.broadcasted_iota(jnp.int32, (tq, n_slc), 1).astype(F32)
    forced = (jb == 0.0) | (jb == blk_t) | (jb == blk_t - 1.0)
    valid = jb <= blk_t
    misc = misc_ref[...].astype(F32)
    for g in range(g_n):
        kc = kv_ref[g]
        vc = kv_ref[g_n + g]
        imp = jnp.zeros((tq, ncp), F32)
        for r in range(r_n):
            h = g * r_n + r
            q = q_ref[:, h * HEAD_DIM:(h + 1) * HEAD_DIM]
            s = _dot_nt(q, kc) * scale + bias_ref[h]
            m = jnp.max(s, axis=-1, keepdims=True)
            e = jnp.exp(s - m)
            p = e * (1.0 / jnp.sum(e, axis=-1, keepdims=True))
            p = jnp.where(has_c, p, 0.0)
            imp = imp + p
            o = _dot(p.astype(BF16), vc)
            c0 = GATE_LANE0 + h * 3
            gate = _sigmoid(misc[:, c0:c0 + 1])
            o_ref[:, h * HEAD_DIM:(h + 1) * HEAD_DIM] = gate * o
        imp_s = jnp.dot(imp, cmat_ref[...], precision=lax.Precision.HIGHEST,
                        preferred_element_type=F32)
        score = jnp.where(forced, FORCE_SCORE, jnp.where(valid, imp_s, -1.0))
        sel = jnp.zeros((tq, n_slc), F32)
        for _ in range(n_top):
            mx = jnp.max(score, axis=-1, keepdims=True)
            first = jnp.min(jnp.where(score == mx, jb, float(n_slc)), axis=-1, keepdims=True)
            hit = jb == first
            sel = jnp.where(hit, jnp.where(mx > -0.5, 1.0, 0.0), sel)
            score = jnp.where(hit, -3e38, score)
        sel_ref[g] = sel.astype(sel_ref.dtype)


def _nsa_cmp_attention(z0, kvc, bias_c, cmat, *, tq):
    b, s, _ = z0.shape
    ncp = kvc.shape[2]
    n_slc = s // SLC_LEN
    n_top = min(N_SEL, n_slc)
    qw = NSA_HEADS * HEAD_DIM
    return pl.pallas_call(
        functools.partial(_cmpattn_kernel, tq=tq, n_slc=n_slc, n_top=n_top, scale=HEAD_DIM ** -0.5),
        out_shape=(jax.ShapeDtypeStruct((b, s, qw), F32),
                   jax.ShapeDtypeStruct((b, NSA_KV_HEADS, s, n_slc), BF16)),
        grid=(b, s // tq),
        in_specs=[pl.BlockSpec((None, tq, qw), lambda i, j: (i, j, Z0_Q // qw)),
                  pl.BlockSpec((None, 2 * NSA_KV_HEADS, ncp, HEAD_DIM), lambda i, j: (i, 0, 0, 0)),
                  pl.BlockSpec((NSA_HEADS, tq, ncp), lambda i, j: (0, j, 0)),
                  pl.BlockSpec((ncp, n_slc), lambda i, j: (0, 0)),
                  pl.BlockSpec((None, tq, LANES), lambda i, j: (i, j, Z0_MISC // LANES))],
        out_specs=(pl.BlockSpec((None, tq, qw), lambda i, j: (i, j, 0)),
                   pl.BlockSpec((None, NSA_KV_HEADS, tq, n_slc), lambda i, j: (i, 0, j, 0))),
        compiler_params=_cparams(("parallel", "parallel")),
        name="nsa_cmp_attn",
    )(z0, kvc, bias_c, cmat, z0)


def _flash_kernel(qi_ref, ki_ref, q_ref, k_ref, v_ref, *rest, t, n_kvh, r_n, dk, scale, has_bias,
                  has_sel, gate_branch):
    rest = list(rest)
    bias_ref = rest.pop(0) if has_bias else None
    sel_ref = rest.pop(0) if has_sel else None
    if gate_branch is not None:
        misc_ref = rest.pop(0)
        prev_ref = rest.pop(0)
    o_ref, qs_scr, m_scr, acc_scr = rest
    pidx = pl.program_id(1)
    qi = qi_ref[pidx]
    ki = ki_ref[pidx]
    reps = t // LANES

    @pl.when(ki == 0)
    def _():
        qs_scr[...] = (q_ref[...].astype(F32) * (scale * LOG2E)).astype(BF16)
        m_scr[...] = jnp.full(m_scr.shape, NEG_INF, F32)
        acc_scr[...] = jnp.zeros(acc_scr.shape, F32)

    def step(diag):
        if has_sel:
            per = t // SLC_LEN
            erow = lax.broadcasted_iota(jnp.int32, (sel_ref.shape[2], t), 0)
            ecol = lax.broadcasted_iota(jnp.int32, (sel_ref.shape[2], t), 1)
            expand = jnp.where(erow == ki * per + lax.shift_right_logical(ecol, int(math.log2(SLC_LEN))),
                               1.0, 0.0).astype(BF16)
        if diag:
            causal = (lax.broadcasted_iota(jnp.int32, (t, t), 0)
                      >= lax.broadcasted_iota(jnp.int32, (t, t), 1))
        ones = jnp.ones((t, HEAD_DIM), BF16)
        for kh in range(n_kvh):
            k = k_ref[:, kh * dk:(kh + 1) * dk]
            v_ext = jnp.concatenate([v_ref[:, kh * HEAD_DIM:(kh + 1) * HEAD_DIM], ones], axis=1)
            madd = None
            if has_sel:
                madd = jnp.where(_dot(sel_ref[kh], expand) > 0.5, 0.0, NEG_INF)
                if diag:
                    madd = jnp.where(causal, madd, NEG_INF)
            elif diag:
                madd = jnp.where(causal, 0.0, NEG_INF)
            for r in range(r_n):
                h = kh * r_n + r
                s = _dot_nt(qs_scr[:, h * dk:(h + 1) * dk], k)
                if has_bias:
                    s = s + bias_ref[h]
                if madd is not None:
                    s = s + madd
                m_prev = m_scr[h]
                m_new = jnp.maximum(m_prev, jnp.max(s, axis=-1, keepdims=True))
                alpha = jnp.exp2(m_prev - m_new)
                p = jnp.exp2(s - jnp.concatenate([m_new] * reps, axis=1))
                acc_scr[h] = (jnp.concatenate([alpha, alpha], axis=1) * acc_scr[h]
                              + _dot(p.astype(BF16), v_ext))
                m_scr[h] = m_new

    @pl.when(ki < qi)
    def _():
        step(False)

    @pl.when(ki == qi)
    def _():
        step(True)
        if gate_branch is not None:
            misc = misc_ref[...].astype(F32)
        for h in range(n_kvh * r_n):
            a = acc_scr[h]
            o = a[:, :HEAD_DIM] * (1.0 / a[:, HEAD_DIM:])
            sl = slice(h * HEAD_DIM, (h + 1) * HEAD_DIM)
            if gate_branch is not None:
                c0 = GATE_LANE0 + h * 3 + gate_branch
                o = prev_ref[:, sl] + _sigmoid(misc[:, c0:c0 + 1]) * o
            o_ref[:, sl] = o.astype(o_ref.dtype)


def _tri_pairs(nq):
    qi = np.concatenate([np.full(i + 1, i, np.int32) for i in range(nq)])
    ki = np.concatenate([np.arange(i + 1, dtype=np.int32) for i in range(nq)])
    return jnp.asarray(qi), jnp.asarray(ki)


def _flash(q_arr, k_arr, v_arr, *, t, n_kvh, r_n, dk, q_blk, k_blk, v_blk, scale, out_dtype,
           bias=None, sel=None, misc=None, misc_blk=0, prev=None, gate_branch=None, name):
    b, s, _ = q_arr.shape
    nq = s // t
    qi_a, ki_a = _tri_pairs(nq)
    nh = n_kvh * r_n
    ow = nh * HEAD_DIM
    in_specs = [pl.BlockSpec((None, t, nh * dk), lambda i, p, qa, ka: (i, qa[p], q_blk)),
                pl.BlockSpec((None, t, n_kvh * dk), lambda i, p, qa, ka: (i, ka[p], k_blk)),
                pl.BlockSpec((None, t, n_kvh * HEAD_DIM), lambda i, p, qa, ka: (i, ka[p], v_blk))]
    args = [q_arr, k_arr, v_arr]
    if bias is not None:
        nd = bias.shape[1]
        in_specs.append(pl.BlockSpec((nh, None, t, t),
                                     lambda i, p, qa, ka: (0, jnp.minimum(qa[p] - ka[p], nd - 1), 0, 0)))
        args.append(bias)
    if sel is not None:
        n_slc = sel.shape[-1]
        in_specs.append(pl.BlockSpec((None, n_kvh, t, n_slc), lambda i, p, qa, ka: (i, 0, qa[p], 0)))
        args.append(sel)
    io_alias = {}
    if gate_branch is not None:
        in_specs.append(pl.BlockSpec((None, t, LANES), lambda i, p, qa, ka: (i, qa[p], misc_blk)))
        args.append(misc)
        in_specs.append(pl.BlockSpec((None, t, ow), lambda i, p, qa, ka: (i, qa[p], 0)))
        args.append(prev)
        io_alias = {2 + len(args) - 1: 0}
    return pl.pallas_call(
        functools.partial(_flash_kernel, t=t, n_kvh=n_kvh, r_n=r_n, dk=dk, scale=scale,
                          has_bias=bias is not None, has_sel=sel is not None, gate_branch=gate_branch),
        out_shape=jax.ShapeDtypeStruct((b, s, ow), out_dtype),
        grid_spec=pltpu.PrefetchScalarGridSpec(
            num_scalar_prefetch=2,
            grid=(b, int(qi_a.shape[0])),
            in_specs=in_specs,
            out_specs=pl.BlockSpec((None, t, ow), lambda i, p, qa, ka: (i, qa[p], 0)),
            scratch_shapes=[pltpu.VMEM((t, nh * dk), BF16), pltpu.VMEM((nh, t, LANES), F32),
                            pltpu.VMEM((nh, t, 2 * HEAD_DIM), F32)]),
        input_output_aliases=io_alias,
        compiler_params=_cparams(("parallel", "arbitrary")),
        name=name,
    )(qi_a, ki_a, *args)


def _band_kernel(q_ref, *rest, tq, pb, npv, n_kvh, r_n, scale, gate_branch, want_lse, qi_axis):
    rest = list(rest)
    kp = [rest.pop(0) for _ in range(npv)]
    kc = rest.pop(0)
    vp = [rest.pop(0) for _ in range(npv)]
    vc = rest.pop(0)
    bias_ref = rest.pop(0)
    if gate_branch is not None:
        misc_ref = rest.pop(0)
        prev_ref = rest.pop(0)
    o_ref = rest.pop(0)
    lse_ref = rest.pop(0) if want_lse else None
    qi = pl.program_id(qi_axis)
    nblk = tq // pb
    kw_prev = npv * pb
    if gate_branch is not None:
        misc = misc_ref[...].astype(F32)
    if want_lse:
        lane = lax.broadcasted_iota(jnp.int32, (tq, LANES), 1)
        lse_tile = jnp.zeros((tq, LANES), F32)
    for kh in range(n_kvh):
        ksl = slice(kh * HEAD_DIM, (kh + 1) * HEAD_DIM)
        for r in range(r_n):
            h = kh * r_n + r
            hsl = slice(h * HEAD_DIM, (h + 1) * HEAD_DIM)
            q = q_ref[:, hsl]
            parts = []
            for n in range(npv):
                pen = jnp.where(qi * nblk - npv + n >= 0, 0.0, NEG_INF)
                parts.append(_dot_nt(q, kp[n][:, ksl]) * scale + bias_ref[h, :, n * pb:(n + 1) * pb] + pen)
            parts.append(_dot_nt(q, kc[:, ksl]) * scale + bias_ref[h, :, kw_prev:])
            m = parts[0].max(axis=-1, keepdims=True)
            for sp in parts[1:]:
                m = jnp.maximum(m, sp.max(axis=-1, keepdims=True))
            l = jnp.zeros((tq, 1), F32)
            o = jnp.zeros((tq, HEAD_DIM), F32)
            for n, sp in enumerate(parts):
                p = jnp.exp(sp - m)
                l = l + jnp.sum(p, axis=-1, keepdims=True)
                vv = vp[n][:, ksl] if n < npv else vc[:, ksl]
                o = o + _dot(p.astype(BF16), vv)
            o = o * (1.0 / l)
            if gate_branch is not None:
                c0 = GATE_LANE0 + h * 3 + gate_branch
                o = prev_ref[:, hsl] + _sigmoid(misc[:, c0:c0 + 1]) * o
            o_ref[:, hsl] = o.astype(o_ref.dtype)
            if want_lse:
                lse_tile = jnp.where(lane == h, m + jnp.log(l), lse_tile)
    if want_lse:
        lse_ref[...] = lse_tile


def _band(q_arr, kv_arr, bias, *, lead_grid, tq, pb, npv, n_kvh, r_n, qmap, kmap, vmap, omap,
          lmap=None, out_shape, lse_shape=None, misc=None, miscmap=None, prev=None, gate_branch=None,
          scale, name):
    nl = len(lead_grid)
    nq = q_arr.shape[1] // tq
    nblk = tq // pb
    nh = n_kvh * r_n
    qw = nh * HEAD_DIM
    kw = n_kvh * HEAD_DIM

    def rows_cur(fn):
        def im(*g):
            bb, cc = fn(*g)
            return (bb, g[nl], cc)
        return im

    def rows_prev(fn, n):
        def im(*g):
            bb, cc = fn(*g)
            return (bb, jnp.maximum(g[nl] * nblk - npv + n, 0), cc)
        return im

    in_specs = [pl.BlockSpec((None, tq, qw), rows_cur(qmap))]
    args = [q_arr]
    for fn in (kmap, vmap):
        for n in range(npv):
            in_specs.append(pl.BlockSpec((None, pb, kw), rows_prev(fn, n)))
            args.append(kv_arr)
        in_specs.append(pl.BlockSpec((None, tq, kw), rows_cur(fn)))
        args.append(kv_arr)
    in_specs.append(pl.BlockSpec(bias.shape, lambda *g: (0, 0, 0)))
    args.append(bias)
    io_alias = {}
    if gate_branch is not None:
        in_specs.append(pl.BlockSpec((None, tq, LANES), rows_cur(miscmap)))
        args.append(misc)
        in_specs.append(pl.BlockSpec((None, tq, qw), rows_cur(omap)))
        args.append(prev)
        io_alias = {len(args) - 1: 0}
    out_shapes = [out_shape]
    out_specs = [pl.BlockSpec((None, tq, qw), rows_cur(omap))]
    if lse_shape is not None:
        out_shapes.append(lse_shape)
        out_specs.append(pl.BlockSpec((None, tq, LANES), rows_cur(lmap)))
    res = pl.pallas_call(
        functools.partial(_band_kernel, tq=tq, pb=pb, npv=npv, n_kvh=n_kvh, r_n=r_n, scale=scale,
                          gate_branch=gate_branch, want_lse=lse_shape is not None, qi_axis=nl),
        out_shape=tuple(out_shapes),
        grid=tuple(lead_grid) + (nq,),
        in_specs=in_specs,
        out_specs=tuple(out_specs),
        input_output_aliases=io_alias,
        compiler_params=_cparams(("parallel",) * (nl + 1)),
        name=name,
    )(*args)
    return res


def _dist_table(rel_bias, n_dist, dist_scale=1):
    tab = rel_bias[_t5_bucket(jnp.arange(n_dist) * dist_scale)].astype(F32)
    return jnp.concatenate([tab, jnp.full((1, tab.shape[1]), NEG_INF, F32)], axis=0)


def _toeplitz_tiles(tab, d0s, t, lo, hi, mult=1):
    masked = tab.shape[0] - 1
    u = np.arange(2 * t)
    i_minus_j = np.where(u < t, -u, 2 * t - u)
    dist = np.asarray(d0s)[:, None] + mult * i_minus_j[None, :]
    idx = np.where((dist >= lo) & (dist <= hi) & (u != t)[None, :], dist, masked)
    w = tab[jnp.asarray(idx, jnp.int32)].transpose(2, 0, 1)
    nh, nc = w.shape[0], w.shape[1]
    flat = jnp.tile(w, (1, 1, t))[:, :, :t * (2 * t - 1)]
    return flat.reshape(nh, nc, t, 2 * t - 1)[:, :, :, :t]


def _band_bias(rel_bias, tq, pb, npv, max_back, dist_scale):
    tab = _dist_table(rel_bias, max_back + 1, dist_scale)
    d0s = [(npv - n) * pb for n in range(npv)] + [0]
    tiles = _toeplitz_tiles(tab, d0s, tq, 0, max_back)
    parts = [tiles[:, n, :, :pb] for n in range(npv)] + [tiles[:, npv]]
    return jnp.concatenate(parts, axis=-1)


def _out_ab_kernel(oa_ref, ob_ref, wa_ref, wb_ref, x_ref, g_ref, o_ref):
    y = _dot(oa_ref[...].astype(BF16), wa_ref[...]) + _dot(ob_ref[...], wb_ref[...])
    o_ref[...] = x_ref[...] + g_ref[...] * y


def _out_ab(oa, ob, w, x2d, gate, *, seq, tm):
    m, d = x2d.shape
    ka = oa.shape[1]
    kb = ob.shape[1]
    tpb = seq // tm
    return pl.pallas_call(
        _out_ab_kernel,
        out_shape=jax.ShapeDtypeStruct((m, d), F32),
        grid=(m // tm,),
        in_specs=[pl.BlockSpec((tm, ka), lambda i: (i, 0)),
                  pl.BlockSpec((tm, kb), lambda i: (i, 0)),
                  pl.BlockSpec((ka, d), lambda i: (0, 0)),
                  pl.BlockSpec((kb, d), lambda i: (ka // kb, 0)),
                  pl.BlockSpec((tm, d), lambda i: (i, 0)),
                  pl.BlockSpec((None, 1, d), lambda i: (i // tpb, 0, 0))],
        out_specs=pl.BlockSpec((tm, d), lambda i: (i, 0)),
        compiler_params=_cparams(("parallel",)),
        name="out_proj_ab",
    )(oa, ob, w, w, x2d, gate)


def _out_c_kernel(o0_ref, o1_ref, o2_ref, l0_ref, l1_ref, l2_ref, w_ref, x_ref, g_ref, o_ref, mrg_scr):
    l0, l1, l2 = l0_ref[...], l1_ref[...], l2_ref[...]
    mx = jnp.maximum(jnp.maximum(l0, l1), l2)
    e0, e1, e2 = jnp.exp(l0 - mx), jnp.exp(l1 - mx), jnp.exp(l2 - mx)
    inv = 1.0 / (e0 + e1 + e2)
    w0, w1, w2 = e0 * inv, e1 * inv, e2 * inv
    for h in range(DIL_HEADS):
        sl = slice(h * HEAD_DIM, (h + 1) * HEAD_DIM)
        mg = (w0[:, h:h + 1] * o0_ref[:, sl] + w1[:, h:h + 1] * o1_ref[:, sl]
              + w2[:, h:h + 1] * o2_ref[:, sl])
        mrg_scr[:, sl] = mg.astype(BF16)
    o_ref[...] = x_ref[...] + g_ref[...] * _dot(mrg_scr[...], w_ref[...])


def _out_c(os_, lses, w, x2d, gate, *, seq, tm):
    m, d = x2d.shape
    kc = w.shape[0]
    tpb = seq // tm
    return pl.pallas_call(
        _out_c_kernel,
        out_shape=jax.ShapeDtypeStruct((m, d), F32),
        grid=(m // tm,),
        in_specs=[pl.BlockSpec((tm, kc), lambda i: (i, 0))] * 3
        + [pl.BlockSpec((tm, LANES), lambda i: (i, 0))] * 3
        + [pl.BlockSpec((kc, d), lambda i: (0, 0)),
           pl.BlockSpec((tm, d), lambda i: (i, 0)),
           pl.BlockSpec((None, 1, d), lambda i: (i // tpb, 0, 0))],
        out_specs=pl.BlockSpec((tm, d), lambda i: (i, 0)),
        scratch_shapes=[pltpu.VMEM((tm, kc), BF16)],
        compiler_params=_cparams(("parallel",)),
        name="out_proj_c",
    )(*os_, *lses, w, x2d, gate)


def _ffn_kernel(x_ref, g_ref, s_ref, wg_ref, wu_ref, wd_ref, gate_ref, o_ref, h_scr, acc_scr):
    f = pl.program_id(1)

    @pl.when(f == 0)
    def _():
        x = x_ref[...]
        ms = jnp.mean(x * x, axis=-1, keepdims=True)
        h_scr[...] = (x * lax.rsqrt(ms + RMS_EPS) * g_ref[...] + s_ref[...]).astype(BF16)
        acc_scr[...] = jnp.zeros(acc_scr.shape, F32)

    h = h_scr[...]
    a = _dot(h, wg_ref[...])
    b = _dot(h, wu_ref[...])
    hid = (a * _sigmoid(a) * b).astype(BF16)
    acc_scr[...] += _dot(hid, wd_ref[...])

    @pl.when(f == pl.num_programs(1) - 1)
    def _():
        o_ref[...] = x_ref[...] + gate_ref[...] * acc_scr[...]


def _ffn(x2d, geff, shift, wg, wu, wd, gate, *, seq, tm, tf):
    m, d = x2d.shape
    ff = wg.shape[1]
    tpb = seq // tm
    vec = pl.BlockSpec((None, 1, d), lambda i, f: (i // tpb, 0, 0))
    return pl.pallas_call(
        _ffn_kernel,
        out_shape=jax.ShapeDtypeStruct((m, d), F32),
        grid=(m // tm, ff // tf),
        in_specs=[pl.BlockSpec((tm, d), lambda i, f: (i, 0)), vec, vec,
                  pl.BlockSpec((d, tf), lambda i, f: (0, f)),
                  pl.BlockSpec((d, tf), lambda i, f: (0, f)),
                  pl.BlockSpec((tf, d), lambda i, f: (f, 0)),
                  vec],
        out_specs=pl.BlockSpec((tm, d), lambda i, f: (i, 0)),
        scratch_shapes=[pltpu.VMEM((tm, d), BF16), pltpu.VMEM((tm, d), F32)],
        compiler_params=_cparams(("parallel", "arbitrary")),
        name="ffn_swiglu",
    )(x2d, geff, shift, wg, wu, wd, gate)


def _moe_prep_kernel(x_ref, g_ref, s_ref, wr_ref, h_ref, route_ref):
    x = x_ref[...]
    ms = jnp.mean(x * x, axis=-1, keepdims=True)
    h = x * lax.rsqrt(ms + RMS_EPS) * g_ref[...] + s_ref[...]
    h_ref[...] = h
    logits = jnp.dot(h, wr_ref[...], precision=lax.Precision.HIGHEST, preferred_element_type=F32)
    lane = lax.broadcasted_iota(jnp.int32, logits.shape, 1).astype(F32)
    lg = jnp.where(lane < float(N_EXPERTS), logits, -3e38)
    m1 = jnp.max(lg, axis=-1, keepdims=True)
    i1 = jnp.min(jnp.where(lg == m1, lane, float(LANES)), axis=-1, keepdims=True)
    lg2 = jnp.where(lane == i1, -3e38, lg)
    m2 = jnp.max(lg2, axis=-1, keepdims=True)
    i2 = jnp.min(jnp.where(lg2 == m2, lane, float(LANES)), axis=-1, keepdims=True)
    e = jnp.exp(m2 - m1)
    inv = 1.0 / (1.0 + e)
    route = jnp.where(lane == 0.0, i1, jnp.where(lane == 1.0, i2, jnp.where(lane == 2.0, inv,
                      jnp.where(lane == 3.0, e * inv, 0.0))))
    route_ref[...] = route


def _moe_prep(x2d, geff, shift, w_router_pad, *, seq, tm):
    m, d = x2d.shape
    tpb = seq // tm
    vec = pl.BlockSpec((None, 1, d), lambda i: (i // tpb, 0, 0))
    return pl.pallas_call(
        _moe_prep_kernel,
        out_shape=(jax.ShapeDtypeStruct((m, d), F32), jax.ShapeDtypeStruct((m, LANES), F32)),
        grid=(m // tm,),
        in_specs=[pl.BlockSpec((tm, d), lambda i: (i, 0)), vec, vec,
                  pl.BlockSpec((d, LANES), lambda i: (0, 0))],
        out_specs=(pl.BlockSpec((tm, d), lambda i: (i, 0)), pl.BlockSpec((tm, LANES), lambda i: (i, 0))),
        compiler_params=_cparams(("parallel",)),
        name="moe_prep",
    )(x2d, geff, shift, w_router_pad)


def _row_copy(src_hbm, row, dst_vmem, slot, sem):
    return pltpu.make_async_copy(src_hbm.at[pl.ds(row, 1)], dst_vmem.at[pl.ds(slot, 1)], sem)


def _expert_kernel(te_ref, nu_ref, slot_ref, h_hbm, wg_ref, wu_ref, wd_ref, o_ref, xbuf, xs, sem, *,
                   tm, issue_steps):
    t = pl.program_id(0)
    f = pl.program_id(1)
    nf = pl.num_programs(1)
    n_used = nu_ref[0]
    live = t < n_used
    rows_per_step = tm // issue_steps

    @pl.when((t == 0) & (f == 0))
    def _():
        def start(r, c):
            _row_copy(h_hbm, slot_ref[0, r], xbuf, r, sem).start()
            return c
        lax.fori_loop(0, tm, start, 0, unroll=ROW_DMA_UNROLL)

    @pl.when(live & (f == 0))
    def _():
        def wait(r, c):
            _row_copy(h_hbm, 0, xbuf, r, sem).wait()
            return c
        lax.fori_loop(0, tm, wait, 0, unroll=ROW_DMA_UNROLL)
        xs[...] = xbuf[...].astype(BF16)
        o_ref[...] = jnp.zeros(o_ref.shape, o_ref.dtype)

    def compute():
        x = xs[...]
        a = _dot(x, wg_ref[...].astype(BF16))
        b = _dot(x, wu_ref[...].astype(BF16))
        hid = (a * _sigmoid(a) * b).astype(BF16)
        o_ref[...] += _dot(hid, wd_ref[...].astype(BF16))

    prefetch = live & (t + 1 < n_used) & (f >= 1) & (f <= issue_steps)

    @pl.when(prefetch)
    def _():
        base = (f - 1) * rows_per_step
        for j in range(rows_per_step):
            _row_copy(h_hbm, slot_ref[t + 1, base + j], xbuf, base + j, sem).start()
        compute()

    @pl.when(live & jnp.logical_not(prefetch))
    def _():
        compute()

    @pl.when(jnp.logical_not(live) & (f == nf - 1))
    def _():
        o_ref[...] = jnp.zeros(o_ref.shape, o_ref.dtype)


def _moe_experts(h2d, slot_tok, tile_e, n_used, wg, wu, wd, *, tm, tf, issue_steps):
    n_tiles = slot_tok.shape[0]
    d = h2d.shape[1]
    ff = wg.shape[2]
    nf = ff // tf
    assert tm % issue_steps == 0 and issue_steps < nf

    def f_blk(t, f, nu):
        return jnp.where(t < nu[0], f, nf - 1)

    return pl.pallas_call(
        functools.partial(_expert_kernel, tm=tm, issue_steps=issue_steps),
        out_shape=jax.ShapeDtypeStruct((n_tiles * tm, d), F32),
        grid_spec=pltpu.PrefetchScalarGridSpec(
            num_scalar_prefetch=3,
            grid=(n_tiles, nf),
            in_specs=[pl.BlockSpec(memory_space=pl.ANY),
                      pl.BlockSpec((None, d, tf), lambda t, f, te, nu, sl: (te[t], 0, f_blk(t, f, nu))),
                      pl.BlockSpec((None, d, tf), lambda t, f, te, nu, sl: (te[t], 0, f_blk(t, f, nu))),
                      pl.BlockSpec((None, tf, d), lambda t, f, te, nu, sl: (te[t], f_blk(t, f, nu), 0))],
            out_specs=pl.BlockSpec((tm, d), lambda t, f, te, nu, sl: (t, 0)),
            scratch_shapes=[pltpu.VMEM((tm, d), F32), pltpu.VMEM((tm, d), BF16),
                            pltpu.SemaphoreType.DMA(())]),
        compiler_params=_cparams(("arbitrary", "arbitrary")),
        name="moe_experts",
    )(tile_e, n_used, slot_tok, h2d, wg, wu, wd)


def _combine_kernel(d1_ref, d2_ref, yb_hbm, route_ref, x_ref, gate_ref, fg_ref, o_ref, b1, b2, sem, *, tmc):
    t = pl.program_id(0)

    def start(r, c):
        _row_copy(yb_hbm, d1_ref[t, r], b1, r, sem).start()
        _row_copy(yb_hbm, d2_ref[t, r], b2, r, sem).start()
        return c

    def wait(r, c):
        _row_copy(yb_hbm, 0, b1, r, sem).wait()
        _row_copy(yb_hbm, 0, b2, r, sem).wait()
        return c

    lax.fori_loop(0, tmc, start, 0, unroll=ROW_DMA_UNROLL)
    lax.fori_loop(0, tmc, wait, 0, unroll=ROW_DMA_UNROLL)
    route = route_ref[...]
    y = route[:, 2:3] * b1[...] + route[:, 3:4] * b2[...]
    xo = x_ref[...] + gate_ref[...] * y
    ms = jnp.mean(xo * xo, axis=-1, keepdims=True)
    o_ref[...] = xo * lax.rsqrt(ms + RMS_EPS) * fg_ref[...]


def _moe_combine(yb, dest1, dest2, route, x2d, gate, final_g, *, seq, tmc):
    m, d = x2d.shape
    tpb = seq // tmc
    return pl.pallas_call(
        functools.partial(_combine_kernel, tmc=tmc),
        out_shape=jax.ShapeDtypeStruct((m, d), F32),
        grid_spec=pltpu.PrefetchScalarGridSpec(
            num_scalar_prefetch=2,
            grid=(m // tmc,),
            in_specs=[pl.BlockSpec(memory_space=pl.ANY),
                      pl.BlockSpec((tmc, LANES), lambda t, a, b: (t, 0)),
                      pl.BlockSpec((tmc, d), lambda t, a, b: (t, 0)),
                      pl.BlockSpec((None, 1, d), lambda t, a, b: (t // tpb, 0, 0)),
                      pl.BlockSpec((1, d), lambda t, a, b: (0, 0))],
            out_specs=pl.BlockSpec((tmc, d), lambda t, a, b: (t, 0)),
            scratch_shapes=[pltpu.VMEM((tmc, d), F32), pltpu.VMEM((tmc, d), F32),
                            pltpu.SemaphoreType.DMA(())]),
        compiler_params=_cparams(("arbitrary",)),
        name="moe_combine",
    )(dest1, dest2, yb, route, x2d, gate, final_g)


def _moe_plan(route, *, tm):
    n = route.shape[0]
    a = n * MOE_TOP_K
    flat_e = route[:, :MOE_TOP_K].astype(jnp.int32).reshape(a)
    onehot = (flat_e[:, None] == jnp.arange(N_EXPERTS, dtype=jnp.int32)[None, :]).astype(jnp.int32)
    csum = jnp.cumsum(onehot, axis=0)
    pos = jnp.sum(onehot * (csum - 1), axis=1)
    counts = csum[-1]
    pcounts = (counts + tm - 1) // tm * tm
    pends = jnp.cumsum(pcounts)
    pstarts = pends - pcounts
    dest = (pstarts[flat_e] + pos).astype(jnp.int32)
    n_tiles = a // tm + N_EXPERTS
    tok = jnp.arange(a, dtype=jnp.int32) // MOE_TOP_K
    slot_tok = jnp.zeros((n_tiles * tm,), jnp.int32).at[dest].set(tok)
    n_used = (pends[-1] // tm).astype(jnp.int32)
    tile_e = jnp.minimum(jnp.searchsorted(pends, jnp.arange(n_tiles, dtype=jnp.int32) * tm, side='right'),
                         N_EXPERTS - 1).astype(jnp.int32)
    tile_e = jnp.where(jnp.arange(n_tiles) < n_used, tile_e, tile_e[jnp.maximum(n_used - 1, 0)])
    return dest.reshape(n, MOE_TOP_K), slot_tok.reshape(n_tiles, tm), tile_e, n_used.reshape(1)


def _tile(n, pref):
    t = min(n, pref)
    assert n % t == 0, (n, pref)
    return t


def kernel(x, c, positions, rel_bias, ada_w, ada_b, mix_norm_g, ffn_norm_g, ab_w_in, ab_w_out, nsa_cmp_pos_k, nsa_cmp_w1_k, nsa_cmp_w2_k, nsa_cmp_pos_v, nsa_cmp_w1_v, nsa_cmp_w2_v, mla_q_norm_g, mla_kv_norm_g, mla_w_uq, mla_w_ukv, ffn_w_gate, ffn_w_up, ffn_w_down, c_w_in, c_w_out, moe_w_router, moe_w_gate, moe_w_up, moe_w_down, final_norm_g):
    b, s, d = x.shape
    assert ada_w.shape[0] == 2 and d == D_MODEL and s % 256 == 0
    m = b * s
    x2d = x.reshape(m, d)
    tm_big = _tile(s, 1024)
    tm_mid = _tile(s, 512)
    t_att = _tile(s, 256)

    mod = _ada_mod(c, ada_w, ada_b)
    mods = mod.reshape(2, b, 6, 1, d)

    def layer_mod(i):
        sh_m, sc_m, g_m, sh_f, sc_f, g_f = (mods[i, :, j] for j in range(6))
        return (mix_norm_g[i][None, None, :] * (1.0 + sc_m), sh_m, g_m,
                ffn_norm_g[i][None, None, :] * (1.0 + sc_f), sh_f, g_f)

    geff_m, sh_m, g_m, geff_f, sh_f, g_f = layer_mod(0)
    w0 = ab_w_in[0]
    c_q, c_kv, c_g, c_ql, c_kvl = np.cumsum([NSA_HEADS * HEAD_DIM, 6 * NSA_KV_HEADS * HEAD_DIM,
                                             3 * NSA_HEADS, Q_LORA, KV_LORA]).tolist()
    zpad = lambda n: jnp.zeros((d, n), w0.dtype)
    w_in0 = jnp.concatenate([w0[:, :c_kv], w0[:, c_ql:c_kvl], w0[:, c_kvl:], w0[:, c_kv:c_g],
                             zpad(LANES - QK_ROPE - 3 * NSA_HEADS), w0[:, c_g:c_ql], zpad(LANES)],
                            axis=1).astype(BF16)
    assert w_in0.shape[1] == Z0_W
    z0 = _nmm(x2d, geff_m, sh_m, w_in0, seq=s, tm=tm_big, tn=Z0_W // 3, norm_cols=d, x_block=0,
              out_dtype=BF16, name="proj_in_ab")
    z0_3d = z0.reshape(b, s, Z0_W)

    inv_freq = ROPE_THETA ** (-jnp.arange(0, QK_ROPE, 2, dtype=F32) / QK_ROPE)
    ang = positions.astype(F32)[..., None] * inv_freq
    cos, sin = jnp.cos(ang).reshape(m, -1), jnp.sin(ang).reshape(m, -1)
    hr = QK_ROPE // 2
    zr = lambda n: jnp.zeros((m, n), F32)
    rope_tabs = (jnp.concatenate([cos, cos, zr(LANES - 2 * hr)], axis=1),
                 jnp.concatenate([-sin, zr(LANES - hr)], axis=1),
                 jnp.concatenate([zr(hr), sin, zr(LANES - 2 * hr)], axis=1))
    hw = MXU_DIM
    wq3 = mla_w_uq[0].reshape(Q_LORA, MLA_HEADS, QK_NOPE + QK_ROPE)
    wq = jnp.concatenate([wq3, jnp.zeros((Q_LORA, MLA_HEADS, hw - QK_NOPE - QK_ROPE), F32)], axis=2)
    wq = jnp.concatenate([wq.reshape(Q_LORA, MLA_HEADS * hw), jnp.zeros((LANES, MLA_HEADS * hw), F32)],
                         axis=0).astype(BF16)
    wkv3 = mla_w_ukv[0].reshape(KV_LORA, MLA_HEADS, QK_NOPE + V_DIM)
    wk_top = jnp.concatenate([wkv3[:, :, :QK_NOPE], jnp.zeros((KV_LORA, MLA_HEADS, hw - QK_NOPE), F32)],
                             axis=2).reshape(KV_LORA, MLA_HEADS * hw)
    pe_pass = jnp.zeros((LANES, hw), F32).at[jnp.arange(QK_ROPE), QK_NOPE + jnp.arange(QK_ROPE)].set(1.0)
    wk_bot = jnp.tile(pe_pass, (1, MLA_HEADS))
    wv = jnp.concatenate([wkv3[:, :, QK_NOPE:].reshape(KV_LORA, MLA_HEADS * V_DIM),
                          jnp.zeros((LANES, MLA_HEADS * V_DIM), F32)], axis=0)
    wkv = jnp.concatenate([jnp.concatenate([wk_top, wk_bot], axis=0), wv], axis=1).astype(BF16)
    ones_b = lambda g: jnp.broadcast_to(g[None, None, :], (b, 1, g.shape[0]))
    zeros_lat = jnp.zeros((b, 1, Q_LORA), F32)
    q_mla = _nmm(z0, ones_b(mla_q_norm_g[0]), zeros_lat, wq, seq=s, tm=tm_big, tn=1024, norm_cols=Q_LORA,
                 x_block=Z0_QLAT // LAT_BLOCK, out_dtype=BF16, rope=rope_tabs, rope_tiles=2,
                 name="mla_q_up")
    kv_mla = _nmm(z0, ones_b(mla_kv_norm_g[0]), zeros_lat, wkv, seq=s, tm=tm_big, tn=1024,
                  norm_cols=KV_LORA, x_block=Z0_KVLAT // LAT_BLOCK, out_dtype=BF16, rope=rope_tabs,
                  rope_tiles=2, name="mla_kv_up")
    o_mla = _flash(q_mla.reshape(b, s, -1), kv_mla.reshape(b, s, -1), kv_mla.reshape(b, s, -1),
                   t=t_att, n_kvh=MLA_HEADS, r_n=1, dk=hw, q_blk=0, k_blk=0,
                   v_blk=(MLA_HEADS * hw) // (MLA_HEADS * V_DIM), scale=(QK_NOPE + QK_ROPE) ** -0.5,
                   out_dtype=BF16, name="mla_attn")

    n16 = s // CMP_STRIDE
    cmp_cols = z0_3d[:, :, Z0_KV:Z0_KV + 2 * NSA_KV_HEADS * HEAD_DIM]
    x16 = cmp_cols.reshape(b, n16, CMP_STRIDE, 2 * NSA_KV_HEADS, HEAD_DIM).transpose(0, 3, 1, 2, 4)
    x16 = x16.reshape(b, 2 * NSA_KV_HEADS, n16, CMP_STRIDE * HEAD_DIM)
    pos_kv = jnp.stack([nsa_cmp_pos_k[0], nsa_cmp_pos_v[0]]).reshape(2, 1, CMP_LEN * HEAD_DIM).astype(BF16)
    w1_kv = jnp.stack([nsa_cmp_w1_k[0], nsa_cmp_w1_v[0]]).astype(BF16)
    w2_kv = jnp.stack([nsa_cmp_w2_k[0], nsa_cmp_w2_v[0]]).astype(BF16)
    kvc = _nsa_compress(x16, pos_kv, w1_kv, w2_kv)

    n_cmp = (s - CMP_LEN) // CMP_STRIDE + 1
    n_slc = s // SLC_LEN
    ratio, span = SLC_LEN // CMP_STRIDE, CMP_LEN // CMP_STRIDE
    cm = np.zeros((n16, n_slc), np.float32)
    for j in range(n_slc):
        for mm in range(ratio):
            for nn in range(span):
                i = ratio * j + mm - nn
                if 0 <= i < n_cmp:
                    cm[i, j] += 1.0
    tab_s = _dist_table(rel_bias, s)
    bias_c = _toeplitz_tiles(tab_s, [r - (CMP_LEN - 1) for r in range(CMP_STRIDE)], n16, 0, s - 1,
                             mult=CMP_STRIDE)
    bias_c = bias_c.transpose(0, 2, 1, 3).reshape(NSA_HEADS, s, n16)
    o_nsa, sel = _nsa_cmp_attention(z0_3d, kvc, bias_c, jnp.asarray(cm), tq=t_att)

    nq = s // t_att
    nd = min(nq, -(-(T5_MAX_DIST + t_att - 1) // t_att) + 1)
    bias_d = _toeplitz_tiles(_dist_table(rel_bias, nd * t_att) * LOG2E, [dd * t_att for dd in range(nd)],
                             t_att, 0, nd * t_att - 1)
    kvw = NSA_KV_HEADS * HEAD_DIM
    o_nsa = _flash(z0_3d, z0_3d, z0_3d, t=t_att, n_kvh=NSA_KV_HEADS, r_n=NSA_GROUP, dk=HEAD_DIM,
                   q_blk=0, k_blk=(Z0_KV + 2 * kvw) // kvw, v_blk=(Z0_KV + 3 * kvw) // kvw,
                   scale=HEAD_DIM ** -0.5, out_dtype=F32, bias=bias_d, sel=sel, misc=z0_3d,
                   misc_blk=Z0_MISC // LANES, prev=o_nsa, gate_branch=1, name="nsa_slc_attn")

    npv_w = -(-(WIN - 1) // t_att)
    bias_w = _band_bias(rel_bias, t_att, t_att, npv_w, WIN - 1, 1)
    (o_nsa,) = _band(z0_3d, z0_3d, bias_w, lead_grid=(b,), tq=t_att, pb=t_att, npv=npv_w,
                     n_kvh=NSA_KV_HEADS, r_n=NSA_GROUP,
                     qmap=lambda i, j: (i, 0), kmap=lambda i, j: (i, (Z0_KV + 4 * kvw) // kvw),
                     vmap=lambda i, j: (i, (Z0_KV + 5 * kvw) // kvw), omap=lambda i, j: (i, 0),
                     out_shape=jax.ShapeDtypeStruct((b, s, NSA_HEADS * HEAD_DIM), F32),
                     misc=z0_3d, miscmap=lambda i, j: (i, Z0_MISC // LANES), prev=o_nsa, gate_branch=2,
                     scale=HEAD_DIM ** -0.5, name="nsa_win_attn")

    x2d = _out_ab(o_nsa.reshape(m, -1), o_mla.reshape(m, -1), ab_w_out[0].astype(BF16), x2d, g_m,
                  seq=s, tm=tm_mid)

    x2d = _ffn(x2d, geff_f, sh_f, ffn_w_gate[0].astype(BF16), ffn_w_up[0].astype(BF16),
               ffn_w_down[0].astype(BF16), g_f, seq=s, tm=tm_mid, tf=512)

    geff_m, sh_m, g_m, geff_f, sh_f, g_f = layer_mod(1)
    cw = c_w_in.shape[2]
    z1 = _nmm(x2d, geff_m, sh_m, c_w_in[0].astype(BF16), seq=s, tm=tm_big, tn=1024, norm_cols=d,
              x_block=0, out_dtype=BF16, name="proj_in_c")
    hw_c = DIL_HEADS * HEAD_DIM
    z1_3d = z1.reshape(b, s, cw)
    os_, lses = [], []
    for gidx, (win, dil) in enumerate(DIL_PATTERNS):
        ls = s // dil
        tq = _tile(ls, 256)
        pb = min(tq, 128)
        max_back = win // dil
        npv = -(-max_back // pb)
        bias_g = _band_bias(rel_bias, tq, pb, npv, max_back, dil)
        if dil == 1:
            zv, cpr, c0 = z1_3d, cw // hw_c, gidx * 3
        else:
            zv = z1_3d[:, :, gidx * 3 * hw_c:(gidx + 1) * 3 * hw_c].reshape(b, ls, dil * 3 * hw_c)
            cpr, c0 = 3, 0
        og, lg = _band(zv, zv, bias_g, lead_grid=(b, dil), tq=tq, pb=pb, npv=npv, n_kvh=DIL_HEADS, r_n=1,
                       qmap=lambda i, r, j, cpr=cpr, c0=c0: (i, r * cpr + c0),
                       kmap=lambda i, r, j, cpr=cpr, c0=c0: (i, r * cpr + c0 + 1),
                       vmap=lambda i, r, j, cpr=cpr, c0=c0: (i, r * cpr + c0 + 2),
                       omap=lambda i, r, j: (i, r), lmap=lambda i, r, j: (i, r),
                       out_shape=jax.ShapeDtypeStruct((b, ls, dil * hw_c), F32),
                       lse_shape=jax.ShapeDtypeStruct((b, ls, dil * LANES), F32),
                       scale=HEAD_DIM ** -0.5, name=f"dil_attn_{gidx}")
        os_.append(og.reshape(m, hw_c))
        lses.append(lg.reshape(m, LANES))
    x2d = _out_c(os_, lses, c_w_out[0].astype(BF16), x2d, g_m, seq=s, tm=tm_mid)

    wr = jnp.concatenate([moe_w_router[0], jnp.zeros((d, LANES - N_EXPERTS), F32)], axis=1)
    h_moe, route = _moe_prep(x2d, geff_f, sh_f, wr, seq=s, tm=tm_mid)
    tm_e = 1024
    dest, slot_tok, tile_e, n_used = _moe_plan(route, tm=tm_e)
    yb = _moe_experts(h_moe, slot_tok, tile_e, n_used, moe_w_gate[0], moe_w_up[0], moe_w_down[0],
                      tm=tm_e, tf=256, issue_steps=16)
    tmc = _tile(s, 256)
    out = _moe_combine(yb, dest[:, 0].reshape(m // tmc, tmc), dest[:, 1].reshape(m // tmc, tmc), route,
                       x2d, g_f, final_norm_g.reshape(1, d), seq=s, tmc=tmc)
    return out.reshape(b, s, d)
```

```python
import functools
import math

import numpy as np
import jax
import jax.numpy as jnp
from jax import lax
from jax.experimental import pallas as pl
from jax.experimental.pallas import tpu as pltpu

F32 = jnp.float32
BF16 = jnp.bfloat16

D_MODEL = 2048
HEAD_DIM = 128
NEG_INF = -1e30
LOG2E = 1.4426950408889634
RMS_EPS = 1e-6
NUM_BUCKETS = 32
T5_MAX_DIST = 2048
NSA_HEADS = 8
NSA_KV_HEADS = 2
NSA_GROUP = NSA_HEADS // NSA_KV_HEADS
CMP_LEN = 32
CMP_STRIDE = 16
SLC_LEN = 64
N_SEL = 16
WIN = 512
FORCE_SCORE = 1e9
MLA_HEADS = 8
Q_LORA = 512
KV_LORA = 512
QK_NOPE = 128
QK_ROPE = 64
V_DIM = 128
ROPE_THETA = 10000.0
DIL_PATTERNS = ((128, 1), (512, 4), (2048, 16))
DIL_HEADS = 8
D_FF = 5632
N_EXPERTS = 8
MOE_TOP_K = 2
D_FF_EXPERT = 7168

LANES = 128
MXU_DIM = 256
VMEM_LIMIT_BYTES = 56 * 1024 * 1024
ROW_DMA_UNROLL = 8

Z0_Q = 0
Z0_KV = Z0_Q + NSA_HEADS * HEAD_DIM
Z0_KVLAT = Z0_KV + 6 * NSA_KV_HEADS * HEAD_DIM
Z0_MISC = Z0_KVLAT + KV_LORA
Z0_QLAT = Z0_MISC + LANES
Z0_W = Z0_QLAT + Q_LORA + LANES
LAT_BLOCK = KV_LORA + LANES
GATE_LANE0 = QK_ROPE


def _cparams(sem, vmem=VMEM_LIMIT_BYTES):
    return pltpu.CompilerParams(dimension_semantics=sem, vmem_limit_bytes=vmem)


def _dot(a, b):
    return jnp.dot(a, b, preferred_element_type=F32)


def _dot_nt(a, b):
    return lax.dot_general(a, b, (((1,), (1,)), ((), ())), preferred_element_type=F32)


def _sigmoid(x):
    return 1.0 / (1.0 + jnp.exp(-x))


def _t5_bucket(dist):
    n = jnp.maximum(dist, 0)
    max_exact = NUM_BUCKETS // 2
    nf = jnp.maximum(n, 1).astype(F32)
    large = max_exact + (jnp.log(nf / max_exact) / math.log(T5_MAX_DIST / max_exact)
                         * (NUM_BUCKETS - max_exact)).astype(jnp.int32)
    large = jnp.minimum(large, NUM_BUCKETS - 1)
    return jnp.where(n < max_exact, n, large)


def _ada_kernel(c_ref, w_ref, b_ref, o_ref):
    c = c_ref[...]
    cs = c * _sigmoid(c)
    o_ref[...] = _dot(cs.astype(BF16), w_ref[...].astype(BF16)) + b_ref[...]


def _ada_mod(c, ada_w, ada_b):
    depth, d, n = ada_w.shape
    b = c.shape[0]
    bp = 8
    cpad = jnp.zeros((bp, d), F32).at[:b].set(c)
    tn = 1024
    out = pl.pallas_call(
        _ada_kernel,
        out_shape=jax.ShapeDtypeStruct((depth, bp, n), F32),
        grid=(depth, n // tn),
        in_specs=[pl.BlockSpec((bp, d), lambda l, j: (0, 0)),
                  pl.BlockSpec((None, d, tn), lambda l, j: (l, 0, j)),
                  pl.BlockSpec((None, 1, tn), lambda l, j: (l, 0, j))],
        out_specs=pl.BlockSpec((None, bp, tn), lambda l, j: (l, 0, j)),
        compiler_params=_cparams(("parallel", "parallel")),
        name="ada_mod",
    )(cpad, ada_w, ada_b.reshape(depth, 1, n))
    return out[:, :b]


def _nmm_kernel(x_ref, g_ref, s_ref, w_ref, *rest, norm_cols, rope_tiles, tn):
    if rope_tiles:
        c_ref, s1_ref, s2_ref, o_ref, h_scr = rest
    else:
        o_ref, h_scr = rest
    j = pl.program_id(1)

    @pl.when(j == 0)
    def _():
        x = x_ref[...].astype(F32)
        xn = x[:, :norm_cols]
        ms = jnp.mean(xn * xn, axis=-1, keepdims=True)
        hn = xn * lax.rsqrt(ms + RMS_EPS) * g_ref[...] + s_ref[...]
        h_scr[:, :norm_cols] = hn.astype(BF16)
        if norm_cols < x.shape[1]:
            h_scr[:, norm_cols:] = x[:, norm_cols:].astype(BF16)

    acc = _dot(h_scr[...], w_ref[...])
    o_ref[...] = acc.astype(o_ref.dtype)
    if rope_tiles:
        @pl.when(j < rope_tiles)
        def _():
            for hh in range(tn // MXU_DIM):
                lo = hh * MXU_DIM + LANES
                y = acc[:, lo:lo + LANES]
                y2 = (y * c_ref[...] + pltpu.roll(y, LANES - 32, 1) * s1_ref[...]
                      + pltpu.roll(y, 32, 1) * s2_ref[...])
                o_ref[:, lo:lo + LANES] = y2.astype(o_ref.dtype)


def _nmm(x2d, geff, shift, w, *, seq, tm, tn, norm_cols, x_block, out_dtype, rope=None,
         rope_tiles=0, name):
    m = x2d.shape[0]
    k, n = w.shape
    tpb = seq // tm
    in_specs = [pl.BlockSpec((tm, k), lambda i, j: (i, x_block)),
                pl.BlockSpec((None, 1, norm_cols), lambda i, j: (i // tpb, 0, 0)),
                pl.BlockSpec((None, 1, norm_cols), lambda i, j: (i // tpb, 0, 0)),
                pl.BlockSpec((k, tn), lambda i, j: (0, j))]
    args = [x2d, geff, shift, w]
    if rope_tiles:
        for t in rope:
            in_specs.append(pl.BlockSpec((tm, LANES), lambda i, j: (i, 0)))
            args.append(t)
    return pl.pallas_call(
        functools.partial(_nmm_kernel, norm_cols=norm_cols, rope_tiles=rope_tiles, tn=tn),
        out_shape=jax.ShapeDtypeStruct((m, n), out_dtype),
        grid=(m // tm, n // tn),
        in_specs=in_specs,
        out_specs=pl.BlockSpec((tm, tn), lambda i, j: (i, j)),
        scratch_shapes=[pltpu.VMEM((tm, k), BF16)],
        compiler_params=_cparams(("parallel", "arbitrary")),
        name=name,
    )(*args)


def _cmp_kernel(x_ref, pos_ref, w1_ref, w2_ref, o_ref):
    half = w1_ref.shape[0] // 2
    x = x_ref[...]
    n16 = x.shape[0]
    a = _dot(x, w1_ref[:half, :])
    b = _dot(x, w1_ref[half:, :])
    c = _dot(pos_ref[...], w1_ref[...])
    hid = a + pltpu.roll(b, n16 - 1, 0) + c
    hid = jax.nn.gelu(hid, approximate=True)
    o_ref[...] = _dot(hid.astype(BF16), w2_ref[...]).astype(o_ref.dtype)


def _nsa_compress(x16, pos, w1, w2):
    b, nkv, n16, kk = x16.shape
    g = NSA_KV_HEADS
    return pl.pallas_call(
        _cmp_kernel,
        out_shape=jax.ShapeDtypeStruct((b, nkv, n16, HEAD_DIM), BF16),
        grid=(b, nkv),
        in_specs=[pl.BlockSpec((None, None, n16, kk), lambda i, j: (i, j, 0, 0)),
                  pl.BlockSpec((None, 1, 2 * kk), lambda i, j: (j // g, 0, 0)),
                  pl.BlockSpec((None, 2 * kk, HEAD_DIM), lambda i, j: (j // g, 0, 0)),
                  pl.BlockSpec((None, HEAD_DIM, HEAD_DIM), lambda i, j: (j // g, 0, 0))],
        out_specs=pl.BlockSpec((None, None, n16, HEAD_DIM), lambda i, j: (i, j, 0, 0)),
        compiler_params=_cparams(("parallel", "parallel")),
        name="nsa_compress",
    )(x16, pos, w1, w2)


def _cmpattn_kernel(q_ref, kv_ref, bias_ref, cmat_ref, misc_ref, o_ref, sel_ref, *, tq, n_slc,
                    n_top, scale):
    qi = pl.program_id(1)
    g_n, r_n = NSA_KV_HEADS, NSA_GROUP
    ncp = kv_ref.shape[1]
    trow = qi * tq + lax.broadcasted_iota(jnp.int32, (tq, 1), 0)
    has_c = trow >= (CMP_LEN - 1)
    blk_t = lax.shift_right_logical(trow, int(math.log2(SLC_LEN))).astype(F32)
    jb = lax.broadcasted_iota(jnp.int32, (tq, n_slc), 1).astype(F32)
    forced = (jb == 0.0) | (jb == blk_t) | (jb == blk_t - 1.0)
    valid = jb <= blk_t
    misc = misc_ref[...].astype(F32)
    for g in range(g_n):
        kc = kv_ref[g]
        vc = kv_ref[g_n + g]
        imp = jnp.zeros((tq, ncp), F32)
        for r in range(r_n):
            h = g * r_n + r
            q = q_ref[:, h * HEAD_DIM:(h + 1) * HEAD_DIM]
            s = _dot_nt(q, kc) * scale + bias_ref[h]
            m = jnp.max(s, axis=-1, keepdims=True)
            e = jnp.exp(s - m)
            p = e * (1.0 / jnp.sum(e, axis=-1, keepdims=True))
            p = jnp.where(has_c, p, 0.0)
            imp = imp + p
            o = _dot(p.astype(BF16), vc)
            c0 = GATE_LANE0 + h * 3
            gate = _sigmoid(misc[:, c0:c0 + 1])
            o_ref[:, h * HEAD_DIM:(h + 1) * HEAD_DIM] = gate * o
        imp_s = jnp.dot(imp, cmat_ref[...], precision=lax.Precision.HIGHEST,
                        preferred_element_type=F32)
        score = jnp.where(forced, FORCE_SCORE, jnp.where(valid, imp_s, -1.0))
        sel = jnp.zeros((tq, n_slc), F32)
        for _ in range(n_top):
            mx = jnp.max(score, axis=-1, keepdims=True)
            first = jnp.min(jnp.where(score == mx, jb, float(n_slc)), axis=-1, keepdims=True)
            hit = jb == first
            sel = jnp.where(hit, jnp.where(mx > -0.5, 1.0, 0.0), sel)
            score = jnp.where(hit, -3e38, score)
        sel_ref[g] = sel.astype(sel_ref.dtype)


def _nsa_cmp_attention(z0, kvc, bias_c, cmat, *, tq):
    b, s, _ = z0.shape
    ncp = kvc.shape[2]
    n_slc = s // SLC_LEN
    n_top = min(N_SEL, n_slc)
    qw = NSA_HEADS * HEAD_DIM
    return pl.pallas_call(
        functools.partial(_cmpattn_kernel, tq=tq, n_slc=n_slc, n_top=n_top, scale=HEAD_DIM ** -0.5),
        out_shape=(jax.ShapeDtypeStruct((b, s, qw), F32),
                   jax.ShapeDtypeStruct((b, NSA_KV_HEADS, s, n_slc), BF16)),
        grid=(b, s // tq),
        in_specs=[pl.BlockSpec((None, tq, qw), lambda i, j: (i, j, Z0_Q // qw)),
                  pl.BlockSpec((None, 2 * NSA_KV_HEADS, ncp, HEAD_DIM), lambda i, j: (i, 0, 0, 0)),
                  pl.BlockSpec((NSA_HEADS, tq, ncp), lambda i, j: (0, j, 0)),
                  pl.BlockSpec((ncp, n_slc), lambda i, j: (0, 0)),
                  pl.BlockSpec((None, tq, LANES), lambda i, j: (i, j, Z0_MISC // LANES))],
        out_specs=(pl.BlockSpec((None, tq, qw), lambda i, j: (i, j, 0)),
                   pl.BlockSpec((None, NSA_KV_HEADS, tq, n_slc), lambda i, j: (i, 0, j, 0))),
        compiler_params=_cparams(("parallel", "parallel")),
        name="nsa_cmp_attn",
    )(z0, kvc, bias_c, cmat, z0)


def _flash_kernel(qi_ref, ki_ref, q_ref, k_ref, v_ref, *rest, t, n_kvh, r_n, dk, scale, has_bias,
                  has_sel, gate_branch):
    rest = list(rest)
    bias_ref = rest.pop(0) if has_bias else None
    sel_ref = rest.pop(0) if has_sel else None
    if gate_branch is not None:
        misc_ref = rest.pop(0)
        prev_ref = rest.pop(0)
    o_ref, qs_scr, m_scr, acc_scr = rest
    pidx = pl.program_id(1)
    qi = qi_ref[pidx]
    ki = ki_ref[pidx]
    reps = t // LANES

    @pl.when(ki == 0)
    def _():
        qs_scr[...] = (q_ref[...].astype(F32) * (scale * LOG2E)).astype(BF16)
        m_scr[...] = jnp.full(m_scr.shape, NEG_INF, F32)
        acc_scr[...] = jnp.zeros(acc_scr.shape, F32)

    def step(diag):
        if has_sel:
            per = t // SLC_LEN
            erow = lax.broadcasted_iota(jnp.int32, (sel_ref.shape[2], t), 0)
            ecol = lax.broadcasted_iota(jnp.int32, (sel_ref.shape[2], t), 1)
            expand = jnp.where(erow == ki * per + lax.shift_right_logical(ecol, int(math.log2(SLC_LEN))),
                               1.0, 0.0).astype(BF16)
        if diag:
            causal = (lax.broadcasted_iota(jnp.int32, (t, t), 0)
                      >= lax.broadcasted_iota(jnp.int32, (t, t), 1))
        ones = jnp.ones((t, HEAD_DIM), BF16)
        for kh in range(n_kvh):
            k = k_ref[:, kh * dk:(kh + 1) * dk]
            v_ext = jnp.concatenate([v_ref[:, kh * HEAD_DIM:(kh + 1) * HEAD_DIM], ones], axis=1)
            madd = None
            if has_sel:
                madd = jnp.where(_dot(sel_ref[kh], expand) > 0.5, 0.0, NEG_INF)
                if diag:
                    madd = jnp.where(causal, madd, NEG_INF)
            elif diag:
                madd = jnp.where(causal, 0.0, NEG_INF)
            for r in range(r_n):
                h = kh * r_n + r
                s = _dot_nt(qs_scr[:, h * dk:(h + 1) * dk], k)
                if has_bias:
                    s = s + bias_ref[h]
                if madd is not None:
                    s = s + madd
                m_prev = m_scr[h]
                m_new = jnp.maximum(m_prev, jnp.max(s, axis=-1, keepdims=True))
                alpha = jnp.exp2(m_prev - m_new)
                p = jnp.exp2(s - jnp.concatenate([m_new] * reps, axis=1))
                acc_scr[h] = (jnp.concatenate([alpha, alpha], axis=1) * acc_scr[h]
                              + _dot(p.astype(BF16), v_ext))
                m_scr[h] = m_new

    @pl.when(ki < qi)
    def _():
        step(False)

    @pl.when(ki == qi)
    def _():
        step(True)
        if gate_branch is not None:
            misc = misc_ref[...].astype(F32)
        for h in range(n_kvh * r_n):
            a = acc_scr[h]
            o = a[:, :HEAD_DIM] * (1.0 / a[:, HEAD_DIM:])
            sl = slice(h * HEAD_DIM, (h + 1) * HEAD_DIM)
            if gate_branch is not None:
                c0 = GATE_LANE0 + h * 3 + gate_branch
                o = prev_ref[:, sl] + _sigmoid(misc[:, c0:c0 + 1]) * o
            o_ref[:, sl] = o.astype(o_ref.dtype)


def _tri_pairs(nq):
    qi = np.concatenate([np.full(i + 1, i, np.int32) for i in range(nq)])
    ki = np.concatenate([np.arange(i + 1, dtype=np.int32) for i in range(nq)])
    return jnp.asarray(qi), jnp.asarray(ki)


def _flash(q_arr, k_arr, v_arr, *, t, n_kvh, r_n, dk, q_blk, k_blk, v_blk, scale, out_dtype,
           bias=None, sel=None, misc=None, misc_blk=0, prev=None, gate_branch=None, name):
    b, s, _ = q_arr.shape
    nq = s // t
    qi_a, ki_a = _tri_pairs(nq)
    nh = n_kvh * r_n
    ow = nh * HEAD_DIM
    in_specs = [pl.BlockSpec((None, t, nh * dk), lambda i, p, qa, ka: (i, qa[p], q_blk)),
                pl.BlockSpec((None, t, n_kvh * dk), lambda i, p, qa, ka: (i, ka[p], k_blk)),
                pl.BlockSpec((None, t, n_kvh * HEAD_DIM), lambda i, p, qa, ka: (i, ka[p], v_blk))]
    args = [q_arr, k_arr, v_arr]
    if bias is not None:
        nd = bias.shape[1]
        in_specs.append(pl.BlockSpec((nh, None, t, t),
                                     lambda i, p, qa, ka: (0, jnp.minimum(qa[p] - ka[p], nd - 1), 0, 0)))
        args.append(bias)
    if sel is not None:
        n_slc = sel.shape[-1]
        in_specs.append(pl.BlockSpec((None, n_kvh, t, n_slc), lambda i, p, qa, ka: (i, 0, qa[p], 0)))
        args.append(sel)
    io_alias = {}
    if gate_branch is not None:
        in_specs.append(pl.BlockSpec((None, t, LANES), lambda i, p, qa, ka: (i, qa[p], misc_blk)))
        args.append(misc)
        in_specs.append(pl.BlockSpec((None, t, ow), lambda i, p, qa, ka: (i, qa[p], 0)))
        args.append(prev)
        io_alias = {2 + len(args) - 1: 0}
    return pl.pallas_call(
        functools.partial(_flash_kernel, t=t, n_kvh=n_kvh, r_n=r_n, dk=dk, scale=scale,
                          has_bias=bias is not None, has_sel=sel is not None, gate_branch=gate_branch),
        out_shape=jax.ShapeDtypeStruct((b, s, ow), out_dtype),
        grid_spec=pltpu.PrefetchScalarGridSpec(
            num_scalar_prefetch=2,
            grid=(b, int(qi_a.shape[0])),
            in_specs=in_specs,
            out_specs=pl.BlockSpec((None, t, ow), lambda i, p, qa, ka: (i, qa[p], 0)),
            scratch_shapes=[pltpu.VMEM((t, nh * dk), BF16), pltpu.VMEM((nh, t, LANES), F32),
                            pltpu.VMEM((nh, t, 2 * HEAD_DIM), F32)]),
        input_output_aliases=io_alias,
        compiler_params=_cparams(("parallel", "arbitrary")),
        name=name,
    )(qi_a, ki_a, *args)


def _band_kernel(q_ref, *rest, tq, pb, npv, n_kvh, r_n, scale, gate_branch, want_lse, qi_axis):
    rest = list(rest)
    kp = [rest.pop(0) for _ in range(npv)]
    kc = rest.pop(0)
    vp = [rest.pop(0) for _ in range(npv)]
    vc = rest.pop(0)
    bias_ref = rest.pop(0)
    if gate_branch is not None:
        misc_ref = rest.pop(0)
        prev_ref = rest.pop(0)
    o_ref = rest.pop(0)
    lse_ref = rest.pop(0) if want_lse else None
    qi = pl.program_id(qi_axis)
    nblk = tq // pb
    kw_prev = npv * pb
    if gate_branch is not None:
        misc = misc_ref[...].astype(F32)
    if want_lse:
        lane = lax.broadcasted_iota(jnp.int32, (tq, LANES), 1)
        lse_tile = jnp.zeros((tq, LANES), F32)
    ones_p = jnp.ones((pb, HEAD_DIM), BF16)
    ones_c = jnp.ones((tq, HEAD_DIM), BF16)
    for kh in range(n_kvh):
        ksl = slice(kh * HEAD_DIM, (kh + 1) * HEAD_DIM)
        v_ext = [jnp.concatenate([vp[n][:, ksl], ones_p], axis=1) for n in range(npv)]
        v_ext.append(jnp.concatenate([vc[:, ksl], ones_c], axis=1))
        for r in range(r_n):
            h = kh * r_n + r
            hsl = slice(h * HEAD_DIM, (h + 1) * HEAD_DIM)
            q = (q_ref[:, hsl].astype(F32) * (scale * LOG2E)).astype(BF16)
            parts = []
            for n in range(npv):
                pen = jnp.where(qi * nblk - npv + n >= 0, 0.0, NEG_INF)
                parts.append(_dot_nt(q, kp[n][:, ksl]) + bias_ref[h, :, n * pb:(n + 1) * pb] + pen)
            parts.append(_dot_nt(q, kc[:, ksl]) + bias_ref[h, :, kw_prev:])
            m = parts[0].max(axis=-1, keepdims=True)
            for sp in parts[1:]:
                m = jnp.maximum(m, sp.max(axis=-1, keepdims=True))
            o_ext = jnp.zeros((tq, 2 * HEAD_DIM), F32)
            for n, sp in enumerate(parts):
                o_ext = o_ext + _dot(jnp.exp2(sp - m).astype(BF16), v_ext[n])
            o = o_ext[:, :HEAD_DIM] * (1.0 / o_ext[:, HEAD_DIM:])
            if gate_branch is not None:
                c0 = GATE_LANE0 + h * 3 + gate_branch
                o = prev_ref[:, hsl] + _sigmoid(misc[:, c0:c0 + 1]) * o
            o_ref[:, hsl] = o.astype(o_ref.dtype)
            if want_lse:
                lse = m * (1.0 / LOG2E) + jnp.log(o_ext[:, HEAD_DIM:HEAD_DIM + 1])
                lse_tile = jnp.where(lane == h, lse, lse_tile)
    if want_lse:
        lse_ref[...] = lse_tile


def _band(q_arr, kv_arr, bias, *, lead_grid, tq, pb, npv, n_kvh, r_n, qmap, kmap, vmap, omap,
          lmap=None, out_shape, lse_shape=None, misc=None, miscmap=None, prev=None, gate_branch=None,
          scale, name):
    nl = len(lead_grid)
    nq = q_arr.shape[1] // tq
    nblk = tq // pb
    nh = n_kvh * r_n
    qw = nh * HEAD_DIM
    kw = n_kvh * HEAD_DIM

    def rows_cur(fn):
        def im(*g):
            bb, cc = fn(*g)
            return (bb, g[nl], cc)
        return im

    def rows_prev(fn, n):
        def im(*g):
            bb, cc = fn(*g)
            return (bb, jnp.maximum(g[nl] * nblk - npv + n, 0), cc)
        return im

    in_specs = [pl.BlockSpec((None, tq, qw), rows_cur(qmap))]
    args = [q_arr]
    for fn in (kmap, vmap):
        for n in range(npv):
            in_specs.append(pl.BlockSpec((None, pb, kw), rows_prev(fn, n)))
            args.append(kv_arr)
        in_specs.append(pl.BlockSpec((None, tq, kw), rows_cur(fn)))
        args.append(kv_arr)
    in_specs.append(pl.BlockSpec(bias.shape, lambda *g: (0, 0, 0)))
    args.append(bias)
    io_alias = {}
    if gate_branch is not None:
        in_specs.append(pl.BlockSpec((None, tq, LANES), rows_cur(miscmap)))
        args.append(misc)
        in_specs.append(pl.BlockSpec((None, tq, qw), rows_cur(omap)))
        args.append(prev)
        io_alias = {len(args) - 1: 0}
    out_shapes = [out_shape]
    out_specs = [pl.BlockSpec((None, tq, qw), rows_cur(omap))]
    if lse_shape is not None:
        out_shapes.append(lse_shape)
        out_specs.append(pl.BlockSpec((None, tq, LANES), rows_cur(lmap)))
    res = pl.pallas_call(
        functools.partial(_band_kernel, tq=tq, pb=pb, npv=npv, n_kvh=n_kvh, r_n=r_n, scale=scale,
                          gate_branch=gate_branch, want_lse=lse_shape is not None, qi_axis=nl),
        out_shape=tuple(out_shapes),
        grid=tuple(lead_grid) + (nq,),
        in_specs=in_specs,
        out_specs=tuple(out_specs),
        input_output_aliases=io_alias,
        compiler_params=_cparams(("parallel",) * (nl + 1)),
        name=name,
    )(*args)
    return res


def _dist_table(rel_bias, n_dist, dist_scale=1):
    tab = rel_bias[_t5_bucket(jnp.arange(n_dist) * dist_scale)].astype(F32)
    return jnp.concatenate([tab, jnp.full((1, tab.shape[1]), NEG_INF, F32)], axis=0)


def _toeplitz_kernel(w_ref, o_ref):
    t = o_ref.shape[0]
    x = jnp.broadcast_to(w_ref[...], (t, 2 * t))
    o_ref[...] = pltpu.roll(x, 0, 1, stride=1, stride_axis=0)[:, :t]


def _toeplitz_tiles(tab, d0s, t, lo, hi, mult=1):
    masked = tab.shape[0] - 1
    u = np.arange(2 * t)
    i_minus_j = np.where(u < t, -u, 2 * t - u)
    dist = np.asarray(d0s)[:, None] + mult * i_minus_j[None, :]
    idx = np.where((dist >= lo) & (dist <= hi) & (u != t)[None, :], dist, masked)
    w = tab[jnp.asarray(idx, jnp.int32)].transpose(2, 0, 1)
    nh, nc = w.shape[0], w.shape[1]
    tiles = pl.pallas_call(
        _toeplitz_kernel,
        out_shape=jax.ShapeDtypeStruct((nh * nc, t, t), F32),
        grid=(nh * nc,),
        in_specs=[pl.BlockSpec((None, 1, 2 * t), lambda i: (i, 0, 0))],
        out_specs=pl.BlockSpec((None, t, t), lambda i: (i, 0, 0)),
        compiler_params=_cparams(("parallel",)),
        name="toeplitz_tiles",
    )(w.reshape(nh * nc, 1, 2 * t))
    return tiles.reshape(nh, nc, t, t)


def _band_bias(rel_bias, tq, pb, npv, max_back, dist_scale):
    tab = _dist_table(rel_bias, max_back + 1, dist_scale) * LOG2E
    d0s = [(npv - n) * pb for n in range(npv)] + [0]
    tiles = _toeplitz_tiles(tab, d0s, tq, 0, max_back)
    parts = [tiles[:, n, :, :pb] for n in range(npv)] + [tiles[:, npv]]
    return jnp.concatenate(parts, axis=-1)


def _out_ab_kernel(oa_ref, ob_ref, wa_ref, wb_ref, x_ref, g_ref, o_ref):
    y = _dot(oa_ref[...].astype(BF16), wa_ref[...]) + _dot(ob_ref[...], wb_ref[...])
    o_ref[...] = x_ref[...] + g_ref[...] * y


def _out_ab(oa, ob, w, x2d, gate, *, seq, tm):
    m, d = x2d.shape
    ka = oa.shape[1]
    kb = ob.shape[1]
    tpb = seq // tm
    return pl.pallas_call(
        _out_ab_kernel,
        out_shape=jax.ShapeDtypeStruct((m, d), F32),
        grid=(m // tm,),
        in_specs=[pl.BlockSpec((tm, ka), lambda i: (i, 0)),
                  pl.BlockSpec((tm, kb), lambda i: (i, 0)),
                  pl.BlockSpec((ka, d), lambda i: (0, 0)),
                  pl.BlockSpec((kb, d), lambda i: (ka // kb, 0)),
                  pl.BlockSpec((tm, d), lambda i: (i, 0)),
                  pl.BlockSpec((None, 1, d), lambda i: (i // tpb, 0, 0))],
        out_specs=pl.BlockSpec((tm, d), lambda i: (i, 0)),
        compiler_params=_cparams(("parallel",)),
        name="out_proj_ab",
    )(oa, ob, w, w, x2d, gate)


def _out_c_kernel(o0_ref, o1_ref, o2_ref, l0_ref, l1_ref, l2_ref, w_ref, x_ref, g_ref, o_ref, mrg_scr):
    l0, l1, l2 = l0_ref[...], l1_ref[...], l2_ref[...]
    mx = jnp.maximum(jnp.maximum(l0, l1), l2)
    e0, e1, e2 = jnp.exp(l0 - mx), jnp.exp(l1 - mx), jnp.exp(l2 - mx)
    inv = 1.0 / (e0 + e1 + e2)
    w0, w1, w2 = e0 * inv, e1 * inv, e2 * inv
    for h in range(DIL_HEADS):
        sl = slice(h * HEAD_DIM, (h + 1) * HEAD_DIM)
        mg = (w0[:, h:h + 1] * o0_ref[:, sl] + w1[:, h:h + 1] * o1_ref[:, sl]
              + w2[:, h:h + 1] * o2_ref[:, sl])
        mrg_scr[:, sl] = mg.astype(BF16)
    o_ref[...] = x_ref[...] + g_ref[...] * _dot(mrg_scr[...], w_ref[...])


def _out_c(os_, lses, w, x2d, gate, *, seq, tm):
    m, d = x2d.shape
    kc = w.shape[0]
    tpb = seq // tm
    return pl.pallas_call(
        _out_c_kernel,
        out_shape=jax.ShapeDtypeStruct((m, d), F32),
        grid=(m // tm,),
        in_specs=[pl.BlockSpec((tm, kc), lambda i: (i, 0))] * 3
        + [pl.BlockSpec((tm, LANES), lambda i: (i, 0))] * 3
        + [pl.BlockSpec((kc, d), lambda i: (0, 0)),
           pl.BlockSpec((tm, d), lambda i: (i, 0)),
           pl.BlockSpec((None, 1, d), lambda i: (i // tpb, 0, 0))],
        out_specs=pl.BlockSpec((tm, d), lambda i: (i, 0)),
        scratch_shapes=[pltpu.VMEM((tm, kc), BF16)],
        compiler_params=_cparams(("parallel",)),
        name="out_proj_c",
    )(*os_, *lses, w, x2d, gate)


def _ffn_kernel(x_ref, g_ref, s_ref, wg_ref, wu_ref, wd_ref, gate_ref, o_ref, h_scr, acc_scr):
    f = pl.program_id(1)

    @pl.when(f == 0)
    def _():
        x = x_ref[...]
        ms = jnp.mean(x * x, axis=-1, keepdims=True)
        h_scr[...] = (x * lax.rsqrt(ms + RMS_EPS) * g_ref[...] + s_ref[...]).astype(BF16)
        acc_scr[...] = jnp.zeros(acc_scr.shape, F32)

    h = h_scr[...]
    a = _dot(h, wg_ref[...])
    b = _dot(h, wu_ref[...])
    hid = (a * _sigmoid(a) * b).astype(BF16)
    acc_scr[...] += _dot(hid, wd_ref[...])

    @pl.when(f == pl.num_programs(1) - 1)
    def _():
        o_ref[...] = x_ref[...] + gate_ref[...] * acc_scr[...]


def _ffn(x2d, geff, shift, wg, wu, wd, gate, *, seq, tm, tf):
    m, d = x2d.shape
    ff = wg.shape[1]
    tpb = seq // tm
    vec = pl.BlockSpec((None, 1, d), lambda i, f: (i // tpb, 0, 0))
    return pl.pallas_call(
        _ffn_kernel,
        out_shape=jax.ShapeDtypeStruct((m, d), F32),
        grid=(m // tm, ff // tf),
        in_specs=[pl.BlockSpec((tm, d), lambda i, f: (i, 0)), vec, vec,
                  pl.BlockSpec((d, tf), lambda i, f: (0, f)),
                  pl.BlockSpec((d, tf), lambda i, f: (0, f)),
                  pl.BlockSpec((tf, d), lambda i, f: (f, 0)),
                  vec],
        out_specs=pl.BlockSpec((tm, d), lambda i, f: (i, 0)),
        scratch_shapes=[pltpu.VMEM((tm, d), BF16), pltpu.VMEM((tm, d), F32)],
        compiler_params=_cparams(("parallel", "arbitrary")),
        name="ffn_swiglu",
    )(x2d, geff, shift, wg, wu, wd, gate)


def _moe_prep_kernel(x_ref, g_ref, s_ref, wr_ref, h_ref, route_ref):
    x = x_ref[...]
    ms = jnp.mean(x * x, axis=-1, keepdims=True)
    h = x * lax.rsqrt(ms + RMS_EPS) * g_ref[...] + s_ref[...]
    h_ref[...] = h
    logits = jnp.dot(h, wr_ref[...], precision=lax.Precision.HIGHEST, preferred_element_type=F32)
    lane = lax.broadcasted_iota(jnp.int32, logits.shape, 1).astype(F32)
    lg = jnp.where(lane < float(N_EXPERTS), logits, -3e38)
    m1 = jnp.max(lg, axis=-1, keepdims=True)
    i1 = jnp.min(jnp.where(lg == m1, lane, float(LANES)), axis=-1, keepdims=True)
    lg2 = jnp.where(lane == i1, -3e38, lg)
    m2 = jnp.max(lg2, axis=-1, keepdims=True)
    i2 = jnp.min(jnp.where(lg2 == m2, lane, float(LANES)), axis=-1, keepdims=True)
    e = jnp.exp(m2 - m1)
    inv = 1.0 / (1.0 + e)
    route = jnp.where(lane == 0.0, i1, jnp.where(lane == 1.0, i2, jnp.where(lane == 2.0, inv,
                      jnp.where(lane == 3.0, e * inv, 0.0))))
    route_ref[...] = route


def _moe_prep(x2d, geff, shift, w_router_pad, *, seq, tm):
    m, d = x2d.shape
    tpb = seq // tm
    vec = pl.BlockSpec((None, 1, d), lambda i: (i // tpb, 0, 0))
    return pl.pallas_call(
        _moe_prep_kernel,
        out_shape=(jax.ShapeDtypeStruct((m, d), F32), jax.ShapeDtypeStruct((m, LANES), F32)),
        grid=(m // tm,),
        in_specs=[pl.BlockSpec((tm, d), lambda i: (i, 0)), vec, vec,
                  pl.BlockSpec((d, LANES), lambda i: (0, 0))],
        out_specs=(pl.BlockSpec((tm, d), lambda i: (i, 0)), pl.BlockSpec((tm, LANES), lambda i: (i, 0))),
        compiler_params=_cparams(("parallel",)),
        name="moe_prep",
    )(x2d, geff, shift, w_router_pad)


def _row_copy(src_hbm, row, dst_vmem, slot, sem):
    return pltpu.make_async_copy(src_hbm.at[pl.ds(row, 1)], dst_vmem.at[pl.ds(slot, 1)], sem)


def _expert_kernel(te_ref, nu_ref, rows_ref, slot_ref, h_hbm, wg_ref, wu_ref, wd_ref, o_ref, xbuf, xs, sem, *,
                   tm, issue_steps):
    t = pl.program_id(0)
    f = pl.program_id(1)
    nf = pl.num_programs(1)
    n_used = nu_ref[0]
    live = t < n_used
    rows_per_step = tm // issue_steps

    @pl.when((t == 0) & (f == 0))
    def _():
        def start(r, c):
            _row_copy(h_hbm, slot_ref[0, r], xbuf, r, sem).start()
            return c
        lax.fori_loop(0, tm, start, 0, unroll=ROW_DMA_UNROLL)

    @pl.when(live & (f == 0))
    def _():
        def wait(r, c):
            _row_copy(h_hbm, 0, xbuf, r, sem).wait()
            return c
        lax.fori_loop(0, tm, wait, 0, unroll=ROW_DMA_UNROLL)
        xs[...] = xbuf[...].astype(BF16)
        o_ref[...] = jnp.zeros(o_ref.shape, o_ref.dtype)

    def compute(mr):
        x = xs[:mr, :]
        a = _dot(x, wg_ref[...].astype(BF16))
        b = _dot(x, wu_ref[...].astype(BF16))
        hid = (a * _sigmoid(a) * b).astype(BF16)
        o_ref[:mr, :] += _dot(hid, wd_ref[...].astype(BF16))

    prefetch = live & (t + 1 < n_used) & (f >= 1) & (f <= issue_steps)
    half = rows_ref[t] <= tm // 2

    for want_prefetch in (True, False):
        for want_half in (True, False):
            cond = live & (prefetch if want_prefetch else jnp.logical_not(prefetch))
            cond = cond & (half if want_half else jnp.logical_not(half))

            @pl.when(cond)
            def _(want_prefetch=want_prefetch, want_half=want_half):
                if want_prefetch:
                    base = (f - 1) * rows_per_step
                    for j in range(rows_per_step):
                        _row_copy(h_hbm, slot_ref[t + 1, base + j], xbuf, base + j, sem).start()
                compute(tm // 2 if want_half else tm)

    @pl.when(jnp.logical_not(live) & (f == nf - 1))
    def _():
        o_ref[...] = jnp.zeros(o_ref.shape, o_ref.dtype)


def _moe_experts(h2d, slot_tok, tile_e, n_used, tile_rows, wg, wu, wd, *, tm, tf, issue_steps):
    n_tiles = slot_tok.shape[0]
    d = h2d.shape[1]
    ff = wg.shape[2]
    nf = ff // tf
    assert tm % issue_steps == 0 and issue_steps < nf

    def f_blk(t, f, nu):
        return jnp.where(t < nu[0], f, nf - 1)

    return pl.pallas_call(
        functools.partial(_expert_kernel, tm=tm, issue_steps=issue_steps),
        out_shape=jax.ShapeDtypeStruct((n_tiles * tm, d), F32),
        grid_spec=pltpu.PrefetchScalarGridSpec(
            num_scalar_prefetch=4,
            grid=(n_tiles, nf),
            in_specs=[pl.BlockSpec(memory_space=pl.ANY),
                      pl.BlockSpec((None, d, tf), lambda t, f, te, nu, rw, sl: (te[t], 0, f_blk(t, f, nu))),
                      pl.BlockSpec((None, d, tf), lambda t, f, te, nu, rw, sl: (te[t], 0, f_blk(t, f, nu))),
                      pl.BlockSpec((None, tf, d), lambda t, f, te, nu, rw, sl: (te[t], f_blk(t, f, nu), 0))],
            out_specs=pl.BlockSpec((tm, d), lambda t, f, te, nu, rw, sl: (t, 0)),
            scratch_shapes=[pltpu.VMEM((tm, d), F32), pltpu.VMEM((tm, d), BF16),
                            pltpu.SemaphoreType.DMA(())]),
        compiler_params=_cparams(("arbitrary", "arbitrary")),
        name="moe_experts",
    )(tile_e, n_used, tile_rows, slot_tok, h2d, wg, wu, wd)


def _combine_kernel(d1_ref, d2_ref, yb_hbm, route_ref, x_ref, gate_ref, fg_ref, o_ref, b1, b2, sem, *, tmc):
    t = pl.program_id(0)

    def start(r, c):
        _row_copy(yb_hbm, d1_ref[t, r], b1, r, sem).start()
        _row_copy(yb_hbm, d2_ref[t, r], b2, r, sem).start()
        return c

    def wait(r, c):
        _row_copy(yb_hbm, 0, b1, r, sem).wait()
        _row_copy(yb_hbm, 0, b2, r, sem).wait()
        return c

    lax.fori_loop(0, tmc, start, 0, unroll=ROW_DMA_UNROLL)
    lax.fori_loop(0, tmc, wait, 0, unroll=ROW_DMA_UNROLL)
    route = route_ref[...]
    y = route[:, 2:3] * b1[...] + route[:, 3:4] * b2[...]
    xo = x_ref[...] + gate_ref[...] * y
    ms = jnp.mean(xo * xo, axis=-1, keepdims=True)
    o_ref[...] = xo * lax.rsqrt(ms + RMS_EPS) * fg_ref[...]


def _moe_combine(yb, dest1, dest2, route, x2d, gate, final_g, *, seq, tmc):
    m, d = x2d.shape
    tpb = seq // tmc
    return pl.pallas_call(
        functools.partial(_combine_kernel, tmc=tmc),
        out_shape=jax.ShapeDtypeStruct((m, d), F32),
        grid_spec=pltpu.PrefetchScalarGridSpec(
            num_scalar_prefetch=2,
            grid=(m // tmc,),
            in_specs=[pl.BlockSpec(memory_space=pl.ANY),
                      pl.BlockSpec((tmc, LANES), lambda t, a, b: (t, 0)),
                      pl.BlockSpec((tmc, d), lambda t, a, b: (t, 0)),
                      pl.BlockSpec((None, 1, d), lambda t, a, b: (t // tpb, 0, 0)),
                      pl.BlockSpec((1, d), lambda t, a, b: (0, 0))],
            out_specs=pl.BlockSpec((tmc, d), lambda t, a, b: (t, 0)),
            scratch_shapes=[pltpu.VMEM((tmc, d), F32), pltpu.VMEM((tmc, d), F32),
                            pltpu.SemaphoreType.DMA(())]),
        compiler_params=_cparams(("arbitrary",)),
        name="moe_combine",
    )(dest1, dest2, yb, route, x2d, gate, final_g)


def _moe_plan(route, *, tm):
    n = route.shape[0]
    a = n * MOE_TOP_K
    flat_e = route[:, :MOE_TOP_K].astype(jnp.int32).reshape(a)
    onehot = (flat_e[:, None] == jnp.arange(N_EXPERTS, dtype=jnp.int32)[None, :]).astype(jnp.int32)
    csum = jnp.cumsum(onehot, axis=0)
    pos = jnp.sum(onehot * (csum - 1), axis=1)
    counts = csum[-1]
    pcounts = (counts + tm - 1) // tm * tm
    pends = jnp.cumsum(pcounts)
    pstarts = pends - pcounts
    dest = (pstarts[flat_e] + pos).astype(jnp.int32)
    n_tiles = a // tm + N_EXPERTS
    tok = jnp.arange(a, dtype=jnp.int32) // MOE_TOP_K
    slot_tok = jnp.zeros((n_tiles * tm,), jnp.int32).at[dest].set(tok)
    n_used = (pends[-1] // tm).astype(jnp.int32)
    tile_e = jnp.minimum(jnp.searchsorted(pends, jnp.arange(n_tiles, dtype=jnp.int32) * tm, side='right'),
                         N_EXPERTS - 1).astype(jnp.int32)
    tile_e = jnp.where(jnp.arange(n_tiles) < n_used, tile_e, tile_e[jnp.maximum(n_used - 1, 0)])
    tile_rows = jnp.clip(counts[tile_e] - (jnp.arange(n_tiles, dtype=jnp.int32) * tm - pstarts[tile_e]), 0, tm)
    tile_rows = jnp.where(jnp.arange(n_tiles) < n_used, tile_rows, 0).astype(jnp.int32)
    return dest.reshape(n, MOE_TOP_K), slot_tok.reshape(n_tiles, tm), tile_e, n_used.reshape(1), tile_rows


def _tile(n, pref):
    t = min(n, pref)
    assert n % t == 0, (n, pref)
    return t


def kernel(x, c, positions, rel_bias, ada_w, ada_b, mix_norm_g, ffn_norm_g, ab_w_in, ab_w_out, nsa_cmp_pos_k, nsa_cmp_w1_k, nsa_cmp_w2_k, nsa_cmp_pos_v, nsa_cmp_w1_v, nsa_cmp_w2_v, mla_q_norm_g, mla_kv_norm_g, mla_w_uq, mla_w_ukv, ffn_w_gate, ffn_w_up, ffn_w_down, c_w_in, c_w_out, moe_w_router, moe_w_gate, moe_w_up, moe_w_down, final_norm_g):
    b, s, d = x.shape
    assert ada_w.shape[0] == 2 and d == D_MODEL and s % 256 == 0
    m = b * s
    x2d = x.reshape(m, d)
    tm_big = _tile(s, 1024)
    tm_mid = _tile(s, 512)
    t_att = _tile(s, 256)
    t_flash = _tile(s, 512)

    mod = _ada_mod(c, ada_w, ada_b)
    mods = mod.reshape(2, b, 6, 1, d)

    def layer_mod(i):
        sh_m, sc_m, g_m, sh_f, sc_f, g_f = (mods[i, :, j] for j in range(6))
        return (mix_norm_g[i][None, None, :] * (1.0 + sc_m), sh_m, g_m,
                ffn_norm_g[i][None, None, :] * (1.0 + sc_f), sh_f, g_f)

    geff_m, sh_m, g_m, geff_f, sh_f, g_f = layer_mod(0)
    w0 = ab_w_in[0]
    c_q, c_kv, c_g, c_ql, c_kvl = np.cumsum([NSA_HEADS * HEAD_DIM, 6 * NSA_KV_HEADS * HEAD_DIM,
                                             3 * NSA_HEADS, Q_LORA, KV_LORA]).tolist()
    zpad = lambda n: jnp.zeros((d, n), w0.dtype)
    w_in0 = jnp.concatenate([w0[:, :c_kv], w0[:, c_ql:c_kvl], w0[:, c_kvl:], w0[:, c_kv:c_g],
                             zpad(LANES - QK_ROPE - 3 * NSA_HEADS), w0[:, c_g:c_ql], zpad(LANES)],
                            axis=1).astype(BF16)
    assert w_in0.shape[1] == Z0_W
    z0 = _nmm(x2d, geff_m, sh_m, w_in0, seq=s, tm=tm_big, tn=Z0_W // 3, norm_cols=d, x_block=0,
              out_dtype=BF16, name="proj_in_ab")
    z0_3d = z0.reshape(b, s, Z0_W)

    inv_freq = ROPE_THETA ** (-jnp.arange(0, QK_ROPE, 2, dtype=F32) / QK_ROPE)
    ang = positions.astype(F32)[..., None] * inv_freq
    cos, sin = jnp.cos(ang).reshape(m, -1), jnp.sin(ang).reshape(m, -1)
    hr = QK_ROPE // 2
    zr = lambda n: jnp.zeros((m, n), F32)
    rope_tabs = (jnp.concatenate([cos, cos, zr(LANES - 2 * hr)], axis=1),
                 jnp.concatenate([-sin, zr(LANES - hr)], axis=1),
                 jnp.concatenate([zr(hr), sin, zr(LANES - 2 * hr)], axis=1))
    hw = MXU_DIM
    wq3 = mla_w_uq[0].reshape(Q_LORA, MLA_HEADS, QK_NOPE + QK_ROPE)
    wq = jnp.concatenate([wq3, jnp.zeros((Q_LORA, MLA_HEADS, hw - QK_NOPE - QK_ROPE), F32)], axis=2)
    wq = jnp.concatenate([wq.reshape(Q_LORA, MLA_HEADS * hw), jnp.zeros((LANES, MLA_HEADS * hw), F32)],
                         axis=0).astype(BF16)
    wkv3 = mla_w_ukv[0].reshape(KV_LORA, MLA_HEADS, QK_NOPE + V_DIM)
    wk_top = jnp.concatenate([wkv3[:, :, :QK_NOPE], jnp.zeros((KV_LORA, MLA_HEADS, hw - QK_NOPE), F32)],
                             axis=2).reshape(KV_LORA, MLA_HEADS * hw)
    pe_pass = jnp.zeros((LANES, hw), F32).at[jnp.arange(QK_ROPE), QK_NOPE + jnp.arange(QK_ROPE)].set(1.0)
    wk_bot = jnp.tile(pe_pass, (1, MLA_HEADS))
    wv = jnp.concatenate([wkv3[:, :, QK_NOPE:].reshape(KV_LORA, MLA_HEADS * V_DIM),
                          jnp.zeros((LANES, MLA_HEADS * V_DIM), F32)], axis=0)
    wkv = jnp.concatenate([jnp.concatenate([wk_top, wk_bot], axis=0), wv], axis=1).astype(BF16)
    ones_b = lambda g: jnp.broadcast_to(g[None, None, :], (b, 1, g.shape[0]))
    zeros_lat = jnp.zeros((b, 1, Q_LORA), F32)
    q_mla = _nmm(z0, ones_b(mla_q_norm_g[0]), zeros_lat, wq, seq=s, tm=tm_big, tn=1024, norm_cols=Q_LORA,
                 x_block=Z0_QLAT // LAT_BLOCK, out_dtype=BF16, rope=rope_tabs, rope_tiles=2,
                 name="mla_q_up")
    kv_mla = _nmm(z0, ones_b(mla_kv_norm_g[0]), zeros_lat, wkv, seq=s, tm=tm_big, tn=1024,
                  norm_cols=KV_LORA, x_block=Z0_KVLAT // LAT_BLOCK, out_dtype=BF16, rope=rope_tabs,
                  rope_tiles=2, name="mla_kv_up")
    o_mla = _flash(q_mla.reshape(b, s, -1), kv_mla.reshape(b, s, -1), kv_mla.reshape(b, s, -1),
                   t=t_flash, n_kvh=MLA_HEADS, r_n=1, dk=hw, q_blk=0, k_blk=0,
                   v_blk=(MLA_HEADS * hw) // (MLA_HEADS * V_DIM), scale=(QK_NOPE + QK_ROPE) ** -0.5,
                   out_dtype=BF16, name="mla_attn")

    n16 = s // CMP_STRIDE
    cmp_cols = z0_3d[:, :, Z0_KV:Z0_KV + 2 * NSA_KV_HEADS * HEAD_DIM]
    x16 = cmp_cols.reshape(b, n16, CMP_STRIDE, 2 * NSA_KV_HEADS, HEAD_DIM).transpose(0, 3, 1, 2, 4)
    x16 = x16.reshape(b, 2 * NSA_KV_HEADS, n16, CMP_STRIDE * HEAD_DIM)
    pos_kv = jnp.stack([nsa_cmp_pos_k[0], nsa_cmp_pos_v[0]]).reshape(2, 1, CMP_LEN * HEAD_DIM).astype(BF16)
    w1_kv = jnp.stack([nsa_cmp_w1_k[0], nsa_cmp_w1_v[0]]).astype(BF16)
    w2_kv = jnp.stack([nsa_cmp_w2_k[0], nsa_cmp_w2_v[0]]).astype(BF16)
    kvc = _nsa_compress(x16, pos_kv, w1_kv, w2_kv)

    n_cmp = (s - CMP_LEN) // CMP_STRIDE + 1
    n_slc = s // SLC_LEN
    ratio, span = SLC_LEN // CMP_STRIDE, CMP_LEN // CMP_STRIDE
    cm = np.zeros((n16, n_slc), np.float32)
    for j in range(n_slc):
        for mm in range(ratio):
            for nn in range(span):
                i = ratio * j + mm - nn
                if 0 <= i < n_cmp:
                    cm[i, j] += 1.0
    tab_s = _dist_table(rel_bias, s)
    bias_c = _toeplitz_tiles(tab_s, [r - (CMP_LEN - 1) for r in range(CMP_STRIDE)], n16, 0, s - 1,
                             mult=CMP_STRIDE)
    bias_c = bias_c.transpose(0, 2, 1, 3).reshape(NSA_HEADS, s, n16)
    o_nsa, sel = _nsa_cmp_attention(z0_3d, kvc, bias_c, jnp.asarray(cm), tq=t_att)

    nq = s // t_flash
    nd = min(nq, -(-(T5_MAX_DIST + t_flash - 1) // t_flash) + 1)
    bias_d = _toeplitz_tiles(_dist_table(rel_bias, nd * t_flash) * LOG2E, [dd * t_flash for dd in range(nd)],
                             t_flash, 0, nd * t_flash - 1)
    kvw = NSA_KV_HEADS * HEAD_DIM
    o_nsa = _flash(z0_3d, z0_3d, z0_3d, t=t_flash, n_kvh=NSA_KV_HEADS, r_n=NSA_GROUP, dk=HEAD_DIM,
                   q_blk=0, k_blk=(Z0_KV + 2 * kvw) // kvw, v_blk=(Z0_KV + 3 * kvw) // kvw,
                   scale=HEAD_DIM ** -0.5, out_dtype=F32, bias=bias_d, sel=sel, misc=z0_3d,
                   misc_blk=Z0_MISC // LANES, prev=o_nsa, gate_branch=1, name="nsa_slc_attn")

    npv_w = -(-(WIN - 1) // t_att)
    bias_w = _band_bias(rel_bias, t_att, t_att, npv_w, WIN - 1, 1)
    (o_nsa,) = _band(z0_3d, z0_3d, bias_w, lead_grid=(b,), tq=t_att, pb=t_att, npv=npv_w,
                     n_kvh=NSA_KV_HEADS, r_n=NSA_GROUP,
                     qmap=lambda i, j: (i, 0), kmap=lambda i, j: (i, (Z0_KV + 4 * kvw) // kvw),
                     vmap=lambda i, j: (i, (Z0_KV + 5 * kvw) // kvw), omap=lambda i, j: (i, 0),
                     out_shape=jax.ShapeDtypeStruct((b, s, NSA_HEADS * HEAD_DIM), F32),
                     misc=z0_3d, miscmap=lambda i, j: (i, Z0_MISC // LANES), prev=o_nsa, gate_branch=2,
                     scale=HEAD_DIM ** -0.5, name="nsa_win_attn")

    x2d = _out_ab(o_nsa.reshape(m, -1), o_mla.reshape(m, -1), ab_w_out[0].astype(BF16), x2d, g_m,
                  seq=s, tm=tm_mid)

    x2d = _ffn(x2d, geff_f, sh_f, ffn_w_gate[0].astype(BF16), ffn_w_up[0].astype(BF16),
               ffn_w_down[0].astype(BF16), g_f, seq=s, tm=tm_mid, tf=512)

    geff_m, sh_m, g_m, geff_f, sh_f, g_f = layer_mod(1)
    cw = c_w_in.shape[2]
    z1 = _nmm(x2d, geff_m, sh_m, c_w_in[0].astype(BF16), seq=s, tm=tm_big, tn=1024, norm_cols=d,
              x_block=0, out_dtype=BF16, name="proj_in_c")
    hw_c = DIL_HEADS * HEAD_DIM
    z1_3d = z1.reshape(b, s, cw)
    os_, lses = [], []
    for gidx, (win, dil) in enumerate(DIL_PATTERNS):
        ls = s // dil
        tq = _tile(ls, 256)
        pb = min(tq, 128)
        max_back = win // dil
        npv = -(-max_back // pb)
        bias_g = _band_bias(rel_bias, tq, pb, npv, max_back, dil)
        if dil == 1:
            zv, cpr, c0 = z1_3d, cw // hw_c, gidx * 3
        else:
            zv = z1_3d[:, :, gidx * 3 * hw_c:(gidx + 1) * 3 * hw_c].reshape(b, ls, dil * 3 * hw_c)
            cpr, c0 = 3, 0
        og, lg = _band(zv, zv, bias_g, lead_grid=(b, dil), tq=tq, pb=pb, npv=npv, n_kvh=DIL_HEADS, r_n=1,
                       qmap=lambda i, r, j, cpr=cpr, c0=c0: (i, r * cpr + c0),
                       kmap=lambda i, r, j, cpr=cpr, c0=c0: (i, r * cpr + c0 + 1),
                       vmap=lambda i, r, j, cpr=cpr, c0=c0: (i, r * cpr + c0 + 2),
                       omap=lambda i, r, j: (i, r), lmap=lambda i, r, j: (i, r),
                       out_shape=jax.ShapeDtypeStruct((b, ls, dil * hw_c), F32),
                       lse_shape=jax.ShapeDtypeStruct((b, ls, dil * LANES), F32),
                       scale=HEAD_DIM ** -0.5, name=f"dil_attn_{gidx}")
        os_.append(og.reshape(m, hw_c))
        lses.append(lg.reshape(m, LANES))
    x2d = _out_c(os_, lses, c_w_out[0].astype(BF16), x2d, g_m, seq=s, tm=tm_mid)

    wr = jnp.concatenate([moe_w_router[0], jnp.zeros((d, LANES - N_EXPERTS), F32)], axis=1)
    h_moe, route = _moe_prep(x2d, geff_f, sh_f, wr, seq=s, tm=tm_mid)
    tm_e = 1024
    dest, slot_tok, tile_e, n_used, tile_rows = _moe_plan(route, tm=tm_e)
    yb = _moe_experts(h_moe, slot_tok, tile_e, n_used, tile_rows, moe_w_gate[0], moe_w_up[0], moe_w_down[0],
                      tm=tm_e, tf=256, issue_steps=16)
    tmc = _tile(s, 256)
    out = _moe_combine(yb, dest[:, 0].reshape(m // tmc, tmc), dest[:, 1].reshape(m // tmc, tmc), route,
                       x2d, g_f, final_norm_g.reshape(1, d), seq=s, tmc=tmc)
    return out.reshape(b, s, d)
```

```python
import functools
import math

import numpy as np
import jax
import jax.numpy as jnp
from jax import lax
from jax.experimental import pallas as pl
from jax.experimental.pallas import tpu as pltpu

F32 = jnp.float32
BF16 = jnp.bfloat16

D_MODEL = 2048
HEAD_DIM = 128
NEG_INF = -1e30
LOG2E = 1.4426950408889634
RMS_EPS = 1e-6
NUM_BUCKETS = 32
T5_MAX_DIST = 2048
NSA_HEADS = 8
NSA_KV_HEADS = 2
NSA_GROUP = NSA_HEADS // NSA_KV_HEADS
CMP_LEN = 32
CMP_STRIDE = 16
SLC_LEN = 64
N_SEL = 16
WIN = 512
FORCE_SCORE = 1e9
MLA_HEADS = 8
Q_LORA = 512
KV_LORA = 512
QK_NOPE = 128
QK_ROPE = 64
V_DIM = 128
ROPE_THETA = 10000.0
DIL_PATTERNS = ((128, 1), (512, 4), (2048, 16))
DIL_HEADS = 8
D_FF = 5632
N_EXPERTS = 8
MOE_TOP_K = 2
D_FF_EXPERT = 7168

LANES = 128
MXU_DIM = 256
VMEM_LIMIT_BYTES = 56 * 1024 * 1024
ROW_DMA_UNROLL = 8
PERM_MIN_CHUNKS = 4

Z0_Q = 0
Z0_KV = Z0_Q + NSA_HEADS * HEAD_DIM
Z0_KVLAT = Z0_KV + 6 * NSA_KV_HEADS * HEAD_DIM
Z0_MISC = Z0_KVLAT + KV_LORA
Z0_QLAT = Z0_MISC + LANES
Z0_W = Z0_QLAT + Q_LORA + LANES
LAT_BLOCK = KV_LORA + LANES
GATE_LANE0 = QK_ROPE


def _cparams(sem, vmem=VMEM_LIMIT_BYTES):
    return pltpu.CompilerParams(dimension_semantics=sem, vmem_limit_bytes=vmem)


def _dot(a, b):
    return jnp.dot(a, b, preferred_element_type=F32)


def _dot_nt(a, b):
    return lax.dot_general(a, b, (((1,), (1,)), ((), ())), preferred_element_type=F32)


def _sigmoid(x):
    return 1.0 / (1.0 + jnp.exp(-x))


def _t5_bucket(dist):
    n = jnp.maximum(dist, 0)
    max_exact = NUM_BUCKETS // 2
    nf = jnp.maximum(n, 1).astype(F32)
    large = max_exact + (jnp.log(nf / max_exact) / math.log(T5_MAX_DIST / max_exact)
                         * (NUM_BUCKETS - max_exact)).astype(jnp.int32)
    large = jnp.minimum(large, NUM_BUCKETS - 1)
    return jnp.where(n < max_exact, n, large)


def _ada_kernel(c_ref, w_ref, b_ref, o_ref):
    c = c_ref[...]
    cs = c * _sigmoid(c)
    o_ref[...] = _dot(cs.astype(BF16), w_ref[...].astype(BF16)) + b_ref[...]


def _ada_mod(c, ada_w, ada_b):
    depth, d, n = ada_w.shape
    b = c.shape[0]
    bp = 8
    cpad = jnp.zeros((bp, d), F32).at[:b].set(c)
    tn = 1024
    out = pl.pallas_call(
        _ada_kernel,
        out_shape=jax.ShapeDtypeStruct((depth, bp, n), F32),
        grid=(depth, n // tn),
        in_specs=[pl.BlockSpec((bp, d), lambda l, j: (0, 0)),
                  pl.BlockSpec((None, d, tn), lambda l, j: (l, 0, j)),
                  pl.BlockSpec((None, 1, tn), lambda l, j: (l, 0, j))],
        out_specs=pl.BlockSpec((None, bp, tn), lambda l, j: (l, 0, j)),
        compiler_params=_cparams(("parallel", "parallel")),
        name="ada_mod",
    )(cpad, ada_w, ada_b.reshape(depth, 1, n))
    return out[:, :b]


def _nmm_kernel(x_ref, g_ref, s_ref, w_ref, *rest, norm_cols, rope_tiles, tn):
    if rope_tiles:
        c_ref, s1_ref, s2_ref, o_ref, h_scr = rest
    else:
        o_ref, h_scr = rest
    j = pl.program_id(1)

    @pl.when(j == 0)
    def _():
        x = x_ref[...].astype(F32)
        xn = x[:, :norm_cols]
        ms = jnp.mean(xn * xn, axis=-1, keepdims=True)
        hn = xn * lax.rsqrt(ms + RMS_EPS) * g_ref[...] + s_ref[...]
        h_scr[:, :norm_cols] = hn.astype(BF16)
        if norm_cols < x.shape[1]:
            h_scr[:, norm_cols:] = x[:, norm_cols:].astype(BF16)

    acc = _dot(h_scr[...], w_ref[...])
    o_ref[...] = acc.astype(o_ref.dtype)
    if rope_tiles:
        @pl.when(j < rope_tiles)
        def _():
            for hh in range(tn // MXU_DIM):
                lo = hh * MXU_DIM + LANES
                y = acc[:, lo:lo + LANES]
                y2 = (y * c_ref[...] + pltpu.roll(y, LANES - 32, 1) * s1_ref[...]
                      + pltpu.roll(y, 32, 1) * s2_ref[...])
                o_ref[:, lo:lo + LANES] = y2.astype(o_ref.dtype)


def _nmm(x2d, geff, shift, w, *, seq, tm, tn, norm_cols, x_block, out_dtype, rope=None,
         rope_tiles=0, name):
    m = x2d.shape[0]
    k, n = w.shape
    tpb = seq // tm
    in_specs = [pl.BlockSpec((tm, k), lambda i, j: (i, x_block)),
                pl.BlockSpec((None, 1, norm_cols), lambda i, j: (i // tpb, 0, 0)),
                pl.BlockSpec((None, 1, norm_cols), lambda i, j: (i // tpb, 0, 0)),
                pl.BlockSpec((k, tn), lambda i, j: (0, j))]
    args = [x2d, geff, shift, w]
    if rope_tiles:
        for t in rope:
            in_specs.append(pl.BlockSpec((tm, LANES), lambda i, j: (i, 0)))
            args.append(t)
    return pl.pallas_call(
        functools.partial(_nmm_kernel, norm_cols=norm_cols, rope_tiles=rope_tiles, tn=tn),
        out_shape=jax.ShapeDtypeStruct((m, n), out_dtype),
        grid=(m // tm, n // tn),
        in_specs=in_specs,
        out_specs=pl.BlockSpec((tm, tn), lambda i, j: (i, j)),
        scratch_shapes=[pltpu.VMEM((tm, k), BF16)],
        compiler_params=_cparams(("parallel", "arbitrary")),
        name=name,
    )(*args)


def _nmm_perm_kernel(x_ref, g_ref, s_ref, w_ref, o_ref, h_scr, xcol_scr, *, dils, tiles_per_group):
    j = pl.program_id(1)
    tm, k = x_ref.shape
    ncb = k // LANES

    @pl.when(j == 0)
    def _():
        for cb in range(ncb):
            xcol_scr[cb] = x_ref[:, cb * LANES:(cb + 1) * LANES]
        for gi, dil in enumerate(dils):
            chunks = max(dil, PERM_MIN_CHUNKS)
            n = tm // chunks
            for r in range(chunks):
                if dil > 1:
                    x = jnp.concatenate([xcol_scr[cb, pl.ds(r, n, stride=dil), :] for cb in range(ncb)],
                                        axis=1)
                else:
                    x = x_ref[r * n:(r + 1) * n, :]
                ms = jnp.mean(x * x, axis=-1, keepdims=True)
                h_scr[gi, r * n:(r + 1) * n, :] = (x * lax.rsqrt(ms + RMS_EPS) * g_ref[...]
                                                   + s_ref[...]).astype(BF16)

    o_ref[...] = _dot(h_scr[j // tiles_per_group], w_ref[...]).astype(o_ref.dtype)


def _nmm_perm(x2d, geff, shift, w, *, seq, tm, tn, dils, out_dtype, name):
    m, k = x2d.shape
    n = w.shape[1]
    tpb = seq // tm
    tiles_per_group = n // len(dils) // tn
    return pl.pallas_call(
        functools.partial(_nmm_perm_kernel, dils=dils, tiles_per_group=tiles_per_group),
        out_shape=jax.ShapeDtypeStruct((m, n), out_dtype),
        grid=(m // tm, n // tn),
        in_specs=[pl.BlockSpec((tm, k), lambda i, j: (i, 0), pipeline_mode=pl.Buffered(1)),
                  pl.BlockSpec((None, 1, k), lambda i, j: (i // tpb, 0, 0)),
                  pl.BlockSpec((None, 1, k), lambda i, j: (i // tpb, 0, 0)),
                  pl.BlockSpec((k, tn), lambda i, j: (0, j))],
        out_specs=pl.BlockSpec((tm, tn), lambda i, j: (i, j)),
        scratch_shapes=[pltpu.VMEM((len(dils), tm, k), BF16), pltpu.VMEM((k // LANES, tm, LANES), F32)],
        compiler_params=_cparams(("parallel", "arbitrary")),
        name=name,
    )(x2d, geff, shift, w)


def _cmp_kernel(x_ref, pos_ref, w1_ref, w2_ref, o_ref):
    half = w1_ref.shape[0] // 2
    x = x_ref[...]
    n16 = x.shape[0]
    a = _dot(x, w1_ref[:half, :])
    b = _dot(x, w1_ref[half:, :])
    c = _dot(pos_ref[...], w1_ref[...])
    hid = a + pltpu.roll(b, n16 - 1, 0) + c
    hid = jax.nn.gelu(hid, approximate=True)
    o_ref[...] = _dot(hid.astype(BF16), w2_ref[...]).astype(o_ref.dtype)


def _nsa_compress(x16, pos, w1, w2):
    b, nkv, n16, kk = x16.shape
    g = NSA_KV_HEADS
    return pl.pallas_call(
        _cmp_kernel,
        out_shape=jax.ShapeDtypeStruct((b, nkv, n16, HEAD_DIM), BF16),
        grid=(b, nkv),
        in_specs=[pl.BlockSpec((None, None, n16, kk), lambda i, j: (i, j, 0, 0)),
                  pl.BlockSpec((None, 1, 2 * kk), lambda i, j: (j // g, 0, 0)),
                  pl.BlockSpec((None, 2 * kk, HEAD_DIM), lambda i, j: (j // g, 0, 0)),
                  pl.BlockSpec((None, HEAD_DIM, HEAD_DIM), lambda i, j: (j // g, 0, 0))],
        out_specs=pl.BlockSpec((None, None, n16, HEAD_DIM), lambda i, j: (i, j, 0, 0)),
        compiler_params=_cparams(("parallel", "parallel")),
        name="nsa_compress",
    )(x16, pos, w1, w2)


def _cmpattn_kernel(q_ref, kv_ref, bias_ref, cmat_ref, misc_ref, o_ref, sel_ref, *, tq, n_slc,
                    n_top, scale):
    qi = pl.program_id(1)
    g_n, r_n = NSA_KV_HEADS, NSA_GROUP
    ncp = kv_ref.shape[1]
    trow = qi * tq + lax.broadcasted_iota(jnp.int32, (tq, 1), 0)
    has_c = trow >= (CMP_LEN - 1)
    blk_t = lax.shift_right_logical(trow, int(math.log2(SLC_LEN))).astype(F32)
    jb = lax.broadcasted_iota(jnp.int32, (tq, n_slc), 1).astype(F32)
    forced = (jb == 0.0) | (jb == blk_t) | (jb == blk_t - 1.0)
    valid = jb <= blk_t
    misc = misc_ref[...].astype(F32)
    for g in range(g_n):
        kc = kv_ref[g]
        vc = kv_ref[g_n + g]
        imp = jnp.zeros((tq, ncp), F32)
        for r in range(r_n):
            h = g * r_n + r
            q = q_ref[:, h * HEAD_DIM:(h + 1) * HEAD_DIM]
            s = _dot_nt(q, kc) * scale + bias_ref[h]
            m = jnp.max(s, axis=-1, keepdims=True)
            e = jnp.exp(s - m)
            p = e * (1.0 / jnp.sum(e, axis=-1, keepdims=True))
            p = jnp.where(has_c, p, 0.0)
            imp = imp + p
            o = _dot(p.astype(BF16), vc)
            c0 = GATE_LANE0 + h * 3
            gate = _sigmoid(misc[:, c0:c0 + 1])
            o_ref[:, h * HEAD_DIM:(h + 1) * HEAD_DIM] = gate * o
        imp_s = jnp.dot(imp, cmat_ref[...], precision=lax.Precision.HIGHEST,
                        preferred_element_type=F32)
        score = jnp.where(forced, FORCE_SCORE, jnp.where(valid, imp_s, -1.0))
        sel = jnp.zeros((tq, n_slc), F32)
        for _ in range(n_top):
            mx = jnp.max(score, axis=-1, keepdims=True)
            first = jnp.min(jnp.where(score == mx, jb, float(n_slc)), axis=-1, keepdims=True)
            hit = jb == first
            sel = jnp.where(hit, jnp.where(mx > -0.5, 1.0, 0.0), sel)
            score = jnp.where(hit, -3e38, score)
        sel_ref[g] = sel.astype(sel_ref.dtype)


def _nsa_cmp_attention(z0, kvc, bias_c, cmat, *, tq):
    b, s, _ = z0.shape
    ncp = kvc.shape[2]
    n_slc = s // SLC_LEN
    n_top = min(N_SEL, n_slc)
    qw = NSA_HEADS * HEAD_DIM
    return pl.pallas_call(
        functools.partial(_cmpattn_kernel, tq=tq, n_slc=n_slc, n_top=n_top, scale=HEAD_DIM ** -0.5),
        out_shape=(jax.ShapeDtypeStruct((b, s, qw), F32),
                   jax.ShapeDtypeStruct((b, NSA_KV_HEADS, s, n_slc), BF16)),
        grid=(b, s // tq),
        in_specs=[pl.BlockSpec((None, tq, qw), lambda i, j: (i, j, Z0_Q // qw)),
                  pl.BlockSpec((None, 2 * NSA_KV_HEADS, ncp, HEAD_DIM), lambda i, j: (i, 0, 0, 0)),
                  pl.BlockSpec((NSA_HEADS, tq, ncp), lambda i, j: (0, j, 0)),
                  pl.BlockSpec((ncp, n_slc), lambda i, j: (0, 0)),
                  pl.BlockSpec((None, tq, LANES), lambda i, j: (i, j, Z0_MISC // LANES))],
        out_specs=(pl.BlockSpec((None, tq, qw), lambda i, j: (i, j, 0)),
                   pl.BlockSpec((None, NSA_KV_HEADS, tq, n_slc), lambda i, j: (i, 0, j, 0))),
        compiler_params=_cparams(("parallel", "parallel")),
        name="nsa_cmp_attn",
    )(z0, kvc, bias_c, cmat, z0)


def _flash_kernel(qi_ref, ki_ref, q_ref, k_ref, v_ref, *rest, t, n_kvh, r_n, dk, scale, has_bias,
                  has_sel, gate_branch):
    rest = list(rest)
    bias_ref = rest.pop(0) if has_bias else None
    sel_ref = rest.pop(0) if has_sel else None
    if gate_branch is not None:
        misc_ref = rest.pop(0)
        prev_ref = rest.pop(0)
    o_ref, qs_scr, m_scr, acc_scr = rest
    pidx = pl.program_id(1)
    qi = qi_ref[pidx]
    ki = ki_ref[pidx]
    reps = t // LANES

    @pl.when(ki == 0)
    def _():
        qs_scr[...] = (q_ref[...].astype(F32) * (scale * LOG2E)).astype(BF16)
        m_scr[...] = jnp.full(m_scr.shape, NEG_INF, F32)
        acc_scr[...] = jnp.zeros(acc_scr.shape, F32)

    def step(diag):
        if has_sel:
            per = t // SLC_LEN
            erow = lax.broadcasted_iota(jnp.int32, (sel_ref.shape[2], t), 0)
            ecol = lax.broadcasted_iota(jnp.int32, (sel_ref.shape[2], t), 1)
            expand = jnp.where(erow == ki * per + lax.shift_right_logical(ecol, int(math.log2(SLC_LEN))),
                               1.0, 0.0).astype(BF16)
        if diag:
            causal = (lax.broadcasted_iota(jnp.int32, (t, t), 0)
                      >= lax.broadcasted_iota(jnp.int32, (t, t), 1))
        ones = jnp.ones((t, HEAD_DIM), BF16)
        for kh in range(n_kvh):
            k = k_ref[:, kh * dk:(kh + 1) * dk]
            v_ext = jnp.concatenate([v_ref[:, kh * HEAD_DIM:(kh + 1) * HEAD_DIM], ones], axis=1)
            madd = None
            if has_sel:
                madd = jnp.where(_dot(sel_ref[kh], expand) > 0.5, 0.0, NEG_INF)
                if diag:
                    madd = jnp.where(causal, madd, NEG_INF)
            elif diag:
                madd = jnp.where(causal, 0.0, NEG_INF)
            for r in range(r_n):
                h = kh * r_n + r
                s = _dot_nt(qs_scr[:, h * dk:(h + 1) * dk], k)
                if has_bias:
                    s = s + bias_ref[h]
                if madd is not None:
                    s = s + madd
                m_prev = m_scr[h]
                m_new = jnp.maximum(m_prev, jnp.max(s, axis=-1, keepdims=True))
                alpha = jnp.exp2(m_prev - m_new)
                p = jnp.exp2(s - jnp.concatenate([m_new] * reps, axis=1))
                acc_scr[h] = (jnp.concatenate([alpha, alpha], axis=1) * acc_scr[h]
                              + _dot(p.astype(BF16), v_ext))
                m_scr[h] = m_new

    @pl.when(ki < qi)
    def _():
        step(False)

    @pl.when(ki == qi)
    def _():
        step(True)
        if gate_branch is not None:
            misc = misc_ref[...].astype(F32)
        for h in range(n_kvh * r_n):
            a = acc_scr[h]
            o = a[:, :HEAD_DIM] * (1.0 / a[:, HEAD_DIM:])
            sl = slice(h * HEAD_DIM, (h + 1) * HEAD_DIM)
            if gate_branch is not None:
                c0 = GATE_LANE0 + h * 3 + gate_branch
                o = prev_ref[:, sl] + _sigmoid(misc[:, c0:c0 + 1]) * o
            o_ref[:, sl] = o.astype(o_ref.dtype)


def _tri_pairs(nq):
    qi = np.concatenate([np.full(i + 1, i, np.int32) for i in range(nq)])
    ki = np.concatenate([np.arange(i + 1, dtype=np.int32) for i in range(nq)])
    return jnp.asarray(qi), jnp.asarray(ki)


def _flash(q_arr, k_arr, v_arr, *, t, n_kvh, r_n, dk, q_blk, k_blk, v_blk, scale, out_dtype,
           bias=None, sel=None, misc=None, misc_blk=0, prev=None, gate_branch=None, name):
    b, s, _ = q_arr.shape
    nq = s // t
    qi_a, ki_a = _tri_pairs(nq)
    nh = n_kvh * r_n
    ow = nh * HEAD_DIM
    in_specs = [pl.BlockSpec((None, t, nh * dk), lambda i, p, qa, ka: (i, qa[p], q_blk)),
                pl.BlockSpec((None, t, n_kvh * dk), lambda i, p, qa, ka: (i, ka[p], k_blk)),
                pl.BlockSpec((None, t, n_kvh * HEAD_DIM), lambda i, p, qa, ka: (i, ka[p], v_blk))]
    args = [q_arr, k_arr, v_arr]
    if bias is not None:
        nd = bias.shape[1]
        in_specs.append(pl.BlockSpec((nh, None, t, t),
                                     lambda i, p, qa, ka: (0, jnp.minimum(qa[p] - ka[p], nd - 1), 0, 0)))
        args.append(bias)
    if sel is not None:
        n_slc = sel.shape[-1]
        in_specs.append(pl.BlockSpec((None, n_kvh, t, n_slc), lambda i, p, qa, ka: (i, 0, qa[p], 0)))
        args.append(sel)
    io_alias = {}
    if gate_branch is not None:
        in_specs.append(pl.BlockSpec((None, t, LANES), lambda i, p, qa, ka: (i, qa[p], misc_blk)))
        args.append(misc)
        in_specs.append(pl.BlockSpec((None, t, ow), lambda i, p, qa, ka: (i, qa[p], 0)))
        args.append(prev)
        io_alias = {2 + len(args) - 1: 0}
    return pl.pallas_call(
        functools.partial(_flash_kernel, t=t, n_kvh=n_kvh, r_n=r_n, dk=dk, scale=scale,
                          has_bias=bias is not None, has_sel=sel is not None, gate_branch=gate_branch),
        out_shape=jax.ShapeDtypeStruct((b, s, ow), out_dtype),
        grid_spec=pltpu.PrefetchScalarGridSpec(
            num_scalar_prefetch=2,
            grid=(b, int(qi_a.shape[0])),
            in_specs=in_specs,
            out_specs=pl.BlockSpec((None, t, ow), lambda i, p, qa, ka: (i, qa[p], 0)),
            scratch_shapes=[pltpu.VMEM((t, nh * dk), BF16), pltpu.VMEM((nh, t, LANES), F32),
                            pltpu.VMEM((nh, t, 2 * HEAD_DIM), F32)]),
        input_output_aliases=io_alias,
        compiler_params=_cparams(("parallel", "arbitrary")),
        name=name,
    )(qi_a, ki_a, *args)


def _band_kernel(q_ref, *rest, tq, pb, npv, n_kvh, r_n, scale, gate_branch, want_lse, qi_axis):
    rest = list(rest)
    kp = [rest.pop(0) for _ in range(npv)]
    kc = rest.pop(0)
    vp = [rest.pop(0) for _ in range(npv)]
    vc = rest.pop(0)
    bias_ref = rest.pop(0)
    if gate_branch is not None:
        misc_ref = rest.pop(0)
        prev_ref = rest.pop(0)
    o_ref = rest.pop(0)
    lse_ref = rest.pop(0) if want_lse else None
    qi = pl.program_id(qi_axis)
    nblk = tq // pb
    kw_prev = npv * pb
    if gate_branch is not None:
        misc = misc_ref[...].astype(F32)
    if want_lse:
        lane = lax.broadcasted_iota(jnp.int32, (tq, LANES), 1)
        lse_tile = jnp.zeros((tq, LANES), F32)
    ones_p = jnp.ones((pb, HEAD_DIM), BF16)
    ones_c = jnp.ones((tq, HEAD_DIM), BF16)
    for kh in range(n_kvh):
        ksl = slice(kh * HEAD_DIM, (kh + 1) * HEAD_DIM)
        v_ext = [jnp.concatenate([vp[n][:, ksl], ones_p], axis=1) for n in range(npv)]
        v_ext.append(jnp.concatenate([vc[:, ksl], ones_c], axis=1))
        for r in range(r_n):
            h = kh * r_n + r
            hsl = slice(h * HEAD_DIM, (h + 1) * HEAD_DIM)
            q = (q_ref[:, hsl].astype(F32) * (scale * LOG2E)).astype(BF16)
            parts = []
            for n in range(npv):
                pen = jnp.where(qi * nblk - npv + n >= 0, 0.0, NEG_INF)
                parts.append(_dot_nt(q, kp[n][:, ksl]) + bias_ref[h, :, n * pb:(n + 1) * pb] + pen)
            parts.append(_dot_nt(q, kc[:, ksl]) + bias_ref[h, :, kw_prev:])
            m = parts[0].max(axis=-1, keepdims=True)
            for sp in parts[1:]:
                m = jnp.maximum(m, sp.max(axis=-1, keepdims=True))
            o_ext = jnp.zeros((tq, 2 * HEAD_DIM), F32)
            for n, sp in enumerate(parts):
                o_ext = o_ext + _dot(jnp.exp2(sp - m).astype(BF16), v_ext[n])
            o = o_ext[:, :HEAD_DIM] * (1.0 / o_ext[:, HEAD_DIM:])
            if gate_branch is not None:
                c0 = GATE_LANE0 + h * 3 + gate_branch
                o = prev_ref[:, hsl] + _sigmoid(misc[:, c0:c0 + 1]) * o
            o_ref[:, hsl] = o.astype(o_ref.dtype)
            if want_lse:
                lse = m * (1.0 / LOG2E) + jnp.log(o_ext[:, HEAD_DIM:HEAD_DIM + 1])
                lse_tile = jnp.where(lane == h, lse, lse_tile)
    if want_lse:
        lse_ref[...] = lse_tile


def _band(q_arr, kv_arr, bias, *, lead_grid, tq, pb, npv, n_kvh, r_n, qmap, kmap, vmap, omap,
          lmap=None, out_shape, lse_shape=None, misc=None, miscmap=None, prev=None, gate_branch=None,
          scale, name):
    nl = len(lead_grid)
    nq = q_arr.shape[1] // tq
    nblk = tq // pb
    nh = n_kvh * r_n
    qw = nh * HEAD_DIM
    kw = n_kvh * HEAD_DIM

    def rows_cur(fn):
        def im(*g):
            bb, cc = fn(*g)
            return (bb, g[nl], cc)
        return im

    def rows_prev(fn, n):
        def im(*g):
            bb, cc = fn(*g)
            return (bb, jnp.maximum(g[nl] * nblk - npv + n, 0), cc)
        return im

    in_specs = [pl.BlockSpec((None, tq, qw), rows_cur(qmap))]
    args = [q_arr]
    for fn in (kmap, vmap):
        for n in range(npv):
            in_specs.append(pl.BlockSpec((None, pb, kw), rows_prev(fn, n)))
            args.append(kv_arr)
        in_specs.append(pl.BlockSpec((None, tq, kw), rows_cur(fn)))
        args.append(kv_arr)
    in_specs.append(pl.BlockSpec(bias.shape, lambda *g: (0, 0, 0)))
    args.append(bias)
    io_alias = {}
    if gate_branch is not None:
        in_specs.append(pl.BlockSpec((None, tq, LANES), rows_cur(miscmap)))
        args.append(misc)
        in_specs.append(pl.BlockSpec((None, tq, qw), rows_cur(omap)))
        args.append(prev)
        io_alias = {len(args) - 1: 0}
    out_shapes = [out_shape]
    out_specs = [pl.BlockSpec((None, tq, qw), rows_cur(omap))]
    if lse_shape is not None:
        out_shapes.append(lse_shape)
        out_specs.append(pl.BlockSpec((None, tq, LANES), rows_cur(lmap)))
    res = pl.pallas_call(
        functools.partial(_band_kernel, tq=tq, pb=pb, npv=npv, n_kvh=n_kvh, r_n=r_n, scale=scale,
                          gate_branch=gate_branch, want_lse=lse_shape is not None, qi_axis=nl),
        out_shape=tuple(out_shapes),
        grid=tuple(lead_grid) + (nq,),
        in_specs=in_specs,
        out_specs=tuple(out_specs),
        input_output_aliases=io_alias,
        compiler_params=_cparams(("parallel",) * (nl + 1)),
        name=name,
    )(*args)
    return res


def _dil_kernel(q_ref, *rest, dil, n_prev, back, scale):
    rest = list(rest)
    kp = [rest.pop(0) for _ in range(n_prev)]
    kc = rest.pop(0)
    vp = [rest.pop(0) for _ in range(n_prev)]
    vc = rest.pop(0)
    bias_ref, o_ref, lse_ref = rest
    c = pl.program_id(1)
    tile = q_ref.shape[0]
    n = tile // dil
    pl_rows = back // n_prev
    lane = lax.broadcasted_iota(jnp.int32, (n, LANES), 1)
    ones_p = jnp.ones((pl_rows, HEAD_DIM), BF16)
    ones_c = jnp.ones((n, HEAD_DIM), BF16)
    pens = [jnp.where(c - n_prev + pi >= 0, 0.0, NEG_INF) for pi in range(n_prev)]
    for r in range(dil):
        cur = slice(r * n, (r + 1) * n)
        prev = slice((r + 1) * n - pl_rows, (r + 1) * n)
        lse_tile = jnp.zeros((n, LANES), F32)
        for h in range(DIL_HEADS):
            hsl = slice(h * HEAD_DIM, (h + 1) * HEAD_DIM)
            q = (q_ref[cur, hsl].astype(F32) * (scale * LOG2E)).astype(BF16)
            parts, vals = [], []
            for pi in range(n_prev):
                parts.append(_dot_nt(q, kp[pi][prev, hsl]) + bias_ref[h, :, pi * pl_rows:(pi + 1) * pl_rows]
                             + pens[pi])
                vals.append(jnp.concatenate([vp[pi][prev, hsl], ones_p], axis=1))
            parts.append(_dot_nt(q, kc[cur, hsl]) + bias_ref[h, :, back:])
            vals.append(jnp.concatenate([vc[cur, hsl], ones_c], axis=1))
            m = parts[0].max(axis=-1, keepdims=True)
            for sp in parts[1:]:
                m = jnp.maximum(m, sp.max(axis=-1, keepdims=True))
            o_ext = jnp.zeros((n, 2 * HEAD_DIM), F32)
            for sp, vv in zip(parts, vals):
                o_ext = o_ext + _dot(jnp.exp2(sp - m).astype(BF16), vv)
            o_ref[h, pl.ds(r, n, stride=dil), :] = o_ext[:, :HEAD_DIM] * (1.0 / o_ext[:, HEAD_DIM:])
            lse = m * (1.0 / LOG2E) + jnp.log(o_ext[:, HEAD_DIM:HEAD_DIM + 1])
            lse_tile = jnp.where(lane == h, lse, lse_tile)
        lse_ref[pl.ds(r, n, stride=dil), :] = lse_tile


def _dil_attention(z, bias, *, tile, dil, back, col0, name):
    b, s, _ = z.shape
    n = tile // dil
    n_prev = max(1, back // n)
    hw_c = DIL_HEADS * HEAD_DIM

    def prev_map(pi, cb):
        return lambda i, c: (i, jnp.maximum(c - n_prev + pi, 0), cb)

    in_specs = [pl.BlockSpec((None, tile, hw_c), lambda i, c: (i, c, col0))]
    for cb in (col0 + 1, col0 + 2):
        for pi in range(n_prev):
            in_specs.append(pl.BlockSpec((None, tile, hw_c), prev_map(pi, cb)))
        in_specs.append(pl.BlockSpec((None, tile, hw_c), lambda i, c, cb=cb: (i, c, cb)))
    in_specs.append(pl.BlockSpec(bias.shape, lambda i, c: (0, 0, 0)))
    return pl.pallas_call(
        functools.partial(_dil_kernel, dil=dil, n_prev=n_prev, back=back, scale=HEAD_DIM ** -0.5),
        out_shape=(jax.ShapeDtypeStruct((b, DIL_HEADS, s, HEAD_DIM), F32),
                   jax.ShapeDtypeStruct((b, s, LANES), F32)),
        grid=(b, s // tile),
        in_specs=in_specs,
        out_specs=(pl.BlockSpec((None, DIL_HEADS, tile, HEAD_DIM), lambda i, c: (i, 0, c, 0)),
                   pl.BlockSpec((None, tile, LANES), lambda i, c: (i, c, 0))),
        compiler_params=_cparams(("parallel", "parallel")),
        name=name,
    )(*([z] * (3 + 2 * n_prev)), bias)


def _dil_full_kernel(q_ref, k_ref, v_ref, bias_ref, o_ref, lse_ref, *, dil, tile, heads, scale):
    hp = pl.program_id(1)
    s = q_ref.shape[0]
    n = tile // dil
    ls = s // dil
    lane = lax.broadcasted_iota(jnp.int32, (ls, LANES), 1)
    ones = jnp.ones((ls, HEAD_DIM), BF16)
    for r in range(dil):
        rows = [slice(t * tile + r * n, t * tile + (r + 1) * n) for t in range(s // tile)]
        lse_tile = jnp.zeros((ls, LANES), F32)
        for hh in range(heads):
            h = hp * heads + hh
            hsl = slice(hh * HEAD_DIM, (hh + 1) * HEAD_DIM)
            q = jnp.concatenate([q_ref[rs, hsl] for rs in rows], axis=0)
            k = jnp.concatenate([k_ref[rs, hsl] for rs in rows], axis=0)
            v = jnp.concatenate([v_ref[rs, hsl] for rs in rows] , axis=0)
            q = (q.astype(F32) * (scale * LOG2E)).astype(BF16)
            sc = _dot_nt(q, k) + bias_ref[h]
            m = sc.max(axis=-1, keepdims=True)
            o_ext = _dot(jnp.exp2(sc - m).astype(BF16), jnp.concatenate([v, ones], axis=1))
            o_ref[hh, pl.ds(r, ls, stride=dil), :] = o_ext[:, :HEAD_DIM] * (1.0 / o_ext[:, HEAD_DIM:])
            lse = m * (1.0 / LOG2E) + jnp.log(o_ext[:, HEAD_DIM:HEAD_DIM + 1])
            lse_tile = jnp.where(lane == h, lse, lse_tile)
        lse_ref[pl.ds(r, ls, stride=dil), :] = lse_tile


def _dil_full_attention(z, bias, *, tile, dil, col0, heads, name):
    b, s, _ = z.shape
    hw = heads * HEAD_DIM
    ng = DIL_HEADS // heads
    cpb = (DIL_HEADS * HEAD_DIM) // hw

    def col(j):
        return lambda i, g: (i, 0, (col0 + j) * cpb + g)

    return pl.pallas_call(
        functools.partial(_dil_full_kernel, dil=dil, tile=tile, heads=heads, scale=HEAD_DIM ** -0.5),
        out_shape=(jax.ShapeDtypeStruct((b, DIL_HEADS, s, HEAD_DIM), F32),
                   jax.ShapeDtypeStruct((b, ng, s, LANES), F32)),
        grid=(b, ng),
        in_specs=[pl.BlockSpec((None, s, hw), col(0)), pl.BlockSpec((None, s, hw), col(1)),
                  pl.BlockSpec((None, s, hw), col(2)), pl.BlockSpec(bias.shape, lambda i, g: (0, 0, 0))],
        out_specs=(pl.BlockSpec((None, heads, s, HEAD_DIM), lambda i, g: (i, g, 0, 0)),
                   pl.BlockSpec((None, None, s, LANES), lambda i, g: (i, g, 0, 0))),
        compiler_params=_cparams(("parallel", "parallel")),
        name=name,
    )(z, z, z, bias)


def _dist_table(rel_bias, n_dist, dist_scale=1):
    tab = rel_bias[_t5_bucket(jnp.arange(n_dist) * dist_scale)].astype(F32)
    return jnp.concatenate([tab, jnp.full((1, tab.shape[1]), NEG_INF, F32)], axis=0)


def _toeplitz_kernel(w_ref, o_ref):
    t = o_ref.shape[0]
    x = jnp.broadcast_to(w_ref[...], (t, 2 * t))
    o_ref[...] = pltpu.roll(x, 0, 1, stride=1, stride_axis=0)[:, :t]


def _toeplitz_tiles(tab, d0s, t, lo, hi, mult=1):
    masked = tab.shape[0] - 1
    u = np.arange(2 * t)
    i_minus_j = np.where(u < t, -u, 2 * t - u)
    dist = np.asarray(d0s)[:, None] + mult * i_minus_j[None, :]
    idx = np.where((dist >= lo) & (dist <= hi) & (u != t)[None, :], dist, masked)
    w = tab[jnp.asarray(idx, jnp.int32)].transpose(2, 0, 1)
    nh, nc = w.shape[0], w.shape[1]
    tiles = pl.pallas_call(
        _toeplitz_kernel,
        out_shape=jax.ShapeDtypeStruct((nh * nc, t, t), F32),
        grid=(nh * nc,),
        in_specs=[pl.BlockSpec((None, 1, 2 * t), lambda i: (i, 0, 0))],
        out_specs=pl.BlockSpec((None, t, t), lambda i: (i, 0, 0)),
        compiler_params=_cparams(("parallel",)),
        name="toeplitz_tiles",
    )(w.reshape(nh * nc, 1, 2 * t))
    return tiles.reshape(nh, nc, t, t)


def _band_bias(rel_bias, tq, pb, npv, max_back, dist_scale):
    tab = _dist_table(rel_bias, max_back + 1, dist_scale) * LOG2E
    d0s = [(npv - n) * pb for n in range(npv)] + [0]
    tiles = _toeplitz_tiles(tab, d0s, tq, 0, max_back)
    parts = [tiles[:, n, :, :pb] for n in range(npv)] + [tiles[:, npv]]
    return jnp.concatenate(parts, axis=-1)


def _out_ab_kernel(oa_ref, ob_ref, wa_ref, wb_ref, x_ref, g_ref, o_ref):
    y = _dot(oa_ref[...].astype(BF16), wa_ref[...]) + _dot(ob_ref[...], wb_ref[...])
    o_ref[...] = x_ref[...] + g_ref[...] * y


def _out_ab(oa, ob, w, x2d, gate, *, seq, tm):
    m, d = x2d.shape
    ka = oa.shape[1]
    kb = ob.shape[1]
    tpb = seq // tm
    return pl.pallas_call(
        _out_ab_kernel,
        out_shape=jax.ShapeDtypeStruct((m, d), F32),
        grid=(m // tm,),
        in_specs=[pl.BlockSpec((tm, ka), lambda i: (i, 0)),
                  pl.BlockSpec((tm, kb), lambda i: (i, 0)),
                  pl.BlockSpec((ka, d), lambda i: (0, 0)),
                  pl.BlockSpec((kb, d), lambda i: (ka // kb, 0)),
                  pl.BlockSpec((tm, d), lambda i: (i, 0)),
                  pl.BlockSpec((None, 1, d), lambda i: (i // tpb, 0, 0))],
        out_specs=pl.BlockSpec((tm, d), lambda i: (i, 0)),
        compiler_params=_cparams(("parallel",)),
        name="out_proj_ab",
    )(oa, ob, w, w, x2d, gate)


def _out_c_kernel(o0_ref, o1_ref, o2_ref, l0_ref, l1_ref, l2_ref, w_ref, x_ref, g_ref, o_ref, mrg_scr):
    l0, l1, l2 = l0_ref[...], l1_ref[...], l2_ref[...]
    mx = jnp.maximum(jnp.maximum(l0, l1), l2)
    e0, e1, e2 = jnp.exp(l0 - mx), jnp.exp(l1 - mx), jnp.exp(l2 - mx)
    inv = 1.0 / (e0 + e1 + e2)
    w0, w1, w2 = e0 * inv, e1 * inv, e2 * inv
    for h in range(DIL_HEADS):
        sl = slice(h * HEAD_DIM, (h + 1) * HEAD_DIM)
        mg = (w0[:, h:h + 1] * o0_ref[:, sl] + w1[:, h:h + 1] * o1_ref[h]
              + w2[:, h:h + 1] * o2_ref[h])
        mrg_scr[:, sl] = mg.astype(BF16)
    o_ref[...] = x_ref[...] + g_ref[...] * _dot(mrg_scr[...], w_ref[...])


def _out_c(os_, lses, w, x2d, gate, *, seq, tm):
    m, d = x2d.shape
    kc = w.shape[0]
    tpb = seq // tm
    return pl.pallas_call(
        _out_c_kernel,
        out_shape=jax.ShapeDtypeStruct((m, d), F32),
        grid=(m // tm,),
        in_specs=[pl.BlockSpec((tm, kc), lambda i: (i, 0))]
        + [pl.BlockSpec((None, DIL_HEADS, tm, HEAD_DIM), lambda i: (i // tpb, 0, i % tpb, 0))] * 2
        + [pl.BlockSpec((tm, LANES), lambda i: (i, 0))] * 3
        + [pl.BlockSpec((kc, d), lambda i: (0, 0)),
           pl.BlockSpec((tm, d), lambda i: (i, 0)),
           pl.BlockSpec((None, 1, d), lambda i: (i // tpb, 0, 0))],
        out_specs=pl.BlockSpec((tm, d), lambda i: (i, 0)),
        scratch_shapes=[pltpu.VMEM((tm, kc), BF16)],
        compiler_params=_cparams(("parallel",)),
        name="out_proj_c",
    )(*os_, *lses, w, x2d, gate)


def _ffn_kernel(x_ref, g_ref, s_ref, wg_ref, wu_ref, wd_ref, gate_ref, o_ref, h_scr, acc_scr):
    f = pl.program_id(1)

    @pl.when(f == 0)
    def _():
        x = x_ref[...]
        ms = jnp.mean(x * x, axis=-1, keepdims=True)
        h_scr[...] = (x * lax.rsqrt(ms + RMS_EPS) * g_ref[...] + s_ref[...]).astype(BF16)
        acc_scr[...] = jnp.zeros(acc_scr.shape, F32)

    h = h_scr[...]
    a = _dot(h, wg_ref[...])
    b = _dot(h, wu_ref[...])
    hid = (a * _sigmoid(a) * b).astype(BF16)
    acc_scr[...] += _dot(hid, wd_ref[...])

    @pl.when(f == pl.num_programs(1) - 1)
    def _():
        o_ref[...] = x_ref[...] + gate_ref[...] * acc_scr[...]


def _ffn(x2d, geff, shift, wg, wu, wd, gate, *, seq, tm, tf):
    m, d = x2d.shape
    ff = wg.shape[1]
    tpb = seq // tm
    vec = pl.BlockSpec((None, 1, d), lambda i, f: (i // tpb, 0, 0))
    return pl.pallas_call(
        _ffn_kernel,
        out_shape=jax.ShapeDtypeStruct((m, d), F32),
        grid=(m // tm, ff // tf),
        in_specs=[pl.BlockSpec((tm, d), lambda i, f: (i, 0)), vec, vec,
                  pl.BlockSpec((d, tf), lambda i, f: (0, f)),
                  pl.BlockSpec((d, tf), lambda i, f: (0, f)),
                  pl.BlockSpec((tf, d), lambda i, f: (f, 0)),
                  vec],
        out_specs=pl.BlockSpec((tm, d), lambda i, f: (i, 0)),
        scratch_shapes=[pltpu.VMEM((tm, d), BF16), pltpu.VMEM((tm, d), F32)],
        compiler_params=_cparams(("parallel", "arbitrary")),
        name="ffn_swiglu",
    )(x2d, geff, shift, wg, wu, wd, gate)


def _moe_prep_kernel(x_ref, g_ref, s_ref, wr_ref, h_ref, route_ref):
    x = x_ref[...]
    ms = jnp.mean(x * x, axis=-1, keepdims=True)
    h = x * lax.rsqrt(ms + RMS_EPS) * g_ref[...] + s_ref[...]
    h_ref[...] = h
    logits = jnp.dot(h, wr_ref[...], precision=lax.Precision.HIGHEST, preferred_element_type=F32)
    lane = lax.broadcasted_iota(jnp.int32, logits.shape, 1).astype(F32)
    lg = jnp.where(lane < float(N_EXPERTS), logits, -3e38)
    m1 = jnp.max(lg, axis=-1, keepdims=True)
    i1 = jnp.min(jnp.where(lg == m1, lane, float(LANES)), axis=-1, keepdims=True)
    lg2 = jnp.where(lane == i1, -3e38, lg)
    m2 = jnp.max(lg2, axis=-1, keepdims=True)
    i2 = jnp.min(jnp.where(lg2 == m2, lane, float(LANES)), axis=-1, keepdims=True)
    e = jnp.exp(m2 - m1)
    inv = 1.0 / (1.0 + e)
    route = jnp.where(lane == 0.0, i1, jnp.where(lane == 1.0, i2, jnp.where(lane == 2.0, inv,
                      jnp.where(lane == 3.0, e * inv, 0.0))))
    route_ref[...] = route


def _moe_prep(x2d, geff, shift, w_router_pad, *, seq, tm):
    m, d = x2d.shape
    tpb = seq // tm
    vec = pl.BlockSpec((None, 1, d), lambda i: (i // tpb, 0, 0))
    return pl.pallas_call(
        _moe_prep_kernel,
        out_shape=(jax.ShapeDtypeStruct((m, d), F32), jax.ShapeDtypeStruct((m, LANES), F32)),
        grid=(m // tm,),
        in_specs=[pl.BlockSpec((tm, d), lambda i: (i, 0)), vec, vec,
                  pl.BlockSpec((d, LANES), lambda i: (0, 0))],
        out_specs=(pl.BlockSpec((tm, d), lambda i: (i, 0)), pl.BlockSpec((tm, LANES), lambda i: (i, 0))),
        compiler_params=_cparams(("parallel",)),
        name="moe_prep",
    )(x2d, geff, shift, w_router_pad)


def _row_copy(src_hbm, row, dst_vmem, slot, sem):
    return pltpu.make_async_copy(src_hbm.at[pl.ds(row, 1)], dst_vmem.at[pl.ds(slot, 1)], sem)


def _expert_kernel(te_ref, nu_ref, rows_ref, slot_ref, h_hbm, wg_ref, wu_ref, wd_ref, o_ref, xbuf, xs, sem, *,
                   tm, issue_steps):
    t = pl.program_id(0)
    f = pl.program_id(1)
    nf = pl.num_programs(1)
    n_used = nu_ref[0]
    live = t < n_used
    rows_per_step = tm // issue_steps

    @pl.when((t == 0) & (f == 0))
    def _():
        def start(r, c):
            _row_copy(h_hbm, slot_ref[0, r], xbuf, r, sem).start()
            return c
        lax.fori_loop(0, tm, start, 0, unroll=ROW_DMA_UNROLL)

    @pl.when(live & (f == 0))
    def _():
        def wait(r, c):
            _row_copy(h_hbm, 0, xbuf, r, sem).wait()
            return c
        lax.fori_loop(0, tm, wait, 0, unroll=ROW_DMA_UNROLL)
        xs[...] = xbuf[...].astype(BF16)
        o_ref[...] = jnp.zeros(o_ref.shape, o_ref.dtype)

    def compute(mr):
        x = xs[:mr, :]
        a = _dot(x, wg_ref[...].astype(BF16))
        b = _dot(x, wu_ref[...].astype(BF16))
        hid = (a * _sigmoid(a) * b).astype(BF16)
        o_ref[:mr, :] += _dot(hid, wd_ref[...].astype(BF16))

    prefetch = live & (t + 1 < n_used) & (f >= 1) & (f <= issue_steps)
    half = rows_ref[t] <= tm // 2

    for want_prefetch in (True, False):
        for want_half in (True, False):
            cond = live & (prefetch if want_prefetch else jnp.logical_not(prefetch))
            cond = cond & (half if want_half else jnp.logical_not(half))

            @pl.when(cond)
            def _(want_prefetch=want_prefetch, want_half=want_half):
                if want_prefetch:
                    base = (f - 1) * rows_per_step
                    for j in range(rows_per_step):
                        _row_copy(h_hbm, slot_ref[t + 1, base + j], xbuf, base + j, sem).start()
                compute(tm // 2 if want_half else tm)

    @pl.when(jnp.logical_not(live) & (f == nf - 1))
    def _():
        o_ref[...] = jnp.zeros(o_ref.shape, o_ref.dtype)


def _moe_experts(h2d, slot_tok, tile_e, n_used, tile_rows, wg, wu, wd, *, tm, tf, issue_steps):
    n_tiles = slot_tok.shape[0]
    d = h2d.shape[1]
    ff = wg.shape[2]
    nf = ff // tf
    assert tm % issue_steps == 0 and issue_steps < nf

    def f_blk(t, f, nu):
        return jnp.where(t < nu[0], f, nf - 1)

    return pl.pallas_call(
        functools.partial(_expert_kernel, tm=tm, issue_steps=issue_steps),
        out_shape=jax.ShapeDtypeStruct((n_tiles * tm, d), F32),
        grid_spec=pltpu.PrefetchScalarGridSpec(
            num_scalar_prefetch=4,
            grid=(n_tiles, nf),
            in_specs=[pl.BlockSpec(memory_space=pl.ANY),
                      pl.BlockSpec((None, d, tf), lambda t, f, te, nu, rw, sl: (te[t], 0, f_blk(t, f, nu))),
                      pl.BlockSpec((None, d, tf), lambda t, f, te, nu, rw, sl: (te[t], 0, f_blk(t, f, nu))),
                      pl.BlockSpec((None, tf, d), lambda t, f, te, nu, rw, sl: (te[t], f_blk(t, f, nu), 0))],
            out_specs=pl.BlockSpec((tm, d), lambda t, f, te, nu, rw, sl: (t, 0)),
            scratch_shapes=[pltpu.VMEM((tm, d), F32), pltpu.VMEM((tm, d), BF16),
                            pltpu.SemaphoreType.DMA(())]),
        compiler_params=_cparams(("arbitrary", "arbitrary")),
        name="moe_experts",
    )(tile_e, n_used, tile_rows, slot_tok, h2d, wg, wu, wd)


def _combine_kernel(d1_ref, d2_ref, yb_hbm, route_ref, x_ref, gate_ref, fg_ref, o_ref, b1, b2, sem, *, tmc):
    t = pl.program_id(0)

    def start(r, c):
        _row_copy(yb_hbm, d1_ref[t, r], b1, r, sem).start()
        _row_copy(yb_hbm, d2_ref[t, r], b2, r, sem).start()
        return c

    def wait(r, c):
        _row_copy(yb_hbm, 0, b1, r, sem).wait()
        _row_copy(yb_hbm, 0, b2, r, sem).wait()
        return c

    lax.fori_loop(0, tmc, start, 0, unroll=ROW_DMA_UNROLL)
    lax.fori_loop(0, tmc, wait, 0, unroll=ROW_DMA_UNROLL)
    route = route_ref[...]
    y = route[:, 2:3] * b1[...] + route[:, 3:4] * b2[...]
    xo = x_ref[...] + gate_ref[...] * y
    ms = jnp.mean(xo * xo, axis=-1, keepdims=True)
    o_ref[...] = xo * lax.rsqrt(ms + RMS_EPS) * fg_ref[...]


def _moe_combine(yb, dest1, dest2, route, x2d, gate, final_g, *, seq, tmc):
    m, d = x2d.shape
    tpb = seq // tmc
    return pl.pallas_call(
        functools.partial(_combine_kernel, tmc=tmc),
        out_shape=jax.ShapeDtypeStruct((m, d), F32),
        grid_spec=pltpu.PrefetchScalarGridSpec(
            num_scalar_prefetch=2,
            grid=(m // tmc,),
            in_specs=[pl.BlockSpec(memory_space=pl.ANY),
                      pl.BlockSpec((tmc, LANES), lambda t, a, b: (t, 0)),
                      pl.BlockSpec((tmc, d), lambda t, a, b: (t, 0)),
                      pl.BlockSpec((None, 1, d), lambda t, a, b: (t // tpb, 0, 0)),
                      pl.BlockSpec((1, d), lambda t, a, b: (0, 0))],
            out_specs=pl.BlockSpec((tmc, d), lambda t, a, b: (t, 0)),
            scratch_shapes=[pltpu.VMEM((tmc, d), F32), pltpu.VMEM((tmc, d), F32),
                            pltpu.SemaphoreType.DMA(())]),
        compiler_params=_cparams(("arbitrary",)),
        name="moe_combine",
    )(dest1, dest2, yb, route, x2d, gate, final_g)


def _moe_plan(route, *, tm):
    n = route.shape[0]
    a = n * MOE_TOP_K
    flat_e = route[:, :MOE_TOP_K].astype(jnp.int32).reshape(a)
    onehot = (flat_e[:, None] == jnp.arange(N_EXPERTS, dtype=jnp.int32)[None, :]).astype(jnp.int32)
    csum = jnp.cumsum(onehot, axis=0)
    pos = jnp.sum(onehot * (csum - 1), axis=1)
    counts = csum[-1]
    pcounts = (counts + tm - 1) // tm * tm
    pends = jnp.cumsum(pcounts)
    pstarts = pends - pcounts
    dest = (pstarts[flat_e] + pos).astype(jnp.int32)
    n_tiles = a // tm + N_EXPERTS
    tok = jnp.arange(a, dtype=jnp.int32) // MOE_TOP_K
    slot_tok = jnp.zeros((n_tiles * tm,), jnp.int32).at[dest].set(tok)
    n_used = (pends[-1] // tm).astype(jnp.int32)
    tile_e = jnp.minimum(jnp.searchsorted(pends, jnp.arange(n_tiles, dtype=jnp.int32) * tm, side='right'),
                         N_EXPERTS - 1).astype(jnp.int32)
    tile_e = jnp.where(jnp.arange(n_tiles) < n_used, tile_e, tile_e[jnp.maximum(n_used - 1, 0)])
    tile_rows = jnp.clip(counts[tile_e] - (jnp.arange(n_tiles, dtype=jnp.int32) * tm - pstarts[tile_e]), 0, tm)
    tile_rows = jnp.where(jnp.arange(n_tiles) < n_used, tile_rows, 0).astype(jnp.int32)
    return dest.reshape(n, MOE_TOP_K), slot_tok.reshape(n_tiles, tm), tile_e, n_used.reshape(1), tile_rows


def _tile(n, pref):
    t = min(n, pref)
    assert n % t == 0, (n, pref)
    return t


def kernel(x, c, positions, rel_bias, ada_w, ada_b, mix_norm_g, ffn_norm_g, ab_w_in, ab_w_out, nsa_cmp_pos_k, nsa_cmp_w1_k, nsa_cmp_w2_k, nsa_cmp_pos_v, nsa_cmp_w1_v, nsa_cmp_w2_v, mla_q_norm_g, mla_kv_norm_g, mla_w_uq, mla_w_ukv, ffn_w_gate, ffn_w_up, ffn_w_down, c_w_in, c_w_out, moe_w_router, moe_w_gate, moe_w_up, moe_w_down, final_norm_g):
    b, s, d = x.shape
    assert ada_w.shape[0] == 2 and d == D_MODEL and s % 256 == 0
    m = b * s
    x2d = x.reshape(m, d)
    tm_big = _tile(s, 1024)
    tm_mid = _tile(s, 512)
    t_att = _tile(s, 256)
    t_flash = _tile(s, 512)

    mod = _ada_mod(c, ada_w, ada_b)
    mods = mod.reshape(2, b, 6, 1, d)

    def layer_mod(i):
        sh_m, sc_m, g_m, sh_f, sc_f, g_f = (mods[i, :, j] for j in range(6))
        return (mix_norm_g[i][None, None, :] * (1.0 + sc_m), sh_m, g_m,
                ffn_norm_g[i][None, None, :] * (1.0 + sc_f), sh_f, g_f)

    geff_m, sh_m, g_m, geff_f, sh_f, g_f = layer_mod(0)
    w0 = ab_w_in[0]
    c_q, c_kv, c_g, c_ql, c_kvl = np.cumsum([NSA_HEADS * HEAD_DIM, 6 * NSA_KV_HEADS * HEAD_DIM,
                                             3 * NSA_HEADS, Q_LORA, KV_LORA]).tolist()
    zpad = lambda n: jnp.zeros((d, n), w0.dtype)
    w_in0 = jnp.concatenate([w0[:, :c_kv], w0[:, c_ql:c_kvl], w0[:, c_kvl:], w0[:, c_kv:c_g],
                             zpad(LANES - QK_ROPE - 3 * NSA_HEADS), w0[:, c_g:c_ql], zpad(LANES)],
                            axis=1).astype(BF16)
    assert w_in0.shape[1] == Z0_W
    z0 = _nmm(x2d, geff_m, sh_m, w_in0, seq=s, tm=tm_big, tn=Z0_W // 3, norm_cols=d, x_block=0,
              out_dtype=BF16, name="proj_in_ab")
    z0_3d = z0.reshape(b, s, Z0_W)

    inv_freq = ROPE_THETA ** (-jnp.arange(0, QK_ROPE, 2, dtype=F32) / QK_ROPE)
    ang = positions.astype(F32)[..., None] * inv_freq
    cos, sin = jnp.cos(ang).reshape(m, -1), jnp.sin(ang).reshape(m, -1)
    hr = QK_ROPE // 2
    zr = lambda n: jnp.zeros((m, n), F32)
    rope_tabs = (jnp.concatenate([cos, cos, zr(LANES - 2 * hr)], axis=1),
                 jnp.concatenate([-sin, zr(LANES - hr)], axis=1),
                 jnp.concatenate([zr(hr), sin, zr(LANES - 2 * hr)], axis=1))
    hw = MXU_DIM
    wq3 = mla_w_uq[0].reshape(Q_LORA, MLA_HEADS, QK_NOPE + QK_ROPE)
    wq = jnp.concatenate([wq3, jnp.zeros((Q_LORA, MLA_HEADS, hw - QK_NOPE - QK_ROPE), F32)], axis=2)
    wq = jnp.concatenate([wq.reshape(Q_LORA, MLA_HEADS * hw), jnp.zeros((LANES, MLA_HEADS * hw), F32)],
                         axis=0).astype(BF16)
    wkv3 = mla_w_ukv[0].reshape(KV_LORA, MLA_HEADS, QK_NOPE + V_DIM)
    wk_top = jnp.concatenate([wkv3[:, :, :QK_NOPE], jnp.zeros((KV_LORA, MLA_HEADS, hw - QK_NOPE), F32)],
                             axis=2).reshape(KV_LORA, MLA_HEADS * hw)
    pe_pass = jnp.zeros((LANES, hw), F32).at[jnp.arange(QK_ROPE), QK_NOPE + jnp.arange(QK_ROPE)].set(1.0)
    wk_bot = jnp.tile(pe_pass, (1, MLA_HEADS))
    wv = jnp.concatenate([wkv3[:, :, QK_NOPE:].reshape(KV_LORA, MLA_HEADS * V_DIM),
                          jnp.zeros((LANES, MLA_HEADS * V_DIM), F32)], axis=0)
    wkv = jnp.concatenate([jnp.concatenate([wk_top, wk_bot], axis=0), wv], axis=1).astype(BF16)
    ones_b = lambda g: jnp.broadcast_to(g[None, None, :], (b, 1, g.shape[0]))
    zeros_lat = jnp.zeros((b, 1, Q_LORA), F32)
    q_mla = _nmm(z0, ones_b(mla_q_norm_g[0]), zeros_lat, wq, seq=s, tm=tm_big, tn=1024, norm_cols=Q_LORA,
                 x_block=Z0_QLAT // LAT_BLOCK, out_dtype=BF16, rope=rope_tabs, rope_tiles=2,
                 name="mla_q_up")
    kv_mla = _nmm(z0, ones_b(mla_kv_norm_g[0]), zeros_lat, wkv, seq=s, tm=tm_big, tn=1024,
                  norm_cols=KV_LORA, x_block=Z0_KVLAT // LAT_BLOCK, out_dtype=BF16, rope=rope_tabs,
                  rope_tiles=2, name="mla_kv_up")
    o_mla = _flash(q_mla.reshape(b, s, -1), kv_mla.reshape(b, s, -1), kv_mla.reshape(b, s, -1),
                   t=t_flash, n_kvh=MLA_HEADS, r_n=1, dk=hw, q_blk=0, k_blk=0,
                   v_blk=(MLA_HEADS * hw) // (MLA_HEADS * V_DIM), scale=(QK_NOPE + QK_ROPE) ** -0.5,
                   out_dtype=BF16, name="mla_attn")

    n16 = s // CMP_STRIDE
    cmp_cols = z0_3d[:, :, Z0_KV:Z0_KV + 2 * NSA_KV_HEADS * HEAD_DIM]
    x16 = cmp_cols.reshape(b, n16, CMP_STRIDE, 2 * NSA_KV_HEADS, HEAD_DIM).transpose(0, 3, 1, 2, 4)
    x16 = x16.reshape(b, 2 * NSA_KV_HEADS, n16, CMP_STRIDE * HEAD_DIM)
    pos_kv = jnp.stack([nsa_cmp_pos_k[0], nsa_cmp_pos_v[0]]).reshape(2, 1, CMP_LEN * HEAD_DIM).astype(BF16)
    w1_kv = jnp.stack([nsa_cmp_w1_k[0], nsa_cmp_w1_v[0]]).astype(BF16)
    w2_kv = jnp.stack([nsa_cmp_w2_k[0], nsa_cmp_w2_v[0]]).astype(BF16)
    kvc = _nsa_compress(x16, pos_kv, w1_kv, w2_kv)

    n_cmp = (s - CMP_LEN) // CMP_STRIDE + 1
    n_slc = s // SLC_LEN
    ratio, span = SLC_LEN // CMP_STRIDE, CMP_LEN // CMP_STRIDE
    cm = np.zeros((n16, n_slc), np.float32)
    for j in range(n_slc):
        for mm in range(ratio):
            for nn in range(span):
                i = ratio * j + mm - nn
                if 0 <= i < n_cmp:
                    cm[i, j] += 1.0
    tab_s = _dist_table(rel_bias, s)
    bias_c = _toeplitz_tiles(tab_s, [r - (CMP_LEN - 1) for r in range(CMP_STRIDE)], n16, 0, s - 1,
                             mult=CMP_STRIDE)
    bias_c = bias_c.transpose(0, 2, 1, 3).reshape(NSA_HEADS, s, n16)
    o_nsa, sel = _nsa_cmp_attention(z0_3d, kvc, bias_c, jnp.asarray(cm), tq=t_att)

    nq = s // t_flash
    nd = min(nq, -(-(T5_MAX_DIST + t_flash - 1) // t_flash) + 1)
    bias_d = _toeplitz_tiles(_dist_table(rel_bias, nd * t_flash) * LOG2E, [dd * t_flash for dd in range(nd)],
                             t_flash, 0, nd * t_flash - 1)
    kvw = NSA_KV_HEADS * HEAD_DIM
    o_nsa = _flash(z0_3d, z0_3d, z0_3d, t=t_flash, n_kvh=NSA_KV_HEADS, r_n=NSA_GROUP, dk=HEAD_DIM,
                   q_blk=0, k_blk=(Z0_KV + 2 * kvw) // kvw, v_blk=(Z0_KV + 3 * kvw) // kvw,
                   scale=HEAD_DIM ** -0.5, out_dtype=F32, bias=bias_d, sel=sel, misc=z0_3d,
                   misc_blk=Z0_MISC // LANES, prev=o_nsa, gate_branch=1, name="nsa_slc_attn")

    npv_w = -(-(WIN - 1) // t_att)
    bias_w = _band_bias(rel_bias, t_att, t_att, npv_w, WIN - 1, 1)
    (o_nsa,) = _band(z0_3d, z0_3d, bias_w, lead_grid=(b,), tq=t_att, pb=t_att, npv=npv_w,
                     n_kvh=NSA_KV_HEADS, r_n=NSA_GROUP,
                     qmap=lambda i, j: (i, 0), kmap=lambda i, j: (i, (Z0_KV + 4 * kvw) // kvw),
                     vmap=lambda i, j: (i, (Z0_KV + 5 * kvw) // kvw), omap=lambda i, j: (i, 0),
                     out_shape=jax.ShapeDtypeStruct((b, s, NSA_HEADS * HEAD_DIM), F32),
                     misc=z0_3d, miscmap=lambda i, j: (i, Z0_MISC // LANES), prev=o_nsa, gate_branch=2,
                     scale=HEAD_DIM ** -0.5, name="nsa_win_attn")

    x2d = _out_ab(o_nsa.reshape(m, -1), o_mla.reshape(m, -1), ab_w_out[0].astype(BF16), x2d, g_m,
                  seq=s, tm=tm_mid)

    x2d = _ffn(x2d, geff_f, sh_f, ffn_w_gate[0].astype(BF16), ffn_w_up[0].astype(BF16),
               ffn_w_down[0].astype(BF16), g_f, seq=s, tm=tm_mid, tf=512)

    geff_m, sh_m, g_m, geff_f, sh_f, g_f = layer_mod(1)
    cw = c_w_in.shape[2]
    dils = tuple(dil for _, dil in DIL_PATTERNS)
    z1 = _nmm_perm(x2d, geff_m, sh_m, c_w_in[0].astype(BF16), seq=s, tm=tm_big, tn=1024, dils=dils,
                   out_dtype=BF16, name="proj_in_c")
    hw_c = DIL_HEADS * HEAD_DIM
    z1_3d = z1.reshape(b, s, cw)
    os_, lses = [], []
    for gidx, (win, dil) in enumerate(DIL_PATTERNS):
        max_back = win // dil
        if dil == 1:
            tq = _tile(s, 256)
            pb = min(tq, 128)
            npv = -(-max_back // pb)
            bias_g = _band_bias(rel_bias, tq, pb, npv, max_back, dil)
            og, lg = _band(z1_3d, z1_3d, bias_g, lead_grid=(b,), tq=tq, pb=pb, npv=npv, n_kvh=DIL_HEADS,
                           r_n=1, qmap=lambda i, j, c0=gidx * 3: (i, c0),
                           kmap=lambda i, j, c0=gidx * 3: (i, c0 + 1),
                           vmap=lambda i, j, c0=gidx * 3: (i, c0 + 2),
                           omap=lambda i, j: (i, 0), lmap=lambda i, j: (i, 0),
                           out_shape=jax.ShapeDtypeStruct((b, s, hw_c), F32),
                           lse_shape=jax.ShapeDtypeStruct((b, s, LANES), F32),
                           scale=HEAD_DIM ** -0.5, name=f"dil_attn_{gidx}")
        elif s // dil <= 256:
            ls = s // dil
            bias_g = _toeplitz_tiles(_dist_table(rel_bias, max_back + 1, dil) * LOG2E, [0], ls, 0,
                                     max_back)[:, 0]
            og, lg = _dil_full_attention(z1_3d, bias_g, tile=tm_big, dil=dil, col0=gidx * 3, heads=2,
                                         name=f"dil_attn_{gidx}")
            lg = lg.sum(axis=1)
        else:
            n_res = tm_big // dil
            n_prev = max(1, max_back // n_res)
            bias_g = _band_bias(rel_bias, n_res, max_back // n_prev, n_prev, max_back, dil)
            og, lg = _dil_attention(z1_3d, bias_g, tile=tm_big, dil=dil, back=max_back, col0=gidx * 3,
                                    name=f"dil_attn_{gidx}")
        os_.append(og.reshape(m, hw_c) if dil == 1 else og)
        lses.append(lg.reshape(m, LANES))
    x2d = _out_c(os_, lses, c_w_out[0].astype(BF16), x2d, g_m, seq=s, tm=tm_mid)

    wr = jnp.concatenate([moe_w_router[0], jnp.zeros((d, LANES - N_EXPERTS), F32)], axis=1)
    h_moe, route = _moe_prep(x2d, geff_f, sh_f, wr, seq=s, tm=tm_mid)
    tm_e = 1024
    dest, slot_tok, tile_e, n_used, tile_rows = _moe_plan(route, tm=tm_e)
    yb = _moe_experts(h_moe, slot_tok, tile_e, n_used, tile_rows, moe_w_gate[0], moe_w_up[0], moe_w_down[0],
                      tm=tm_e, tf=256, issue_steps=16)
    tmc = _tile(s, 256)
    out = _moe_combine(yb, dest[:, 0].reshape(m // tmc, tmc), dest[:, 1].reshape(m // tmc, tmc), route,
                       x2d, g_f, final_norm_g.reshape(1, d), seq=s, tmc=tmc)
    return out.reshape(b, s, d)
```

```python
import functools
import math

import numpy as np
import jax
import jax.numpy as jnp
from jax import lax
from jax.experimental import pallas as pl
from jax.experimental.pallas import tpu as pltpu

F32 = jnp.float32
BF16 = jnp.bfloat16

D_MODEL = 2048
HEAD_DIM = 128
NEG_INF = -1e30
LOG2E = 1.4426950408889634
RMS_EPS = 1e-6
NUM_BUCKETS = 32
T5_MAX_DIST = 2048
NSA_HEADS = 8
NSA_KV_HEADS = 2
NSA_GROUP = NSA_HEADS // NSA_KV_HEADS
CMP_LEN = 32
CMP_STRIDE = 16
SLC_LEN = 64
N_SEL = 16
WIN = 512
FORCE_SCORE = 1e9
MLA_HEADS = 8
Q_LORA = 512
KV_LORA = 512
QK_NOPE = 128
QK_ROPE = 64
V_DIM = 128
ROPE_THETA = 10000.0
DIL_PATTERNS = ((128, 1), (512, 4), (2048, 16))
DIL_HEADS = 8
D_FF = 5632
N_EXPERTS = 8
MOE_TOP_K = 2
D_FF_EXPERT = 7168

LANES = 128
MXU_DIM = 256
VMEM_LIMIT_BYTES = 56 * 1024 * 1024
ROW_DMA_UNROLL = 8
MOE_ROW_SPLITS = 4
PERM_NORM_CHUNKS = 4

Z0_Q = 0
Z0_KV = Z0_Q + NSA_HEADS * HEAD_DIM
Z0_KVLAT = Z0_KV + 6 * NSA_KV_HEADS * HEAD_DIM
Z0_MISC = Z0_KVLAT + KV_LORA
Z0_QLAT = Z0_MISC + LANES
Z0_W = Z0_QLAT + Q_LORA + LANES
LAT_BLOCK = KV_LORA + LANES
GATE_LANE0 = QK_ROPE


def _cparams(sem, vmem=VMEM_LIMIT_BYTES):
    return pltpu.CompilerParams(dimension_semantics=sem, vmem_limit_bytes=vmem)


def _dot(a, b):
    return jnp.dot(a, b, preferred_element_type=F32)


def _dot_nt(a, b):
    return lax.dot_general(a, b, (((1,), (1,)), ((), ())), preferred_element_type=F32)


def _sigmoid(x):
    return 1.0 / (1.0 + jnp.exp(-x))


def _t5_bucket(dist):
    n = jnp.maximum(dist, 0)
    max_exact = NUM_BUCKETS // 2
    nf = jnp.maximum(n, 1).astype(F32)
    large = max_exact + (jnp.log(nf / max_exact) / math.log(T5_MAX_DIST / max_exact)
                         * (NUM_BUCKETS - max_exact)).astype(jnp.int32)
    large = jnp.minimum(large, NUM_BUCKETS - 1)
    return jnp.where(n < max_exact, n, large)


def _ada_kernel(c_ref, w_ref, b_ref, o_ref):
    c = c_ref[...]
    cs = c * _sigmoid(c)
    o_ref[...] = _dot(cs.astype(BF16), w_ref[...].astype(BF16)) + b_ref[...]


def _ada_mod(c, ada_w, ada_b):
    depth, d, n = ada_w.shape
    b = c.shape[0]
    bp = 8
    cpad = jnp.zeros((bp, d), F32).at[:b].set(c)
    tn = 1024
    out = pl.pallas_call(
        _ada_kernel,
        out_shape=jax.ShapeDtypeStruct((depth, bp, n), F32),
        grid=(depth, n // tn),
        in_specs=[pl.BlockSpec((bp, d), lambda l, j: (0, 0)),
                  pl.BlockSpec((None, d, tn), lambda l, j: (l, 0, j)),
                  pl.BlockSpec((None, 1, tn), lambda l, j: (l, 0, j))],
        out_specs=pl.BlockSpec((None, bp, tn), lambda l, j: (l, 0, j)),
        compiler_params=_cparams(("parallel", "parallel")),
        name="ada_mod",
    )(cpad, ada_w, ada_b.reshape(depth, 1, n))
    return out[:, :b]


def _nmm_kernel(x_ref, g_ref, s_ref, w_ref, *rest, norm_cols, rope_tiles, tn):
    if rope_tiles:
        c_ref, s1_ref, s2_ref, o_ref, h_scr = rest
    else:
        o_ref, h_scr = rest
    j = pl.program_id(1)

    @pl.when(j == 0)
    def _():
        x = x_ref[...].astype(F32)
        xn = x[:, :norm_cols]
        ms = jnp.mean(xn * xn, axis=-1, keepdims=True)
        hn = xn * lax.rsqrt(ms + RMS_EPS) * g_ref[...] + s_ref[...]
        h_scr[:, :norm_cols] = hn.astype(BF16)
        if norm_cols < x.shape[1]:
            h_scr[:, norm_cols:] = x[:, norm_cols:].astype(BF16)

    acc = _dot(h_scr[...], w_ref[...])
    o_ref[...] = acc.astype(o_ref.dtype)
    if rope_tiles:
        @pl.when(j < rope_tiles)
        def _():
            for hh in range(tn // MXU_DIM):
                lo = hh * MXU_DIM + LANES
                y = acc[:, lo:lo + LANES]
                y2 = (y * c_ref[...] + pltpu.roll(y, LANES - 32, 1) * s1_ref[...]
                      + pltpu.roll(y, 32, 1) * s2_ref[...])
                o_ref[:, lo:lo + LANES] = y2.astype(o_ref.dtype)


def _nmm(x2d, geff, shift, w, *, seq, tm, tn, norm_cols, x_block, out_dtype, rope=None,
         rope_tiles=0, name):
    m = x2d.shape[0]
    k, n = w.shape
    tpb = seq // tm
    in_specs = [pl.BlockSpec((tm, k), lambda i, j: (i, x_block)),
                pl.BlockSpec((None, 1, norm_cols), lambda i, j: (i // tpb, 0, 0)),
                pl.BlockSpec((None, 1, norm_cols), lambda i, j: (i // tpb, 0, 0)),
                pl.BlockSpec((k, tn), lambda i, j: (0, j))]
    args = [x2d, geff, shift, w]
    if rope_tiles:
        for t in rope:
            in_specs.append(pl.BlockSpec((tm, LANES), lambda i, j: (i, 0)))
            args.append(t)
    return pl.pallas_call(
        functools.partial(_nmm_kernel, norm_cols=norm_cols, rope_tiles=rope_tiles, tn=tn),
        out_shape=jax.ShapeDtypeStruct((m, n), out_dtype),
        grid=(m // tm, n // tn),
        in_specs=in_specs,
        out_specs=pl.BlockSpec((tm, tn), lambda i, j: (i, j)),
        scratch_shapes=[pltpu.VMEM((tm, k), BF16)],
        compiler_params=_cparams(("parallel", "arbitrary")),
        name=name,
    )(*args)


def _nmm_perm_kernel(x_ref, g_ref, s_ref, w_ref, o_ref, h_scr, xcol_scr, *, dils, tiles_per_group):
    j = pl.program_id(1)
    tm, k = x_ref.shape
    ncb = k // LANES

    @pl.when(j == 0)
    def _():
        n0 = tm // PERM_NORM_CHUNKS
        for r in range(PERM_NORM_CHUNKS):
            rows = slice(r * n0, (r + 1) * n0)
            x = x_ref[rows, :]
            ms = jnp.mean(x * x, axis=-1, keepdims=True)
            hn = x * lax.rsqrt(ms + RMS_EPS) * g_ref[...] + s_ref[...]
            for gi, dil in enumerate(dils):
                if dil == 1:
                    h_scr[gi, rows, :] = hn.astype(BF16)
            for cb in range(ncb):
                xcol_scr[cb, rows, :] = hn[:, cb * LANES:(cb + 1) * LANES]
        for gi, dil in enumerate(dils):
            if dil == 1:
                continue
            n = tm // dil
            for r in range(dil):
                for cb in range(ncb):
                    h_scr[gi, r * n:(r + 1) * n, cb * LANES:(cb + 1) * LANES] = (
                        xcol_scr[cb, pl.ds(r, n, stride=dil), :].astype(BF16))

    o_ref[...] = _dot(h_scr[j // tiles_per_group], w_ref[...]).astype(o_ref.dtype)


def _nmm_perm(x2d, geff, shift, w, *, seq, tm, tn, dils, out_dtype, name):
    m, k = x2d.shape
    n = w.shape[1]
    tpb = seq // tm
    tiles_per_group = n // len(dils) // tn
    return pl.pallas_call(
        functools.partial(_nmm_perm_kernel, dils=dils, tiles_per_group=tiles_per_group),
        out_shape=jax.ShapeDtypeStruct((m, n), out_dtype),
        grid=(m // tm, n // tn),
        in_specs=[pl.BlockSpec((tm, k), lambda i, j: (i, 0), pipeline_mode=pl.Buffered(1)),
                  pl.BlockSpec((None, 1, k), lambda i, j: (i // tpb, 0, 0)),
                  pl.BlockSpec((None, 1, k), lambda i, j: (i // tpb, 0, 0)),
                  pl.BlockSpec((k, tn), lambda i, j: (0, j))],
        out_specs=pl.BlockSpec((tm, tn), lambda i, j: (i, j)),
        scratch_shapes=[pltpu.VMEM((len(dils), tm, k), BF16), pltpu.VMEM((k // LANES, tm, LANES), F32)],
        compiler_params=_cparams(("parallel", "arbitrary")),
        name=name,
    )(x2d, geff, shift, w)


def _cmp_kernel(x_ref, pos_ref, w1_ref, w2_ref, o_ref):
    half = w1_ref.shape[0] // 2
    x = x_ref[...]
    n16 = x.shape[0]
    a = _dot(x, w1_ref[:half, :])
    b = _dot(x, w1_ref[half:, :])
    c = _dot(pos_ref[...], w1_ref[...])
    hid = a + pltpu.roll(b, n16 - 1, 0) + c
    hid = jax.nn.gelu(hid, approximate=True)
    o_ref[...] = _dot(hid.astype(BF16), w2_ref[...]).astype(o_ref.dtype)


def _nsa_compress(x16, pos, w1, w2):
    b, nkv, n16, kk = x16.shape
    g = NSA_KV_HEADS
    return pl.pallas_call(
        _cmp_kernel,
        out_shape=jax.ShapeDtypeStruct((b, nkv, n16, HEAD_DIM), BF16),
        grid=(b, nkv),
        in_specs=[pl.BlockSpec((None, None, n16, kk), lambda i, j: (i, j, 0, 0)),
                  pl.BlockSpec((None, 1, 2 * kk), lambda i, j: (j // g, 0, 0)),
                  pl.BlockSpec((None, 2 * kk, HEAD_DIM), lambda i, j: (j // g, 0, 0)),
                  pl.BlockSpec((None, HEAD_DIM, HEAD_DIM), lambda i, j: (j // g, 0, 0))],
        out_specs=pl.BlockSpec((None, None, n16, HEAD_DIM), lambda i, j: (i, j, 0, 0)),
        compiler_params=_cparams(("parallel", "parallel")),
        name="nsa_compress",
    )(x16, pos, w1, w2)


def _cmpattn_kernel(q_ref, kv_ref, bias_ref, cmat_ref, misc_ref, o_ref, sel_ref, *, tq, n_slc,
                    n_top, scale):
    qi = pl.program_id(1)
    g_n, r_n = NSA_KV_HEADS, NSA_GROUP
    ncp = kv_ref.shape[1]
    trow = qi * tq + lax.broadcasted_iota(jnp.int32, (tq, 1), 0)
    has_c = trow >= (CMP_LEN - 1)
    blk_t = lax.shift_right_logical(trow, int(math.log2(SLC_LEN))).astype(F32)
    jb = lax.broadcasted_iota(jnp.int32, (tq, n_slc), 1).astype(F32)
    forced = (jb == 0.0) | (jb == blk_t) | (jb == blk_t - 1.0)
    valid = jb <= blk_t
    misc = misc_ref[...].astype(F32)
    for g in range(g_n):
        kc = kv_ref[g]
        vc = kv_ref[g_n + g]
        imp = jnp.zeros((tq, ncp), F32)
        for r in range(r_n):
            h = g * r_n + r
            q = q_ref[:, h * HEAD_DIM:(h + 1) * HEAD_DIM]
            s = _dot_nt(q, kc) * scale + bias_ref[h]
            m = jnp.max(s, axis=-1, keepdims=True)
            e = jnp.exp(s - m)
            p = e * (1.0 / jnp.sum(e, axis=-1, keepdims=True))
            p = jnp.where(has_c, p, 0.0)
            imp = imp + p
            o = _dot(p.astype(BF16), vc)
            c0 = GATE_LANE0 + h * 3
            gate = _sigmoid(misc[:, c0:c0 + 1])
            o_ref[:, h * HEAD_DIM:(h + 1) * HEAD_DIM] = gate * o
        imp_s = jnp.dot(imp, cmat_ref[...], precision=lax.Precision.HIGHEST,
                        preferred_element_type=F32)
        score = jnp.where(forced, FORCE_SCORE, jnp.where(valid, imp_s, -1.0))
        sel = jnp.zeros((tq, n_slc), F32)
        for _ in range(n_top):
            mx = jnp.max(score, axis=-1, keepdims=True)
            first = jnp.min(jnp.where(score == mx, jb, float(n_slc)), axis=-1, keepdims=True)
            hit = jb == first
            sel = jnp.where(hit, jnp.where(mx > -0.5, 1.0, 0.0), sel)
            score = jnp.where(hit, -3e38, score)
        sel_ref[g] = sel.astype(sel_ref.dtype)


def _nsa_cmp_attention(z0, kvc, bias_c, cmat, *, tq):
    b, s, _ = z0.shape
    ncp = kvc.shape[2]
    n_slc = s // SLC_LEN
    n_top = min(N_SEL, n_slc)
    qw = NSA_HEADS * HEAD_DIM
    return pl.pallas_call(
        functools.partial(_cmpattn_kernel, tq=tq, n_slc=n_slc, n_top=n_top, scale=HEAD_DIM ** -0.5),
        out_shape=(jax.ShapeDtypeStruct((b, s, qw), F32),
                   jax.ShapeDtypeStruct((b, NSA_KV_HEADS, s, n_slc), BF16)),
        grid=(b, s // tq),
        in_specs=[pl.BlockSpec((None, tq, qw), lambda i, j: (i, j, Z0_Q // qw)),
                  pl.BlockSpec((None, 2 * NSA_KV_HEADS, ncp, HEAD_DIM), lambda i, j: (i, 0, 0, 0)),
                  pl.BlockSpec((NSA_HEADS, tq, ncp), lambda i, j: (0, j, 0)),
                  pl.BlockSpec((ncp, n_slc), lambda i, j: (0, 0)),
                  pl.BlockSpec((None, tq, LANES), lambda i, j: (i, j, Z0_MISC // LANES))],
        out_specs=(pl.BlockSpec((None, tq, qw), lambda i, j: (i, j, 0)),
                   pl.BlockSpec((None, NSA_KV_HEADS, tq, n_slc), lambda i, j: (i, 0, j, 0))),
        compiler_params=_cparams(("parallel", "parallel")),
        name="nsa_cmp_attn",
    )(z0, kvc, bias_c, cmat, z0)


def _flash_kernel(qi_ref, ki_ref, q_ref, k_ref, v_ref, *rest, t, n_kvh, r_n, dk, scale, has_bias,
                  has_sel, gate_branch):
    rest = list(rest)
    bias_ref = rest.pop(0) if has_bias else None
    sel_ref = rest.pop(0) if has_sel else None
    if gate_branch is not None:
        misc_ref = rest.pop(0)
        prev_ref = rest.pop(0)
    o_ref, qs_scr, m_scr, acc_scr = rest
    pidx = pl.program_id(1)
    qi = qi_ref[pidx]
    ki = ki_ref[pidx]
    reps = t // LANES

    @pl.when(ki == 0)
    def _():
        qs_scr[...] = (q_ref[...].astype(F32) * (scale * LOG2E)).astype(BF16)
        m_scr[...] = jnp.full(m_scr.shape, NEG_INF, F32)
        acc_scr[...] = jnp.zeros(acc_scr.shape, F32)

    def step(diag):
        if has_sel:
            per = t // SLC_LEN
            erow = lax.broadcasted_iota(jnp.int32, (sel_ref.shape[2], t), 0)
            ecol = lax.broadcasted_iota(jnp.int32, (sel_ref.shape[2], t), 1)
            expand = jnp.where(erow == ki * per + lax.shift_right_logical(ecol, int(math.log2(SLC_LEN))),
                               1.0, 0.0).astype(BF16)
        if diag:
            causal = (lax.broadcasted_iota(jnp.int32, (t, t), 0)
                      >= lax.broadcasted_iota(jnp.int32, (t, t), 1))
        ones = jnp.ones((t, HEAD_DIM), BF16)
        for kh in range(n_kvh):
            k = k_ref[:, kh * dk:(kh + 1) * dk]
            v_ext = jnp.concatenate([v_ref[:, kh * HEAD_DIM:(kh + 1) * HEAD_DIM], ones], axis=1)
            madd = None
            if has_sel:
                madd = jnp.where(_dot(sel_ref[kh], expand) > 0.5, 0.0, NEG_INF)
                if diag:
                    madd = jnp.where(causal, madd, NEG_INF)
            elif diag:
                madd = jnp.where(causal, 0.0, NEG_INF)
            for r in range(r_n):
                h = kh * r_n + r
                s = _dot_nt(qs_scr[:, h * dk:(h + 1) * dk], k)
                if has_bias:
                    s = s + bias_ref[h]
                if madd is not None:
                    s = s + madd
                m_prev = m_scr[h]
                m_new = jnp.maximum(m_prev, jnp.max(s, axis=-1, keepdims=True))
                alpha = jnp.exp2(m_prev - m_new)
                p = jnp.exp2(s - jnp.concatenate([m_new] * reps, axis=1))
                acc_scr[h] = (jnp.concatenate([alpha, alpha], axis=1) * acc_scr[h]
                              + _dot(p.astype(BF16), v_ext))
                m_scr[h] = m_new

    @pl.when(ki < qi)
    def _():
        step(False)

    @pl.when(ki == qi)
    def _():
        step(True)
        if gate_branch is not None:
            misc = misc_ref[...].astype(F32)
        for h in range(n_kvh * r_n):
            a = acc_scr[h]
            o = a[:, :HEAD_DIM] * (1.0 / a[:, HEAD_DIM:])
            sl = slice(h * HEAD_DIM, (h + 1) * HEAD_DIM)
            if gate_branch is not None:
                c0 = GATE_LANE0 + h * 3 + gate_branch
                o = prev_ref[:, sl] + _sigmoid(misc[:, c0:c0 + 1]) * o
            o_ref[:, sl] = o.astype(o_ref.dtype)


def _tri_pairs(nq):
    qi = np.concatenate([np.full(i + 1, i, np.int32) for i in range(nq)])
    ki = np.concatenate([np.arange(i + 1, dtype=np.int32) for i in range(nq)])
    return jnp.asarray(qi), jnp.asarray(ki)


def _flash(q_arr, k_arr, v_arr, *, t, n_kvh, r_n, dk, q_blk, k_blk, v_blk, scale, out_dtype,
           bias=None, sel=None, misc=None, misc_blk=0, prev=None, gate_branch=None, name):
    b, s, _ = q_arr.shape
    nq = s // t
    qi_a, ki_a = _tri_pairs(nq)
    nh = n_kvh * r_n
    ow = nh * HEAD_DIM
    in_specs = [pl.BlockSpec((None, t, nh * dk), lambda i, p, qa, ka: (i, qa[p], q_blk)),
                pl.BlockSpec((None, t, n_kvh * dk), lambda i, p, qa, ka: (i, ka[p], k_blk)),
                pl.BlockSpec((None, t, n_kvh * HEAD_DIM), lambda i, p, qa, ka: (i, ka[p], v_blk))]
    args = [q_arr, k_arr, v_arr]
    if bias is not None:
        nd = bias.shape[1]
        in_specs.append(pl.BlockSpec((nh, None, t, t),
                                     lambda i, p, qa, ka: (0, jnp.minimum(qa[p] - ka[p], nd - 1), 0, 0)))
        args.append(bias)
    if sel is not None:
        n_slc = sel.shape[-1]
        in_specs.append(pl.BlockSpec((None, n_kvh, t, n_slc), lambda i, p, qa, ka: (i, 0, qa[p], 0)))
        args.append(sel)
    io_alias = {}
    if gate_branch is not None:
        in_specs.append(pl.BlockSpec((None, t, LANES), lambda i, p, qa, ka: (i, qa[p], misc_blk)))
        args.append(misc)
        in_specs.append(pl.BlockSpec((None, t, ow), lambda i, p, qa, ka: (i, qa[p], 0)))
        args.append(prev)
        io_alias = {2 + len(args) - 1: 0}
    return pl.pallas_call(
        functools.partial(_flash_kernel, t=t, n_kvh=n_kvh, r_n=r_n, dk=dk, scale=scale,
                          has_bias=bias is not None, has_sel=sel is not None, gate_branch=gate_branch),
        out_shape=jax.ShapeDtypeStruct((b, s, ow), out_dtype),
        grid_spec=pltpu.PrefetchScalarGridSpec(
            num_scalar_prefetch=2,
            grid=(b, int(qi_a.shape[0])),
            in_specs=in_specs,
            out_specs=pl.BlockSpec((None, t, ow), lambda i, p, qa, ka: (i, qa[p], 0)),
            scratch_shapes=[pltpu.VMEM((t, nh * dk), BF16), pltpu.VMEM((nh, t, LANES), F32),
                            pltpu.VMEM((nh, t, 2 * HEAD_DIM), F32)]),
        input_output_aliases=io_alias,
        compiler_params=_cparams(("parallel", "arbitrary")),
        name=name,
    )(qi_a, ki_a, *args)


def _band_kernel(q_ref, *rest, tq, pb, npv, n_kvh, r_n, scale, gate_branch, want_lse, qi_axis):
    rest = list(rest)
    kp = [rest.pop(0) for _ in range(npv)]
    kc = rest.pop(0)
    vp = [rest.pop(0) for _ in range(npv)]
    vc = rest.pop(0)
    bias_ref = rest.pop(0)
    if gate_branch is not None:
        misc_ref = rest.pop(0)
        prev_ref = rest.pop(0)
    o_ref = rest.pop(0)
    lse_ref = rest.pop(0) if want_lse else None
    qi = pl.program_id(qi_axis)
    nblk = tq // pb
    kw_prev = npv * pb
    if gate_branch is not None:
        misc = misc_ref[...].astype(F32)
    if want_lse:
        lane = lax.broadcasted_iota(jnp.int32, (tq, LANES), 1)
        lse_tile = jnp.zeros((tq, LANES), F32)
    ones_p = jnp.ones((pb, HEAD_DIM), BF16)
    ones_c = jnp.ones((tq, HEAD_DIM), BF16)
    for kh in range(n_kvh):
        ksl = slice(kh * HEAD_DIM, (kh + 1) * HEAD_DIM)
        v_ext = [jnp.concatenate([vp[n][:, ksl], ones_p], axis=1) for n in range(npv)]
        v_ext.append(jnp.concatenate([vc[:, ksl], ones_c], axis=1))
        for r in range(r_n):
            h = kh * r_n + r
            hsl = slice(h * HEAD_DIM, (h + 1) * HEAD_DIM)
            q = (q_ref[:, hsl].astype(F32) * (scale * LOG2E)).astype(BF16)
            parts = []
            for n in range(npv):
                pen = jnp.where(qi * nblk - npv + n >= 0, 0.0, NEG_INF)
                parts.append(_dot_nt(q, kp[n][:, ksl]) + bias_ref[h, :, n * pb:(n + 1) * pb] + pen)
            parts.append(_dot_nt(q, kc[:, ksl]) + bias_ref[h, :, kw_prev:])
            m = parts[0].max(axis=-1, keepdims=True)
            for sp in parts[1:]:
                m = jnp.maximum(m, sp.max(axis=-1, keepdims=True))
            o_ext = jnp.zeros((tq, 2 * HEAD_DIM), F32)
            for n, sp in enumerate(parts):
                o_ext = o_ext + _dot(jnp.exp2(sp - m).astype(BF16), v_ext[n])
            o = o_ext[:, :HEAD_DIM] * (1.0 / o_ext[:, HEAD_DIM:])
            if gate_branch is not None:
                c0 = GATE_LANE0 + h * 3 + gate_branch
                o = prev_ref[:, hsl] + _sigmoid(misc[:, c0:c0 + 1]) * o
            o_ref[:, hsl] = o.astype(o_ref.dtype)
            if want_lse:
                lse = m * (1.0 / LOG2E) + jnp.log(o_ext[:, HEAD_DIM:HEAD_DIM + 1])
                lse_tile = jnp.where(lane == h, lse, lse_tile)
    if want_lse:
        lse_ref[...] = lse_tile


def _band(q_arr, kv_arr, bias, *, lead_grid, tq, pb, npv, n_kvh, r_n, qmap, kmap, vmap, omap,
          lmap=None, out_shape, lse_shape=None, misc=None, miscmap=None, prev=None, gate_branch=None,
          scale, name):
    nl = len(lead_grid)
    nq = q_arr.shape[1] // tq
    nblk = tq // pb
    nh = n_kvh * r_n
    qw = nh * HEAD_DIM
    kw = n_kvh * HEAD_DIM

    def rows_cur(fn):
        def im(*g):
            bb, cc = fn(*g)
            return (bb, g[nl], cc)
        return im

    def rows_prev(fn, n):
        def im(*g):
            bb, cc = fn(*g)
            return (bb, jnp.maximum(g[nl] * nblk - npv + n, 0), cc)
        return im

    in_specs = [pl.BlockSpec((None, tq, qw), rows_cur(qmap))]
    args = [q_arr]
    for fn in (kmap, vmap):
        for n in range(npv):
            in_specs.append(pl.BlockSpec((None, pb, kw), rows_prev(fn, n)))
            args.append(kv_arr)
        in_specs.append(pl.BlockSpec((None, tq, kw), rows_cur(fn)))
        args.append(kv_arr)
    in_specs.append(pl.BlockSpec(bias.shape, lambda *g: (0, 0, 0)))
    args.append(bias)
    io_alias = {}
    if gate_branch is not None:
        in_specs.append(pl.BlockSpec((None, tq, LANES), rows_cur(miscmap)))
        args.append(misc)
        in_specs.append(pl.BlockSpec((None, tq, qw), rows_cur(omap)))
        args.append(prev)
        io_alias = {len(args) - 1: 0}
    out_shapes = [out_shape]
    out_specs = [pl.BlockSpec((None, tq, qw), rows_cur(omap))]
    if lse_shape is not None:
        out_shapes.append(lse_shape)
        out_specs.append(pl.BlockSpec((None, tq, LANES), rows_cur(lmap)))
    res = pl.pallas_call(
        functools.partial(_band_kernel, tq=tq, pb=pb, npv=npv, n_kvh=n_kvh, r_n=r_n, scale=scale,
                          gate_branch=gate_branch, want_lse=lse_shape is not None, qi_axis=nl),
        out_shape=tuple(out_shapes),
        grid=tuple(lead_grid) + (nq,),
        in_specs=in_specs,
        out_specs=tuple(out_specs),
        input_output_aliases=io_alias,
        compiler_params=_cparams(("parallel",) * (nl + 1)),
        name=name,
    )(*args)
    return res


def _dil_kernel(q_ref, *rest, dil, n_prev, back, scale):
    rest = list(rest)
    kp = [rest.pop(0) for _ in range(n_prev)]
    kc = rest.pop(0)
    vp = [rest.pop(0) for _ in range(n_prev)]
    vc = rest.pop(0)
    bias_ref, o_ref, lse_ref = rest
    c = pl.program_id(1)
    tile = q_ref.shape[0]
    n = tile // dil
    pl_rows = back // n_prev
    lane = lax.broadcasted_iota(jnp.int32, (n, LANES), 1)
    ones_p = jnp.ones((pl_rows, HEAD_DIM), BF16)
    ones_c = jnp.ones((n, HEAD_DIM), BF16)
    pens = [jnp.where(c - n_prev + pi >= 0, 0.0, NEG_INF) for pi in range(n_prev)]
    for r in range(dil):
        cur = slice(r * n, (r + 1) * n)
        prev = slice((r + 1) * n - pl_rows, (r + 1) * n)
        lse_tile = jnp.zeros((n, LANES), F32)
        for h in range(DIL_HEADS):
            hsl = slice(h * HEAD_DIM, (h + 1) * HEAD_DIM)
            q = (q_ref[cur, hsl].astype(F32) * (scale * LOG2E)).astype(BF16)
            parts, vals = [], []
            for pi in range(n_prev):
                parts.append(_dot_nt(q, kp[pi][prev, hsl]) + bias_ref[h, :, pi * pl_rows:(pi + 1) * pl_rows]
                             + pens[pi])
                vals.append(jnp.concatenate([vp[pi][prev, hsl], ones_p], axis=1))
            parts.append(_dot_nt(q, kc[cur, hsl]) + bias_ref[h, :, back:])
            vals.append(jnp.concatenate([vc[cur, hsl], ones_c], axis=1))
            m = parts[0].max(axis=-1, keepdims=True)
            for sp in parts[1:]:
                m = jnp.maximum(m, sp.max(axis=-1, keepdims=True))
            o_ext = jnp.zeros((n, 2 * HEAD_DIM), F32)
            for sp, vv in zip(parts, vals):
                o_ext = o_ext + _dot(jnp.exp2(sp - m).astype(BF16), vv)
            o_ref[h, pl.ds(r, n, stride=dil), :] = o_ext[:, :HEAD_DIM] * (1.0 / o_ext[:, HEAD_DIM:])
            lse = m * (1.0 / LOG2E) + jnp.log(o_ext[:, HEAD_DIM:HEAD_DIM + 1])
            lse_tile = jnp.where(lane == h, lse, lse_tile)
        lse_ref[pl.ds(r, n, stride=dil), :] = lse_tile


def _dil_attention(z, bias, *, tile, dil, back, col0, name):
    b, s, _ = z.shape
    n = tile // dil
    n_prev = max(1, back // n)
    hw_c = DIL_HEADS * HEAD_DIM

    def prev_map(pi, cb):
        return lambda i, c: (i, jnp.maximum(c - n_prev + pi, 0), cb)

    in_specs = [pl.BlockSpec((None, tile, hw_c), lambda i, c: (i, c, col0))]
    for cb in (col0 + 1, col0 + 2):
        for pi in range(n_prev):
            in_specs.append(pl.BlockSpec((None, tile, hw_c), prev_map(pi, cb)))
        in_specs.append(pl.BlockSpec((None, tile, hw_c), lambda i, c, cb=cb: (i, c, cb)))
    in_specs.append(pl.BlockSpec(bias.shape, lambda i, c: (0, 0, 0)))
    return pl.pallas_call(
        functools.partial(_dil_kernel, dil=dil, n_prev=n_prev, back=back, scale=HEAD_DIM ** -0.5),
        out_shape=(jax.ShapeDtypeStruct((b, DIL_HEADS, s, HEAD_DIM), F32),
                   jax.ShapeDtypeStruct((b, s, LANES), F32)),
        grid=(b, s // tile),
        in_specs=in_specs,
        out_specs=(pl.BlockSpec((None, DIL_HEADS, tile, HEAD_DIM), lambda i, c: (i, 0, c, 0)),
                   pl.BlockSpec((None, tile, LANES), lambda i, c: (i, c, 0))),
        compiler_params=_cparams(("parallel", "parallel")),
        name=name,
    )(*([z] * (3 + 2 * n_prev)), bias)


def _dil_full_kernel(q_ref, k_ref, v_ref, bias_ref, o_ref, lse_ref, *, dil, tile, heads, scale):
    hp = pl.program_id(1)
    s = q_ref.shape[0]
    n = tile // dil
    ls = s // dil
    lane = lax.broadcasted_iota(jnp.int32, (ls, LANES), 1)
    ones = jnp.ones((ls, HEAD_DIM), BF16)
    for r in range(dil):
        rows = [slice(t * tile + r * n, t * tile + (r + 1) * n) for t in range(s // tile)]
        lse_tile = jnp.zeros((ls, LANES), F32)
        for hh in range(heads):
            h = hp * heads + hh
            hsl = slice(hh * HEAD_DIM, (hh + 1) * HEAD_DIM)
            q = jnp.concatenate([q_ref[rs, hsl] for rs in rows], axis=0)
            k = jnp.concatenate([k_ref[rs, hsl] for rs in rows], axis=0)
            v = jnp.concatenate([v_ref[rs, hsl] for rs in rows] , axis=0)
            q = (q.astype(F32) * (scale * LOG2E)).astype(BF16)
            sc = _dot_nt(q, k) + bias_ref[h]
            m = sc.max(axis=-1, keepdims=True)
            o_ext = _dot(jnp.exp2(sc - m).astype(BF16), jnp.concatenate([v, ones], axis=1))
            o_ref[hh, pl.ds(r, ls, stride=dil), :] = o_ext[:, :HEAD_DIM] * (1.0 / o_ext[:, HEAD_DIM:])
            lse = m * (1.0 / LOG2E) + jnp.log(o_ext[:, HEAD_DIM:HEAD_DIM + 1])
            lse_tile = jnp.where(lane == h, lse, lse_tile)
        lse_ref[pl.ds(r, ls, stride=dil), :] = lse_tile


def _dil_full_attention(z, bias, *, tile, dil, col0, heads, name):
    b, s, _ = z.shape
    hw = heads * HEAD_DIM
    ng = DIL_HEADS // heads
    cpb = (DIL_HEADS * HEAD_DIM) // hw

    def col(j):
        return lambda i, g: (i, 0, (col0 + j) * cpb + g)

    return pl.pallas_call(
        functools.partial(_dil_full_kernel, dil=dil, tile=tile, heads=heads, scale=HEAD_DIM ** -0.5),
        out_shape=(jax.ShapeDtypeStruct((b, DIL_HEADS, s, HEAD_DIM), F32),
                   jax.ShapeDtypeStruct((b, ng, s, LANES), F32)),
        grid=(b, ng),
        in_specs=[pl.BlockSpec((None, s, hw), col(0)), pl.BlockSpec((None, s, hw), col(1)),
                  pl.BlockSpec((None, s, hw), col(2)), pl.BlockSpec(bias.shape, lambda i, g: (0, 0, 0))],
        out_specs=(pl.BlockSpec((None, heads, s, HEAD_DIM), lambda i, g: (i, g, 0, 0)),
                   pl.BlockSpec((None, None, s, LANES), lambda i, g: (i, g, 0, 0))),
        compiler_params=_cparams(("parallel", "parallel")),
        name=name,
    )(z, z, z, bias)


def _dist_table(rel_bias, n_dist, dist_scale=1):
    tab = rel_bias[_t5_bucket(jnp.arange(n_dist) * dist_scale)].astype(F32)
    return jnp.concatenate([tab, jnp.full((1, tab.shape[1]), NEG_INF, F32)], axis=0)


def _toeplitz_kernel(w_ref, o_ref):
    t = o_ref.shape[0]
    x = jnp.broadcast_to(w_ref[...], (t, 2 * t))
    o_ref[...] = pltpu.roll(x, 0, 1, stride=1, stride_axis=0)[:, :t]


def _toeplitz_tiles(tab, d0s, t, lo, hi, mult=1):
    masked = tab.shape[0] - 1
    u = np.arange(2 * t)
    i_minus_j = np.where(u < t, -u, 2 * t - u)
    dist = np.asarray(d0s)[:, None] + mult * i_minus_j[None, :]
    idx = np.where((dist >= lo) & (dist <= hi) & (u != t)[None, :], dist, masked)
    w = tab[jnp.asarray(idx, jnp.int32)].transpose(2, 0, 1)
    nh, nc = w.shape[0], w.shape[1]
    tiles = pl.pallas_call(
        _toeplitz_kernel,
        out_shape=jax.ShapeDtypeStruct((nh * nc, t, t), F32),
        grid=(nh * nc,),
        in_specs=[pl.BlockSpec((None, 1, 2 * t), lambda i: (i, 0, 0))],
        out_specs=pl.BlockSpec((None, t, t), lambda i: (i, 0, 0)),
        compiler_params=_cparams(("parallel",)),
        name="toeplitz_tiles",
    )(w.reshape(nh * nc, 1, 2 * t))
    return tiles.reshape(nh, nc, t, t)


def _band_bias(rel_bias, tq, pb, npv, max_back, dist_scale):
    tab = _dist_table(rel_bias, max_back + 1, dist_scale) * LOG2E
    d0s = [(npv - n) * pb for n in range(npv)] + [0]
    tiles = _toeplitz_tiles(tab, d0s, tq, 0, max_back)
    parts = [tiles[:, n, :, :pb] for n in range(npv)] + [tiles[:, npv]]
    return jnp.concatenate(parts, axis=-1)


def _out_ab_kernel(oa_ref, ob_ref, wa_ref, wb_ref, x_ref, g_ref, o_ref):
    y = _dot(oa_ref[...].astype(BF16), wa_ref[...]) + _dot(ob_ref[...], wb_ref[...])
    o_ref[...] = x_ref[...] + g_ref[...] * y


def _out_ab(oa, ob, w, x2d, gate, *, seq, tm):
    m, d = x2d.shape
    ka = oa.shape[1]
    kb = ob.shape[1]
    tpb = seq // tm
    return pl.pallas_call(
        _out_ab_kernel,
        out_shape=jax.ShapeDtypeStruct((m, d), F32),
        grid=(m // tm,),
        in_specs=[pl.BlockSpec((tm, ka), lambda i: (i, 0)),
                  pl.BlockSpec((tm, kb), lambda i: (i, 0)),
                  pl.BlockSpec((ka, d), lambda i: (0, 0)),
                  pl.BlockSpec((kb, d), lambda i: (ka // kb, 0)),
                  pl.BlockSpec((tm, d), lambda i: (i, 0)),
                  pl.BlockSpec((None, 1, d), lambda i: (i // tpb, 0, 0))],
        out_specs=pl.BlockSpec((tm, d), lambda i: (i, 0)),
        compiler_params=_cparams(("parallel",)),
        name="out_proj_ab",
    )(oa, ob, w, w, x2d, gate)


def _out_c_kernel(o0_ref, o1_ref, o2_ref, l0_ref, l1_ref, l2_ref, w_ref, x_ref, g_ref, o_ref, mrg_scr):
    l0, l1, l2 = l0_ref[...], l1_ref[...], l2_ref[...]
    mx = jnp.maximum(jnp.maximum(l0, l1), l2)
    e0, e1, e2 = jnp.exp(l0 - mx), jnp.exp(l1 - mx), jnp.exp(l2 - mx)
    inv = 1.0 / (e0 + e1 + e2)
    w0, w1, w2 = e0 * inv, e1 * inv, e2 * inv
    for h in range(DIL_HEADS):
        sl = slice(h * HEAD_DIM, (h + 1) * HEAD_DIM)
        mg = (w0[:, h:h + 1] * o0_ref[:, sl] + w1[:, h:h + 1] * o1_ref[h]
              + w2[:, h:h + 1] * o2_ref[h])
        mrg_scr[:, sl] = mg.astype(BF16)
    o_ref[...] = x_ref[...] + g_ref[...] * _dot(mrg_scr[...], w_ref[...])


def _out_c(os_, lses, w, x2d, gate, *, seq, tm):
    m, d = x2d.shape
    kc = w.shape[0]
    tpb = seq // tm
    return pl.pallas_call(
        _out_c_kernel,
        out_shape=jax.ShapeDtypeStruct((m, d), F32),
        grid=(m // tm,),
        in_specs=[pl.BlockSpec((tm, kc), lambda i: (i, 0))]
        + [pl.BlockSpec((None, DIL_HEADS, tm, HEAD_DIM), lambda i: (i // tpb, 0, i % tpb, 0))] * 2
        + [pl.BlockSpec((tm, LANES), lambda i: (i, 0))] * 3
        + [pl.BlockSpec((kc, d), lambda i: (0, 0)),
           pl.BlockSpec((tm, d), lambda i: (i, 0)),
           pl.BlockSpec((None, 1, d), lambda i: (i // tpb, 0, 0))],
        out_specs=pl.BlockSpec((tm, d), lambda i: (i, 0)),
        scratch_shapes=[pltpu.VMEM((tm, kc), BF16)],
        compiler_params=_cparams(("parallel",)),
        name="out_proj_c",
    )(*os_, *lses, w, x2d, gate)


def _ffn_kernel(x_ref, g_ref, s_ref, wg_ref, wu_ref, wd_ref, gate_ref, o_ref, h_scr, acc_scr):
    f = pl.program_id(1)

    @pl.when(f == 0)
    def _():
        x = x_ref[...]
        ms = jnp.mean(x * x, axis=-1, keepdims=True)
        h_scr[...] = (x * lax.rsqrt(ms + RMS_EPS) * g_ref[...] + s_ref[...]).astype(BF16)
        acc_scr[...] = jnp.zeros(acc_scr.shape, F32)

    h = h_scr[...]
    a = _dot(h, wg_ref[...])
    b = _dot(h, wu_ref[...])
    hid = (a * _sigmoid(a) * b).astype(BF16)
    acc_scr[...] += _dot(hid, wd_ref[...])

    @pl.when(f == pl.num_programs(1) - 1)
    def _():
        o_ref[...] = x_ref[...] + gate_ref[...] * acc_scr[...]


def _ffn(x2d, geff, shift, wg, wu, wd, gate, *, seq, tm, tf):
    m, d = x2d.shape
    ff = wg.shape[1]
    tpb = seq // tm
    vec = pl.BlockSpec((None, 1, d), lambda i, f: (i // tpb, 0, 0))
    return pl.pallas_call(
        _ffn_kernel,
        out_shape=jax.ShapeDtypeStruct((m, d), F32),
        grid=(m // tm, ff // tf),
        in_specs=[pl.BlockSpec((tm, d), lambda i, f: (i, 0)), vec, vec,
                  pl.BlockSpec((d, tf), lambda i, f: (0, f)),
                  pl.BlockSpec((d, tf), lambda i, f: (0, f)),
                  pl.BlockSpec((tf, d), lambda i, f: (f, 0)),
                  vec],
        out_specs=pl.BlockSpec((tm, d), lambda i, f: (i, 0)),
        scratch_shapes=[pltpu.VMEM((tm, d), BF16), pltpu.VMEM((tm, d), F32)],
        compiler_params=_cparams(("parallel", "arbitrary")),
        name="ffn_swiglu",
    )(x2d, geff, shift, wg, wu, wd, gate)


def _moe_prep_kernel(x_ref, g_ref, s_ref, wr_ref, h_ref, route_ref):
    x = x_ref[...]
    ms = jnp.mean(x * x, axis=-1, keepdims=True)
    h = x * lax.rsqrt(ms + RMS_EPS) * g_ref[...] + s_ref[...]
    h_ref[...] = h
    logits = jnp.dot(h, wr_ref[...], precision=lax.Precision.HIGHEST, preferred_element_type=F32)
    lane = lax.broadcasted_iota(jnp.int32, logits.shape, 1).astype(F32)
    lg = jnp.where(lane < float(N_EXPERTS), logits, -3e38)
    m1 = jnp.max(lg, axis=-1, keepdims=True)
    i1 = jnp.min(jnp.where(lg == m1, lane, float(LANES)), axis=-1, keepdims=True)
    lg2 = jnp.where(lane == i1, -3e38, lg)
    m2 = jnp.max(lg2, axis=-1, keepdims=True)
    i2 = jnp.min(jnp.where(lg2 == m2, lane, float(LANES)), axis=-1, keepdims=True)
    e = jnp.exp(m2 - m1)
    inv = 1.0 / (1.0 + e)
    route = jnp.where(lane == 0.0, i1, jnp.where(lane == 1.0, i2, jnp.where(lane == 2.0, inv,
                      jnp.where(lane == 3.0, e * inv, 0.0))))
    route_ref[...] = route


def _moe_prep(x2d, geff, shift, w_router_pad, *, seq, tm):
    m, d = x2d.shape
    tpb = seq // tm
    vec = pl.BlockSpec((None, 1, d), lambda i: (i // tpb, 0, 0))
    return pl.pallas_call(
        _moe_prep_kernel,
        out_shape=(jax.ShapeDtypeStruct((m, d), F32), jax.ShapeDtypeStruct((m, LANES), F32)),
        grid=(m // tm,),
        in_specs=[pl.BlockSpec((tm, d), lambda i: (i, 0)), vec, vec,
                  pl.BlockSpec((d, LANES), lambda i: (0, 0))],
        out_specs=(pl.BlockSpec((tm, d), lambda i: (i, 0)), pl.BlockSpec((tm, LANES), lambda i: (i, 0))),
        compiler_params=_cparams(("parallel",)),
        name="moe_prep",
    )(x2d, geff, shift, w_router_pad)


def _row_copy(src_hbm, row, dst_vmem, slot, sem):
    return pltpu.make_async_copy(src_hbm.at[pl.ds(row, 1)], dst_vmem.at[pl.ds(slot, 1)], sem)


def _expert_kernel(te_ref, nu_ref, rows_ref, slot_ref, h_hbm, wg_ref, wu_ref, wd_ref, o_ref, xbuf, xs, sem, *,
                   tm, issue_steps):
    t = pl.program_id(0)
    f = pl.program_id(1)
    nf = pl.num_programs(1)
    n_used = nu_ref[0]
    live = t < n_used
    rows_per_step = tm // issue_steps

    @pl.when((t == 0) & (f == 0))
    def _():
        def start(r, c):
            _row_copy(h_hbm, slot_ref[0, r], xbuf, r, sem).start()
            return c
        lax.fori_loop(0, tm, start, 0, unroll=ROW_DMA_UNROLL)

    @pl.when(live & (f == 0))
    def _():
        def wait(r, c):
            _row_copy(h_hbm, 0, xbuf, r, sem).wait()
            return c
        lax.fori_loop(0, tm, wait, 0, unroll=ROW_DMA_UNROLL)
        xs[...] = xbuf[...].astype(BF16)
        o_ref[...] = jnp.zeros(o_ref.shape, o_ref.dtype)

    def compute(mr):
        x = xs[:mr, :]
        a = _dot(x, wg_ref[...].astype(BF16))
        b = _dot(x, wu_ref[...].astype(BF16))
        hid = (a * _sigmoid(a) * b).astype(BF16)
        o_ref[:mr, :] += _dot(hid, wd_ref[...].astype(BF16))

    prefetch = live & (t + 1 < n_used) & (f >= 1) & (f <= issue_steps)
    quarter = tm // MOE_ROW_SPLITS
    n_quarters = (rows_ref[t] + quarter - 1) // quarter

    for want_prefetch in (True, False):
        for nq in range(1, MOE_ROW_SPLITS + 1):
            cond = live & (prefetch if want_prefetch else jnp.logical_not(prefetch)) & (n_quarters == nq)

            @pl.when(cond)
            def _(want_prefetch=want_prefetch, nq=nq):
                if want_prefetch:
                    base = (f - 1) * rows_per_step
                    for j in range(rows_per_step):
                        _row_copy(h_hbm, slot_ref[t + 1, base + j], xbuf, base + j, sem).start()
                compute(nq * quarter)

    @pl.when(jnp.logical_not(live) & (f == nf - 1))
    def _():
        o_ref[...] = jnp.zeros(o_ref.shape, o_ref.dtype)


def _moe_experts(h2d, slot_tok, tile_e, n_used, tile_rows, wg, wu, wd, *, tm, tf, issue_steps):
    n_tiles = slot_tok.shape[0]
    d = h2d.shape[1]
    ff = wg.shape[2]
    nf = ff // tf
    assert tm % issue_steps == 0 and issue_steps < nf

    def f_blk(t, f, nu):
        return jnp.where(t < nu[0], f, nf - 1)

    return pl.pallas_call(
        functools.partial(_expert_kernel, tm=tm, issue_steps=issue_steps),
        out_shape=jax.ShapeDtypeStruct((n_tiles * tm, d), F32),
        grid_spec=pltpu.PrefetchScalarGridSpec(
            num_scalar_prefetch=4,
            grid=(n_tiles, nf),
            in_specs=[pl.BlockSpec(memory_space=pl.ANY),
                      pl.BlockSpec((None, d, tf), lambda t, f, te, nu, rw, sl: (te[t], 0, f_blk(t, f, nu))),
                      pl.BlockSpec((None, d, tf), lambda t, f, te, nu, rw, sl: (te[t], 0, f_blk(t, f, nu))),
                      pl.BlockSpec((None, tf, d), lambda t, f, te, nu, rw, sl: (te[t], f_blk(t, f, nu), 0))],
            out_specs=pl.BlockSpec((tm, d), lambda t, f, te, nu, rw, sl: (t, 0)),
            scratch_shapes=[pltpu.VMEM((tm, d), F32), pltpu.VMEM((tm, d), BF16),
                            pltpu.SemaphoreType.DMA(())]),
        compiler_params=_cparams(("arbitrary", "arbitrary")),
        name="moe_experts",
    )(tile_e, n_used, tile_rows, slot_tok, h2d, wg, wu, wd)


def _combine_kernel(d1_ref, d2_ref, yb_hbm, route_ref, x_ref, gate_ref, fg_ref, o_ref, b1, b2, sem, *, tmc):
    t = pl.program_id(0)

    def start(r, c):
        _row_copy(yb_hbm, d1_ref[t, r], b1, r, sem).start()
        _row_copy(yb_hbm, d2_ref[t, r], b2, r, sem).start()
        return c

    def wait(r, c):
        _row_copy(yb_hbm, 0, b1, r, sem).wait()
        _row_copy(yb_hbm, 0, b2, r, sem).wait()
        return c

    lax.fori_loop(0, tmc, start, 0, unroll=ROW_DMA_UNROLL)
    lax.fori_loop(0, tmc, wait, 0, unroll=ROW_DMA_UNROLL)
    route = route_ref[...]
    y = route[:, 2:3] * b1[...] + route[:, 3:4] * b2[...]
    xo = x_ref[...] + gate_ref[...] * y
    ms = jnp.mean(xo * xo, axis=-1, keepdims=True)
    o_ref[...] = xo * lax.rsqrt(ms + RMS_EPS) * fg_ref[...]


def _moe_combine(yb, dest1, dest2, route, x2d, gate, final_g, *, seq, tmc):
    m, d = x2d.shape
    tpb = seq // tmc
    return pl.pallas_call(
        functools.partial(_combine_kernel, tmc=tmc),
        out_shape=jax.ShapeDtypeStruct((m, d), F32),
        grid_spec=pltpu.PrefetchScalarGridSpec(
            num_scalar_prefetch=2,
            grid=(m // tmc,),
            in_specs=[pl.BlockSpec(memory_space=pl.ANY),
                      pl.BlockSpec((tmc, LANES), lambda t, a, b: (t, 0)),
                      pl.BlockSpec((tmc, d), lambda t, a, b: (t, 0)),
                      pl.BlockSpec((None, 1, d), lambda t, a, b: (t // tpb, 0, 0)),
                      pl.BlockSpec((1, d), lambda t, a, b: (0, 0))],
            out_specs=pl.BlockSpec((tmc, d), lambda t, a, b: (t, 0)),
            scratch_shapes=[pltpu.VMEM((tmc, d), F32), pltpu.VMEM((tmc, d), F32),
                            pltpu.SemaphoreType.DMA(())]),
        compiler_params=_cparams(("arbitrary",)),
        name="moe_combine",
    )(dest1, dest2, yb, route, x2d, gate, final_g)


def _moe_plan(route, *, tm):
    n = route.shape[0]
    a = n * MOE_TOP_K
    flat_e = route[:, :MOE_TOP_K].astype(jnp.int32).reshape(a)
    onehot = (flat_e[:, None] == jnp.arange(N_EXPERTS, dtype=jnp.int32)[None, :]).astype(jnp.int32)
    csum = jnp.cumsum(onehot, axis=0)
    pos = jnp.sum(onehot * (csum - 1), axis=1)
    counts = csum[-1]
    pcounts = (counts + tm - 1) // tm * tm
    pends = jnp.cumsum(pcounts)
    pstarts = pends - pcounts
    dest = (pstarts[flat_e] + pos).astype(jnp.int32)
    n_tiles = a // tm + N_EXPERTS
    tok = jnp.arange(a, dtype=jnp.int32) // MOE_TOP_K
    slot_tok = jnp.zeros((n_tiles * tm,), jnp.int32).at[dest].set(tok)
    n_used = (pends[-1] // tm).astype(jnp.int32)
    tile_e = jnp.minimum(jnp.searchsorted(pends, jnp.arange(n_tiles, dtype=jnp.int32) * tm, side='right'),
                         N_EXPERTS - 1).astype(jnp.int32)
    tile_e = jnp.where(jnp.arange(n_tiles) < n_used, tile_e, tile_e[jnp.maximum(n_used - 1, 0)])
    tile_rows = jnp.clip(counts[tile_e] - (jnp.arange(n_tiles, dtype=jnp.int32) * tm - pstarts[tile_e]), 0, tm)
    tile_rows = jnp.where(jnp.arange(n_tiles) < n_used, tile_rows, 0).astype(jnp.int32)
    return dest.reshape(n, MOE_TOP_K), slot_tok.reshape(n_tiles, tm), tile_e, n_used.reshape(1), tile_rows


def _tile(n, pref):
    t = min(n, pref)
    assert n % t == 0, (n, pref)
    return t


def kernel(x, c, positions, rel_bias, ada_w, ada_b, mix_norm_g, ffn_norm_g, ab_w_in, ab_w_out, nsa_cmp_pos_k, nsa_cmp_w1_k, nsa_cmp_w2_k, nsa_cmp_pos_v, nsa_cmp_w1_v, nsa_cmp_w2_v, mla_q_norm_g, mla_kv_norm_g, mla_w_uq, mla_w_ukv, ffn_w_gate, ffn_w_up, ffn_w_down, c_w_in, c_w_out, moe_w_router, moe_w_gate, moe_w_up, moe_w_down, final_norm_g):
    b, s, d = x.shape
    assert ada_w.shape[0] == 2 and d == D_MODEL and s % 256 == 0
    m = b * s
    x2d = x.reshape(m, d)
    tm_big = _tile(s, 1024)
    tm_mid = _tile(s, 512)
    t_att = _tile(s, 256)
    t_flash = _tile(s, 512)

    mod = _ada_mod(c, ada_w, ada_b)
    mods = mod.reshape(2, b, 6, 1, d)

    def layer_mod(i):
        sh_m, sc_m, g_m, sh_f, sc_f, g_f = (mods[i, :, j] for j in range(6))
        return (mix_norm_g[i][None, None, :] * (1.0 + sc_m), sh_m, g_m,
                ffn_norm_g[i][None, None, :] * (1.0 + sc_f), sh_f, g_f)

    geff_m, sh_m, g_m, geff_f, sh_f, g_f = layer_mod(0)
    w0 = ab_w_in[0]
    c_q, c_kv, c_g, c_ql, c_kvl = np.cumsum([NSA_HEADS * HEAD_DIM, 6 * NSA_KV_HEADS * HEAD_DIM,
                                             3 * NSA_HEADS, Q_LORA, KV_LORA]).tolist()
    zpad = lambda n: jnp.zeros((d, n), w0.dtype)
    w_in0 = jnp.concatenate([w0[:, :c_kv], w0[:, c_ql:c_kvl], w0[:, c_kvl:], w0[:, c_kv:c_g],
                             zpad(LANES - QK_ROPE - 3 * NSA_HEADS), w0[:, c_g:c_ql], zpad(LANES)],
                            axis=1).astype(BF16)
    assert w_in0.shape[1] == Z0_W
    z0 = _nmm(x2d, geff_m, sh_m, w_in0, seq=s, tm=tm_big, tn=Z0_W // 3, norm_cols=d, x_block=0,
              out_dtype=BF16, name="proj_in_ab")
    z0_3d = z0.reshape(b, s, Z0_W)

    inv_freq = ROPE_THETA ** (-jnp.arange(0, QK_ROPE, 2, dtype=F32) / QK_ROPE)
    ang = positions.astype(F32)[..., None] * inv_freq
    cos, sin = jnp.cos(ang).reshape(m, -1), jnp.sin(ang).reshape(m, -1)
    hr = QK_ROPE // 2
    zr = lambda n: jnp.zeros((m, n), F32)
    rope_tabs = (jnp.concatenate([cos, cos, zr(LANES - 2 * hr)], axis=1),
                 jnp.concatenate([-sin, zr(LANES - hr)], axis=1),
                 jnp.concatenate([zr(hr), sin, zr(LANES - 2 * hr)], axis=1))
    hw = MXU_DIM
    wq3 = mla_w_uq[0].reshape(Q_LORA, MLA_HEADS, QK_NOPE + QK_ROPE)
    wq = jnp.concatenate([wq3, jnp.zeros((Q_LORA, MLA_HEADS, hw - QK_NOPE - QK_ROPE), F32)], axis=2)
    wq = jnp.concatenate([wq.reshape(Q_LORA, MLA_HEADS * hw), jnp.zeros((LANES, MLA_HEADS * hw), F32)],
                         axis=0).astype(BF16)
    wkv3 = mla_w_ukv[0].reshape(KV_LORA, MLA_HEADS, QK_NOPE + V_DIM)
    wk_top = jnp.concatenate([wkv3[:, :, :QK_NOPE], jnp.zeros((KV_LORA, MLA_HEADS, hw - QK_NOPE), F32)],
                             axis=2).reshape(KV_LORA, MLA_HEADS * hw)
    pe_pass = jnp.zeros((LANES, hw), F32).at[jnp.arange(QK_ROPE), QK_NOPE + jnp.arange(QK_ROPE)].set(1.0)
    wk_bot = jnp.tile(pe_pass, (1, MLA_HEADS))
    wv = jnp.concatenate([wkv3[:, :, QK_NOPE:].reshape(KV_LORA, MLA_HEADS * V_DIM),
                          jnp.zeros((LANES, MLA_HEADS * V_DIM), F32)], axis=0)
    wkv = jnp.concatenate([jnp.concatenate([wk_top, wk_bot], axis=0), wv], axis=1).astype(BF16)
    ones_b = lambda g: jnp.broadcast_to(g[None, None, :], (b, 1, g.shape[0]))
    zeros_lat = jnp.zeros((b, 1, Q_LORA), F32)
    q_mla = _nmm(z0, ones_b(mla_q_norm_g[0]), zeros_lat, wq, seq=s, tm=tm_big, tn=1024, norm_cols=Q_LORA,
                 x_block=Z0_QLAT // LAT_BLOCK, out_dtype=BF16, rope=rope_tabs, rope_tiles=2,
                 name="mla_q_up")
    kv_mla = _nmm(z0, ones_b(mla_kv_norm_g[0]), zeros_lat, wkv, seq=s, tm=tm_big, tn=1024,
                  norm_cols=KV_LORA, x_block=Z0_KVLAT // LAT_BLOCK, out_dtype=BF16, rope=rope_tabs,
                  rope_tiles=2, name="mla_kv_up")
    o_mla = _flash(q_mla.reshape(b, s, -1), kv_mla.reshape(b, s, -1), kv_mla.reshape(b, s, -1),
                   t=t_flash, n_kvh=MLA_HEADS, r_n=1, dk=hw, q_blk=0, k_blk=0,
                   v_blk=(MLA_HEADS * hw) // (MLA_HEADS * V_DIM), scale=(QK_NOPE + QK_ROPE) ** -0.5,
                   out_dtype=BF16, name="mla_attn")

    n16 = s // CMP_STRIDE
    cmp_cols = z0_3d[:, :, Z0_KV:Z0_KV + 2 * NSA_KV_HEADS * HEAD_DIM]
    x16 = cmp_cols.reshape(b, n16, CMP_STRIDE, 2 * NSA_KV_HEADS, HEAD_DIM).transpose(0, 3, 1, 2, 4)
    x16 = x16.reshape(b, 2 * NSA_KV_HEADS, n16, CMP_STRIDE * HEAD_DIM)
    pos_kv = jnp.stack([nsa_cmp_pos_k[0], nsa_cmp_pos_v[0]]).reshape(2, 1, CMP_LEN * HEAD_DIM).astype(BF16)
    w1_kv = jnp.stack([nsa_cmp_w1_k[0], nsa_cmp_w1_v[0]]).astype(BF16)
    w2_kv = jnp.stack([nsa_cmp_w2_k[0], nsa_cmp_w2_v[0]]).astype(BF16)
    kvc = _nsa_compress(x16, pos_kv, w1_kv, w2_kv)

    n_cmp = (s - CMP_LEN) // CMP_STRIDE + 1
    n_slc = s // SLC_LEN
    ratio, span = SLC_LEN // CMP_STRIDE, CMP_LEN // CMP_STRIDE
    cm = np.zeros((n16, n_slc), np.float32)
    for j in range(n_slc):
        for mm in range(ratio):
            for nn in range(span):
                i = ratio * j + mm - nn
                if 0 <= i < n_cmp:
                    cm[i, j] += 1.0
    tab_s = _dist_table(rel_bias, s)
    bias_c = _toeplitz_tiles(tab_s, [r - (CMP_LEN - 1) for r in range(CMP_STRIDE)], n16, 0, s - 1,
                             mult=CMP_STRIDE)
    bias_c = bias_c.transpose(0, 2, 1, 3).reshape(NSA_HEADS, s, n16)
    o_nsa, sel = _nsa_cmp_attention(z0_3d, kvc, bias_c, jnp.asarray(cm), tq=t_att)

    nq = s // t_flash
    nd = min(nq, -(-(T5_MAX_DIST + t_flash - 1) // t_flash) + 1)
    bias_d = _toeplitz_tiles(_dist_table(rel_bias, nd * t_flash) * LOG2E, [dd * t_flash for dd in range(nd)],
                             t_flash, 0, nd * t_flash - 1)
    kvw = NSA_KV_HEADS * HEAD_DIM
    o_nsa = _flash(z0_3d, z0_3d, z0_3d, t=t_flash, n_kvh=NSA_KV_HEADS, r_n=NSA_GROUP, dk=HEAD_DIM,
                   q_blk=0, k_blk=(Z0_KV + 2 * kvw) // kvw, v_blk=(Z0_KV + 3 * kvw) // kvw,
                   scale=HEAD_DIM ** -0.5, out_dtype=F32, bias=bias_d, sel=sel, misc=z0_3d,
                   misc_blk=Z0_MISC // LANES, prev=o_nsa, gate_branch=1, name="nsa_slc_attn")

    npv_w = -(-(WIN - 1) // t_att)
    bias_w = _band_bias(rel_bias, t_att, t_att, npv_w, WIN - 1, 1)
    (o_nsa,) = _band(z0_3d, z0_3d, bias_w, lead_grid=(b,), tq=t_att, pb=t_att, npv=npv_w,
                     n_kvh=NSA_KV_HEADS, r_n=NSA_GROUP,
                     qmap=lambda i, j: (i, 0), kmap=lambda i, j: (i, (Z0_KV + 4 * kvw) // kvw),
                     vmap=lambda i, j: (i, (Z0_KV + 5 * kvw) // kvw), omap=lambda i, j: (i, 0),
                     out_shape=jax.ShapeDtypeStruct((b, s, NSA_HEADS * HEAD_DIM), F32),
                     misc=z0_3d, miscmap=lambda i, j: (i, Z0_MISC // LANES), prev=o_nsa, gate_branch=2,
                     scale=HEAD_DIM ** -0.5, name="nsa_win_attn")

    x2d = _out_ab(o_nsa.reshape(m, -1), o_mla.reshape(m, -1), ab_w_out[0].astype(BF16), x2d, g_m,
                  seq=s, tm=tm_mid)

    x2d = _ffn(x2d, geff_f, sh_f, ffn_w_gate[0].astype(BF16), ffn_w_up[0].astype(BF16),
               ffn_w_down[0].astype(BF16), g_f, seq=s, tm=tm_mid, tf=512)

    geff_m, sh_m, g_m, geff_f, sh_f, g_f = layer_mod(1)
    cw = c_w_in.shape[2]
    dils = tuple(dil for _, dil in DIL_PATTERNS)
    z1 = _nmm_perm(x2d, geff_m, sh_m, c_w_in[0].astype(BF16), seq=s, tm=tm_big, tn=1024, dils=dils,
                   out_dtype=BF16, name="proj_in_c")
    hw_c = DIL_HEADS * HEAD_DIM
    z1_3d = z1.reshape(b, s, cw)
    os_, lses = [], []
    for gidx, (win, dil) in enumerate(DIL_PATTERNS):
        max_back = win // dil
        if dil == 1:
            tq = _tile(s, 256)
            pb = min(tq, 128)
            npv = -(-max_back // pb)
            bias_g = _band_bias(rel_bias, tq, pb, npv, max_back, dil)
            og, lg = _band(z1_3d, z1_3d, bias_g, lead_grid=(b,), tq=tq, pb=pb, npv=npv, n_kvh=DIL_HEADS,
                           r_n=1, qmap=lambda i, j, c0=gidx * 3: (i, c0),
                           kmap=lambda i, j, c0=gidx * 3: (i, c0 + 1),
                           vmap=lambda i, j, c0=gidx * 3: (i, c0 + 2),
                           omap=lambda i, j: (i, 0), lmap=lambda i, j: (i, 0),
                           out_shape=jax.ShapeDtypeStruct((b, s, hw_c), F32),
                           lse_shape=jax.ShapeDtypeStruct((b, s, LANES), F32),
                           scale=HEAD_DIM ** -0.5, name=f"dil_attn_{gidx}")
        elif s // dil <= 256:
            ls = s // dil
            bias_g = _toeplitz_tiles(_dist_table(rel_bias, max_back + 1, dil) * LOG2E, [0], ls, 0,
                                     max_back)[:, 0]
            og, lg = _dil_full_attention(z1_3d, bias_g, tile=tm_big, dil=dil, col0=gidx * 3, heads=2,
                                         name=f"dil_attn_{gidx}")
            lg = lg.sum(axis=1)
        else:
            n_res = tm_big // dil
            n_prev = max(1, max_back // n_res)
            bias_g = _band_bias(rel_bias, n_res, max_back // n_prev, n_prev, max_back, dil)
            og, lg = _dil_attention(z1_3d, bias_g, tile=tm_big, dil=dil, back=max_back, col0=gidx * 3,
                                    name=f"dil_attn_{gidx}")
        os_.append(og.reshape(m, hw_c) if dil == 1 else og)
        lses.append(lg.reshape(m, LANES))
    x2d = _out_c(os_, lses, c_w_out[0].astype(BF16), x2d, g_m, seq=s, tm=tm_mid)

    wr = jnp.concatenate([moe_w_router[0], jnp.zeros((d, LANES - N_EXPERTS), F32)], axis=1)
    h_moe, route = _moe_prep(x2d, geff_f, sh_f, wr, seq=s, tm=tm_mid)
    tm_e = 1024
    dest, slot_tok, tile_e, n_used, tile_rows = _moe_plan(route, tm=tm_e)
    yb = _moe_experts(h_moe, slot_tok, tile_e, n_used, tile_rows, moe_w_gate[0], moe_w_up[0], moe_w_down[0],
                      tm=tm_e, tf=256, issue_steps=16)
    tmc = _tile(s, 256)
    out = _moe_combine(yb, dest[:, 0].reshape(m // tmc, tmc), dest[:, 1].reshape(m // tmc, tmc), route,
                       x2d, g_f, final_norm_g.reshape(1, d), seq=s, tmc=tmc)
    return out.reshape(b, s, d)
```

```python
import functools
import math

import numpy as np
import jax
import jax.numpy as jnp
from jax import lax
from jax.experimental import pallas as pl
from jax.experimental.pallas import tpu as pltpu

F32 = jnp.float32
BF16 = jnp.bfloat16

D_MODEL = 2048
HEAD_DIM = 128
NEG_INF = -1e30
LOG2E = 1.4426950408889634
RMS_EPS = 1e-6
NUM_BUCKETS = 32
T5_MAX_DIST = 2048
NSA_HEADS = 8
NSA_KV_HEADS = 2
NSA_GROUP = NSA_HEADS // NSA_KV_HEADS
CMP_LEN = 32
CMP_STRIDE = 16
SLC_LEN = 64
N_SEL = 16
WIN = 512
FORCE_SCORE = 1e9
MLA_HEADS = 8
Q_LORA = 512
KV_LORA = 512
QK_NOPE = 128
QK_ROPE = 64
V_DIM = 128
ROPE_THETA = 10000.0
DIL_PATTERNS = ((128, 1), (512, 4), (2048, 16))
DIL_HEADS = 8
D_FF = 5632
N_EXPERTS = 8
MOE_TOP_K = 2
D_FF_EXPERT = 7168

LANES = 128
MXU_DIM = 256
VMEM_LIMIT_BYTES = 56 * 1024 * 1024
ROW_DMA_UNROLL = 8
MOE_ROW_SPLITS = 4
PERM_NORM_CHUNKS = 4

Z0_Q = 0
Z0_KV = Z0_Q + NSA_HEADS * HEAD_DIM
Z0_KVLAT = Z0_KV + 6 * NSA_KV_HEADS * HEAD_DIM
Z0_MISC = Z0_KVLAT + KV_LORA
Z0_QLAT = Z0_MISC + LANES
Z0_W = Z0_QLAT + Q_LORA + LANES
LAT_BLOCK = KV_LORA + LANES
GATE_LANE0 = QK_ROPE


def _cparams(sem, vmem=VMEM_LIMIT_BYTES):
    return pltpu.CompilerParams(dimension_semantics=sem, vmem_limit_bytes=vmem)


def _dot(a, b):
    return jnp.dot(a, b, preferred_element_type=F32)


def _dot_nt(a, b):
    return lax.dot_general(a, b, (((1,), (1,)), ((), ())), preferred_element_type=F32)


def _sigmoid(x):
    return 1.0 / (1.0 + jnp.exp(-x))


def _t5_bucket(dist):
    n = jnp.maximum(dist, 0)
    max_exact = NUM_BUCKETS // 2
    nf = jnp.maximum(n, 1).astype(F32)
    large = max_exact + (jnp.log(nf / max_exact) / math.log(T5_MAX_DIST / max_exact)
                         * (NUM_BUCKETS - max_exact)).astype(jnp.int32)
    large = jnp.minimum(large, NUM_BUCKETS - 1)
    return jnp.where(n < max_exact, n, large)


def _ada_kernel(c_ref, w_ref, b_ref, o_ref):
    c = c_ref[...]
    cs = c * _sigmoid(c)
    o_ref[...] = _dot(cs.astype(BF16), w_ref[...].astype(BF16)) + b_ref[...]


def _ada_mod(c, ada_w, ada_b):
    depth, d, n = ada_w.shape
    b = c.shape[0]
    bp = 8
    cpad = jnp.zeros((bp, d), F32).at[:b].set(c)
    tn = 1024
    out = pl.pallas_call(
        _ada_kernel,
        out_shape=jax.ShapeDtypeStruct((depth, bp, n), F32),
        grid=(depth, n // tn),
        in_specs=[pl.BlockSpec((bp, d), lambda l, j: (0, 0)),
                  pl.BlockSpec((None, d, tn), lambda l, j: (l, 0, j)),
                  pl.BlockSpec((None, 1, tn), lambda l, j: (l, 0, j))],
        out_specs=pl.BlockSpec((None, bp, tn), lambda l, j: (l, 0, j)),
        compiler_params=_cparams(("parallel", "parallel")),
        name="ada_mod",
    )(cpad, ada_w, ada_b.reshape(depth, 1, n))
    return out[:, :b]


def _nmm_kernel(x_ref, g_ref, s_ref, w_ref, *rest, norm_cols, rope_tiles, tn):
    if rope_tiles:
        c_ref, s1_ref, s2_ref, o_ref, h_scr = rest
    else:
        o_ref, h_scr = rest
    j = pl.program_id(1)

    @pl.when(j == 0)
    def _():
        x = x_ref[...].astype(F32)
        xn = x[:, :norm_cols]
        ms = jnp.mean(xn * xn, axis=-1, keepdims=True)
        hn = xn * lax.rsqrt(ms + RMS_EPS) * g_ref[...] + s_ref[...]
        h_scr[:, :norm_cols] = hn.astype(BF16)
        if norm_cols < x.shape[1]:
            h_scr[:, norm_cols:] = x[:, norm_cols:].astype(BF16)

    acc = _dot(h_scr[...], w_ref[...])
    o_ref[...] = acc.astype(o_ref.dtype)
    if rope_tiles:
        @pl.when(j < rope_tiles)
        def _():
            for hh in range(tn // MXU_DIM):
                lo = hh * MXU_DIM + LANES
                y = acc[:, lo:lo + LANES]
                y2 = (y * c_ref[...] + pltpu.roll(y, LANES - 32, 1) * s1_ref[...]
                      + pltpu.roll(y, 32, 1) * s2_ref[...])
                o_ref[:, lo:lo + LANES] = y2.astype(o_ref.dtype)


def _nmm(x2d, geff, shift, w, *, seq, tm, tn, norm_cols, x_block, out_dtype, rope=None,
         rope_tiles=0, name):
    m = x2d.shape[0]
    k, n = w.shape
    tpb = seq // tm
    in_specs = [pl.BlockSpec((tm, k), lambda i, j: (i, x_block)),
                pl.BlockSpec((None, 1, norm_cols), lambda i, j: (i // tpb, 0, 0)),
                pl.BlockSpec((None, 1, norm_cols), lambda i, j: (i // tpb, 0, 0)),
                pl.BlockSpec((k, tn), lambda i, j: (0, j))]
    args = [x2d, geff, shift, w]
    if rope_tiles:
        for t in rope:
            in_specs.append(pl.BlockSpec((tm, LANES), lambda i, j: (i, 0)))
            args.append(t)
    return pl.pallas_call(
        functools.partial(_nmm_kernel, norm_cols=norm_cols, rope_tiles=rope_tiles, tn=tn),
        out_shape=jax.ShapeDtypeStruct((m, n), out_dtype),
        grid=(m // tm, n // tn),
        in_specs=in_specs,
        out_specs=pl.BlockSpec((tm, tn), lambda i, j: (i, j)),
        scratch_shapes=[pltpu.VMEM((tm, k), BF16)],
        compiler_params=_cparams(("parallel", "arbitrary")),
        name=name,
    )(*args)


def _nmm_perm_kernel(x_ref, g_ref, s_ref, w_ref, o_ref, h_scr, xcol_scr, *, dils, tiles_per_group):
    j = pl.program_id(1)
    tm, k = x_ref.shape
    ncb = k // LANES

    @pl.when(j == 0)
    def _():
        n0 = tm // PERM_NORM_CHUNKS
        for r in range(PERM_NORM_CHUNKS):
            rows = slice(r * n0, (r + 1) * n0)
            x = x_ref[rows, :]
            ms = jnp.mean(x * x, axis=-1, keepdims=True)
            hn = x * lax.rsqrt(ms + RMS_EPS) * g_ref[...] + s_ref[...]
            for gi, dil in enumerate(dils):
                if dil == 1:
                    h_scr[gi, rows, :] = hn.astype(BF16)
            for cb in range(ncb):
                xcol_scr[cb, rows, :] = hn[:, cb * LANES:(cb + 1) * LANES]
        for gi, dil in enumerate(dils):
            if dil == 1:
                continue
            n = tm // dil
            for r in range(dil):
                for cb in range(ncb):
                    h_scr[gi, r * n:(r + 1) * n, cb * LANES:(cb + 1) * LANES] = (
                        xcol_scr[cb, pl.ds(r, n, stride=dil), :].astype(BF16))

    o_ref[...] = _dot(h_scr[j // tiles_per_group], w_ref[...]).astype(o_ref.dtype)


def _nmm_perm(x2d, geff, shift, w, *, seq, tm, tn, dils, out_dtype, name):
    m, k = x2d.shape
    n = w.shape[1]
    tpb = seq // tm
    tiles_per_group = n // len(dils) // tn
    return pl.pallas_call(
        functools.partial(_nmm_perm_kernel, dils=dils, tiles_per_group=tiles_per_group),
        out_shape=jax.ShapeDtypeStruct((m, n), out_dtype),
        grid=(m // tm, n // tn),
        in_specs=[pl.BlockSpec((tm, k), lambda i, j: (i, 0), pipeline_mode=pl.Buffered(1)),
                  pl.BlockSpec((None, 1, k), lambda i, j: (i // tpb, 0, 0)),
                  pl.BlockSpec((None, 1, k), lambda i, j: (i // tpb, 0, 0)),
                  pl.BlockSpec((k, tn), lambda i, j: (0, j))],
        out_specs=pl.BlockSpec((tm, tn), lambda i, j: (i, j)),
        scratch_shapes=[pltpu.VMEM((len(dils), tm, k), BF16), pltpu.VMEM((k // LANES, tm, LANES), F32)],
        compiler_params=_cparams(("parallel", "arbitrary")),
        name=name,
    )(x2d, geff, shift, w)


def _cmp_kernel(x_ref, pos_ref, w1_ref, w2_ref, o_ref):
    half = w1_ref.shape[0] // 2
    x = x_ref[...]
    n16 = x.shape[0]
    a = _dot(x, w1_ref[:half, :])
    b = _dot(x, w1_ref[half:, :])
    c = _dot(pos_ref[...], w1_ref[...])
    hid = a + pltpu.roll(b, n16 - 1, 0) + c
    hid = jax.nn.gelu(hid, approximate=True)
    o_ref[...] = _dot(hid.astype(BF16), w2_ref[...]).astype(o_ref.dtype)


def _nsa_compress(x16, pos, w1, w2):
    b, nkv, n16, kk = x16.shape
    g = NSA_KV_HEADS
    return pl.pallas_call(
        _cmp_kernel,
        out_shape=jax.ShapeDtypeStruct((b, nkv, n16, HEAD_DIM), BF16),
        grid=(b, nkv),
        in_specs=[pl.BlockSpec((None, None, n16, kk), lambda i, j: (i, j, 0, 0)),
                  pl.BlockSpec((None, 1, 2 * kk), lambda i, j: (j // g, 0, 0)),
                  pl.BlockSpec((None, 2 * kk, HEAD_DIM), lambda i, j: (j // g, 0, 0)),
                  pl.BlockSpec((None, HEAD_DIM, HEAD_DIM), lambda i, j: (j // g, 0, 0))],
        out_specs=pl.BlockSpec((None, None, n16, HEAD_DIM), lambda i, j: (i, j, 0, 0)),
        compiler_params=_cparams(("parallel", "parallel")),
        name="nsa_compress",
    )(x16, pos, w1, w2)


def _cmpattn_kernel(q_ref, kv_ref, bias_ref, cmat_ref, misc_ref, o_ref, sel_ref, *, tq, n_slc,
                    n_top, scale):
    qi = pl.program_id(1)
    g_n, r_n = NSA_KV_HEADS, NSA_GROUP
    ncp = kv_ref.shape[1]
    trow = qi * tq + lax.broadcasted_iota(jnp.int32, (tq, 1), 0)
    has_c = trow >= (CMP_LEN - 1)
    blk_t = lax.shift_right_logical(trow, int(math.log2(SLC_LEN))).astype(F32)
    jb = lax.broadcasted_iota(jnp.int32, (tq, n_slc), 1).astype(F32)
    forced = (jb == 0.0) | (jb == blk_t) | (jb == blk_t - 1.0)
    valid = jb <= blk_t
    misc = misc_ref[...].astype(F32)
    for g in range(g_n):
        kc = kv_ref[g]
        vc = kv_ref[g_n + g]
        imp = jnp.zeros((tq, ncp), F32)
        for r in range(r_n):
            h = g * r_n + r
            q = q_ref[:, h * HEAD_DIM:(h + 1) * HEAD_DIM]
            s = _dot_nt(q, kc) * scale + bias_ref[h]
            m = jnp.max(s, axis=-1, keepdims=True)
            e = jnp.exp(s - m)
            p = e * (1.0 / jnp.sum(e, axis=-1, keepdims=True))
            p = jnp.where(has_c, p, 0.0)
            imp = imp + p
            o = _dot(p.astype(BF16), vc)
            c0 = GATE_LANE0 + h * 3
            gate = _sigmoid(misc[:, c0:c0 + 1])
            o_ref[:, h * HEAD_DIM:(h + 1) * HEAD_DIM] = gate * o
        imp_s = jnp.dot(imp, cmat_ref[...], precision=lax.Precision.HIGHEST,
                        preferred_element_type=F32)
        score = jnp.where(forced, FORCE_SCORE, jnp.where(valid, imp_s, -1.0))
        sel = jnp.zeros((tq, n_slc), F32)
        for _ in range(n_top):
            mx = jnp.max(score, axis=-1, keepdims=True)
            first = jnp.min(jnp.where(score == mx, jb, float(n_slc)), axis=-1, keepdims=True)
            hit = jb == first
            sel = jnp.where(hit, jnp.where(mx > -0.5, 1.0, 0.0), sel)
            score = jnp.where(hit, -3e38, score)
        sel_ref[g] = sel.astype(sel_ref.dtype)


def _nsa_cmp_attention(z0, kvc, bias_c, cmat, *, tq):
    b, s, _ = z0.shape
    ncp = kvc.shape[2]
    n_slc = s // SLC_LEN
    n_top = min(N_SEL, n_slc)
    qw = NSA_HEADS * HEAD_DIM
    return pl.pallas_call(
        functools.partial(_cmpattn_kernel, tq=tq, n_slc=n_slc, n_top=n_top, scale=HEAD_DIM ** -0.5),
        out_shape=(jax.ShapeDtypeStruct((b, s, qw), F32),
                   jax.ShapeDtypeStruct((b, NSA_KV_HEADS, s, n_slc), BF16)),
        grid=(b, s // tq),
        in_specs=[pl.BlockSpec((None, tq, qw), lambda i, j: (i, j, Z0_Q // qw)),
                  pl.BlockSpec((None, 2 * NSA_KV_HEADS, ncp, HEAD_DIM), lambda i, j: (i, 0, 0, 0)),
                  pl.BlockSpec((NSA_HEADS, tq, ncp), lambda i, j: (0, j, 0)),
                  pl.BlockSpec((ncp, n_slc), lambda i, j: (0, 0)),
                  pl.BlockSpec((None, tq, LANES), lambda i, j: (i, j, Z0_MISC // LANES))],
        out_specs=(pl.BlockSpec((None, tq, qw), lambda i, j: (i, j, 0)),
                   pl.BlockSpec((None, NSA_KV_HEADS, tq, n_slc), lambda i, j: (i, 0, j, 0))),
        compiler_params=_cparams(("parallel", "parallel")),
        name="nsa_cmp_attn",
    )(z0, kvc, bias_c, cmat, z0)


def _flash_kernel(qi_ref, ki_ref, q_ref, k_ref, v_ref, *rest, t, n_kvh, r_n, dk, scale, has_bias,
                  has_sel, gate_branch):
    rest = list(rest)
    bias_ref = rest.pop(0) if has_bias else None
    sel_ref = rest.pop(0) if has_sel else None
    if gate_branch is not None:
        misc_ref = rest.pop(0)
        prev_ref = rest.pop(0)
    o_ref, qs_scr, m_scr, acc_scr = rest
    pidx = pl.program_id(1)
    qi = qi_ref[pidx]
    ki = ki_ref[pidx]
    reps = t // LANES

    @pl.when(ki == 0)
    def _():
        qs_scr[...] = (q_ref[...].astype(F32) * (scale * LOG2E)).astype(BF16)
        m_scr[...] = jnp.full(m_scr.shape, NEG_INF, F32)
        acc_scr[...] = jnp.zeros(acc_scr.shape, F32)

    def step(diag):
        if has_sel:
            per = t // SLC_LEN
            erow = lax.broadcasted_iota(jnp.int32, (sel_ref.shape[2], t), 0)
            ecol = lax.broadcasted_iota(jnp.int32, (sel_ref.shape[2], t), 1)
            expand = jnp.where(erow == ki * per + lax.shift_right_logical(ecol, int(math.log2(SLC_LEN))),
                               1.0, 0.0).astype(BF16)
        if diag:
            causal = (lax.broadcasted_iota(jnp.int32, (t, t), 0)
                      >= lax.broadcasted_iota(jnp.int32, (t, t), 1))
        ones = jnp.ones((t, HEAD_DIM), BF16)
        for kh in range(n_kvh):
            k = k_ref[:, kh * dk:(kh + 1) * dk]
            v_ext = jnp.concatenate([v_ref[:, kh * HEAD_DIM:(kh + 1) * HEAD_DIM], ones], axis=1)
            madd = None
            if has_sel:
                madd = jnp.where(_dot(sel_ref[kh], expand) > 0.5, 0.0, NEG_INF)
                if diag:
                    madd = jnp.where(causal, madd, NEG_INF)
            elif diag:
                madd = jnp.where(causal, 0.0, NEG_INF)
            for r in range(r_n):
                h = kh * r_n + r
                s = _dot_nt(qs_scr[:, h * dk:(h + 1) * dk], k)
                if has_bias:
                    s = s + bias_ref[h]
                if madd is not None:
                    s = s + madd
                m_prev = m_scr[h]
                m_new = jnp.maximum(m_prev, jnp.max(s, axis=-1, keepdims=True))
                alpha = jnp.exp2(m_prev - m_new)
                p = jnp.exp2(s - jnp.concatenate([m_new] * reps, axis=1))
                acc_scr[h] = (jnp.concatenate([alpha, alpha], axis=1) * acc_scr[h]
                              + _dot(p.astype(BF16), v_ext))
                m_scr[h] = m_new

    @pl.when(ki < qi)
    def _():
        step(False)

    @pl.when(ki == qi)
    def _():
        step(True)
        if gate_branch is not None:
            misc = misc_ref[...].astype(F32)
        for h in range(n_kvh * r_n):
            a = acc_scr[h]
            o = a[:, :HEAD_DIM] * (1.0 / a[:, HEAD_DIM:])
            sl = slice(h * HEAD_DIM, (h + 1) * HEAD_DIM)
            if gate_branch is not None:
                c0 = GATE_LANE0 + h * 3 + gate_branch
                o = prev_ref[:, sl] + _sigmoid(misc[:, c0:c0 + 1]) * o
            o_ref[:, sl] = o.astype(o_ref.dtype)


def _tri_pairs(nq):
    qi = np.concatenate([np.full(i + 1, i, np.int32) for i in range(nq)])
    ki = np.concatenate([np.arange(i + 1, dtype=np.int32) for i in range(nq)])
    return jnp.asarray(qi), jnp.asarray(ki)


def _flash(q_arr, k_arr, v_arr, *, t, n_kvh, r_n, dk, q_blk, k_blk, v_blk, scale, out_dtype,
           bias=None, sel=None, misc=None, misc_blk=0, prev=None, gate_branch=None, name):
    b, s, _ = q_arr.shape
    nq = s // t
    qi_a, ki_a = _tri_pairs(nq)
    nh = n_kvh * r_n
    ow = nh * HEAD_DIM
    in_specs = [pl.BlockSpec((None, t, nh * dk), lambda i, p, qa, ka: (i, qa[p], q_blk)),
                pl.BlockSpec((None, t, n_kvh * dk), lambda i, p, qa, ka: (i, ka[p], k_blk)),
                pl.BlockSpec((None, t, n_kvh * HEAD_DIM), lambda i, p, qa, ka: (i, ka[p], v_blk))]
    args = [q_arr, k_arr, v_arr]
    if bias is not None:
        nd = bias.shape[1]
        in_specs.append(pl.BlockSpec((nh, None, t, t),
                                     lambda i, p, qa, ka: (0, jnp.minimum(qa[p] - ka[p], nd - 1), 0, 0)))
        args.append(bias)
    if sel is not None:
        n_slc = sel.shape[-1]
        in_specs.append(pl.BlockSpec((None, n_kvh, t, n_slc), lambda i, p, qa, ka: (i, 0, qa[p], 0)))
        args.append(sel)
    io_alias = {}
    if gate_branch is not None:
        in_specs.append(pl.BlockSpec((None, t, LANES), lambda i, p, qa, ka: (i, qa[p], misc_blk)))
        args.append(misc)
        in_specs.append(pl.BlockSpec((None, t, ow), lambda i, p, qa, ka: (i, qa[p], 0)))
        args.append(prev)
        io_alias = {2 + len(args) - 1: 0}
    return pl.pallas_call(
        functools.partial(_flash_kernel, t=t, n_kvh=n_kvh, r_n=r_n, dk=dk, scale=scale,
                          has_bias=bias is not None, has_sel=sel is not None, gate_branch=gate_branch),
        out_shape=jax.ShapeDtypeStruct((b, s, ow), out_dtype),
        grid_spec=pltpu.PrefetchScalarGridSpec(
            num_scalar_prefetch=2,
            grid=(b, int(qi_a.shape[0])),
            in_specs=in_specs,
            out_specs=pl.BlockSpec((None, t, ow), lambda i, p, qa, ka: (i, qa[p], 0)),
            scratch_shapes=[pltpu.VMEM((t, nh * dk), BF16), pltpu.VMEM((nh, t, LANES), F32),
                            pltpu.VMEM((nh, t, 2 * HEAD_DIM), F32)]),
        input_output_aliases=io_alias,
        compiler_params=_cparams(("parallel", "arbitrary")),
        name=name,
    )(qi_a, ki_a, *args)


def _band_kernel(q_ref, *rest, tq, pb, npv, n_kvh, r_n, scale, gate_branch, want_lse, qi_axis):
    rest = list(rest)
    kp = [rest.pop(0) for _ in range(npv)]
    kc = rest.pop(0)
    vp = [rest.pop(0) for _ in range(npv)]
    vc = rest.pop(0)
    bias_ref = rest.pop(0)
    if gate_branch is not None:
        misc_ref = rest.pop(0)
        prev_ref = rest.pop(0)
    o_ref = rest.pop(0)
    lse_ref = rest.pop(0) if want_lse else None
    qi = pl.program_id(qi_axis)
    nblk = tq // pb
    kw_prev = npv * pb
    if gate_branch is not None:
        misc = misc_ref[...].astype(F32)
    if want_lse:
        lane = lax.broadcasted_iota(jnp.int32, (tq, LANES), 1)
        lse_tile = jnp.zeros((tq, LANES), F32)
    ones_p = jnp.ones((pb, HEAD_DIM), BF16)
    ones_c = jnp.ones((tq, HEAD_DIM), BF16)
    for kh in range(n_kvh):
        ksl = slice(kh * HEAD_DIM, (kh + 1) * HEAD_DIM)
        v_ext = [jnp.concatenate([vp[n][:, ksl], ones_p], axis=1) for n in range(npv)]
        v_ext.append(jnp.concatenate([vc[:, ksl], ones_c], axis=1))
        for r in range(r_n):
            h = kh * r_n + r
            hsl = slice(h * HEAD_DIM, (h + 1) * HEAD_DIM)
            q = (q_ref[:, hsl].astype(F32) * (scale * LOG2E)).astype(BF16)
            parts = []
            for n in range(npv):
                pen = jnp.where(qi * nblk - npv + n >= 0, 0.0, NEG_INF)
                parts.append(_dot_nt(q, kp[n][:, ksl]) + bias_ref[h, :, n * pb:(n + 1) * pb] + pen)
            parts.append(_dot_nt(q, kc[:, ksl]) + bias_ref[h, :, kw_prev:])
            m = parts[0].max(axis=-1, keepdims=True)
            for sp in parts[1:]:
                m = jnp.maximum(m, sp.max(axis=-1, keepdims=True))
            o_ext = jnp.zeros((tq, 2 * HEAD_DIM), F32)
            for n, sp in enumerate(parts):
                o_ext = o_ext + _dot(jnp.exp2(sp - m).astype(BF16), v_ext[n])
            o = o_ext[:, :HEAD_DIM] * (1.0 / o_ext[:, HEAD_DIM:])
            if gate_branch is not None:
                c0 = GATE_LANE0 + h * 3 + gate_branch
                o = prev_ref[:, hsl] + _sigmoid(misc[:, c0:c0 + 1]) * o
            o_ref[:, hsl] = o.astype(o_ref.dtype)
            if want_lse:
                lse = m * (1.0 / LOG2E) + jnp.log(o_ext[:, HEAD_DIM:HEAD_DIM + 1])
                lse_tile = jnp.where(lane == h, lse, lse_tile)
    if want_lse:
        lse_ref[...] = lse_tile


def _band(q_arr, kv_arr, bias, *, lead_grid, tq, pb, npv, n_kvh, r_n, qmap, kmap, vmap, omap,
          lmap=None, out_shape, lse_shape=None, misc=None, miscmap=None, prev=None, gate_branch=None,
          scale, name):
    nl = len(lead_grid)
    nq = q_arr.shape[1] // tq
    nblk = tq // pb
    nh = n_kvh * r_n
    qw = nh * HEAD_DIM
    kw = n_kvh * HEAD_DIM

    def rows_cur(fn):
        def im(*g):
            bb, cc = fn(*g)
            return (bb, g[nl], cc)
        return im

    def rows_prev(fn, n):
        def im(*g):
            bb, cc = fn(*g)
            return (bb, jnp.maximum(g[nl] * nblk - npv + n, 0), cc)
        return im

    in_specs = [pl.BlockSpec((None, tq, qw), rows_cur(qmap))]
    args = [q_arr]
    for fn in (kmap, vmap):
        for n in range(npv):
            in_specs.append(pl.BlockSpec((None, pb, kw), rows_prev(fn, n)))
            args.append(kv_arr)
        in_specs.append(pl.BlockSpec((None, tq, kw), rows_cur(fn)))
        args.append(kv_arr)
    in_specs.append(pl.BlockSpec(bias.shape, lambda *g: (0, 0, 0)))
    args.append(bias)
    io_alias = {}
    if gate_branch is not None:
        in_specs.append(pl.BlockSpec((None, tq, LANES), rows_cur(miscmap)))
        args.append(misc)
        in_specs.append(pl.BlockSpec((None, tq, qw), rows_cur(omap)))
        args.append(prev)
        io_alias = {len(args) - 1: 0}
    out_shapes = [out_shape]
    out_specs = [pl.BlockSpec((None, tq, qw), rows_cur(omap))]
    if lse_shape is not None:
        out_shapes.append(lse_shape)
        out_specs.append(pl.BlockSpec((None, tq, LANES), rows_cur(lmap)))
    res = pl.pallas_call(
        functools.partial(_band_kernel, tq=tq, pb=pb, npv=npv, n_kvh=n_kvh, r_n=r_n, scale=scale,
                          gate_branch=gate_branch, want_lse=lse_shape is not None, qi_axis=nl),
        out_shape=tuple(out_shapes),
        grid=tuple(lead_grid) + (nq,),
        in_specs=in_specs,
        out_specs=tuple(out_specs),
        input_output_aliases=io_alias,
        compiler_params=_cparams(("parallel",) * (nl + 1)),
        name=name,
    )(*args)
    return res


def _dil_kernel(q_ref, *rest, dil, n_prev, back, scale):
    rest = list(rest)
    kp = [rest.pop(0) for _ in range(n_prev)]
    kc = rest.pop(0)
    vp = [rest.pop(0) for _ in range(n_prev)]
    vc = rest.pop(0)
    bias_ref, o_ref, lse_ref = rest
    c = pl.program_id(1)
    tile = q_ref.shape[0]
    n = tile // dil
    pl_rows = back // n_prev
    lane = lax.broadcasted_iota(jnp.int32, (n, LANES), 1)
    ones_p = jnp.ones((pl_rows, HEAD_DIM), BF16)
    ones_c = jnp.ones((n, HEAD_DIM), BF16)
    pens = [jnp.where(c - n_prev + pi >= 0, 0.0, NEG_INF) for pi in range(n_prev)]
    for r in range(dil):
        cur = slice(r * n, (r + 1) * n)
        prev = slice((r + 1) * n - pl_rows, (r + 1) * n)
        lse_tile = jnp.zeros((n, LANES), F32)
        for h in range(DIL_HEADS):
            hsl = slice(h * HEAD_DIM, (h + 1) * HEAD_DIM)
            q = (q_ref[cur, hsl].astype(F32) * (scale * LOG2E)).astype(BF16)
            parts, vals = [], []
            for pi in range(n_prev):
                parts.append(_dot_nt(q, kp[pi][prev, hsl]) + bias_ref[h, :, pi * pl_rows:(pi + 1) * pl_rows]
                             + pens[pi])
                vals.append(jnp.concatenate([vp[pi][prev, hsl], ones_p], axis=1))
            parts.append(_dot_nt(q, kc[cur, hsl]) + bias_ref[h, :, back:])
            vals.append(jnp.concatenate([vc[cur, hsl], ones_c], axis=1))
            m = parts[0].max(axis=-1, keepdims=True)
            for sp in parts[1:]:
                m = jnp.maximum(m, sp.max(axis=-1, keepdims=True))
            o_ext = jnp.zeros((n, 2 * HEAD_DIM), F32)
            for sp, vv in zip(parts, vals):
                o_ext = o_ext + _dot(jnp.exp2(sp - m).astype(BF16), vv)
            o_ref[h, pl.ds(r, n, stride=dil), :] = o_ext[:, :HEAD_DIM] * (1.0 / o_ext[:, HEAD_DIM:])
            lse = m * (1.0 / LOG2E) + jnp.log(o_ext[:, HEAD_DIM:HEAD_DIM + 1])
            lse_tile = jnp.where(lane == h, lse, lse_tile)
        lse_ref[pl.ds(r, n, stride=dil), :] = lse_tile


def _dil_attention(z, bias, *, tile, dil, back, col0, name):
    b, s, _ = z.shape
    n = tile // dil
    n_prev = max(1, back // n)
    hw_c = DIL_HEADS * HEAD_DIM

    def prev_map(pi, cb):
        return lambda i, c: (i, jnp.maximum(c - n_prev + pi, 0), cb)

    in_specs = [pl.BlockSpec((None, tile, hw_c), lambda i, c: (i, c, col0))]
    for cb in (col0 + 1, col0 + 2):
        for pi in range(n_prev):
            in_specs.append(pl.BlockSpec((None, tile, hw_c), prev_map(pi, cb)))
        in_specs.append(pl.BlockSpec((None, tile, hw_c), lambda i, c, cb=cb: (i, c, cb)))
    in_specs.append(pl.BlockSpec(bias.shape, lambda i, c: (0, 0, 0)))
    return pl.pallas_call(
        functools.partial(_dil_kernel, dil=dil, n_prev=n_prev, back=back, scale=HEAD_DIM ** -0.5),
        out_shape=(jax.ShapeDtypeStruct((b, DIL_HEADS, s, HEAD_DIM), F32),
                   jax.ShapeDtypeStruct((b, s, LANES), F32)),
        grid=(b, s // tile),
        in_specs=in_specs,
        out_specs=(pl.BlockSpec((None, DIL_HEADS, tile, HEAD_DIM), lambda i, c: (i, 0, c, 0)),
                   pl.BlockSpec((None, tile, LANES), lambda i, c: (i, c, 0))),
        compiler_params=_cparams(("parallel", "parallel")),
        name=name,
    )(*([z] * (3 + 2 * n_prev)), bias)


def _dil_full_kernel(q_ref, k_ref, v_ref, bias_ref, o_ref, lse_ref, *, dil, tile, heads, scale):
    hp = pl.program_id(1)
    s = q_ref.shape[0]
    n = tile // dil
    ls = s // dil
    lane = lax.broadcasted_iota(jnp.int32, (ls, LANES), 1)
    ones = jnp.ones((ls, HEAD_DIM), BF16)
    for r in range(dil):
        rows = [slice(t * tile + r * n, t * tile + (r + 1) * n) for t in range(s // tile)]
        lse_tile = jnp.zeros((ls, LANES), F32)
        for hh in range(heads):
            h = hp * heads + hh
            hsl = slice(hh * HEAD_DIM, (hh + 1) * HEAD_DIM)
            q = jnp.concatenate([q_ref[rs, hsl] for rs in rows], axis=0)
            k = jnp.concatenate([k_ref[rs, hsl] for rs in rows], axis=0)
            v = jnp.concatenate([v_ref[rs, hsl] for rs in rows] , axis=0)
            q = (q.astype(F32) * (scale * LOG2E)).astype(BF16)
            sc = _dot_nt(q, k) + bias_ref[h]
            m = sc.max(axis=-1, keepdims=True)
            o_ext = _dot(jnp.exp2(sc - m).astype(BF16), jnp.concatenate([v, ones], axis=1))
            o_ref[hh, pl.ds(r, ls, stride=dil), :] = o_ext[:, :HEAD_DIM] * (1.0 / o_ext[:, HEAD_DIM:])
            lse = m * (1.0 / LOG2E) + jnp.log(o_ext[:, HEAD_DIM:HEAD_DIM + 1])
            lse_tile = jnp.where(lane == h, lse, lse_tile)
        lse_ref[pl.ds(r, ls, stride=dil), :] = lse_tile


def _dil_full_attention(z, bias, *, tile, dil, col0, heads, name):
    b, s, _ = z.shape
    hw = heads * HEAD_DIM
    ng = DIL_HEADS // heads
    cpb = (DIL_HEADS * HEAD_DIM) // hw

    def col(j):
        return lambda i, g: (i, 0, (col0 + j) * cpb + g)

    return pl.pallas_call(
        functools.partial(_dil_full_kernel, dil=dil, tile=tile, heads=heads, scale=HEAD_DIM ** -0.5),
        out_shape=(jax.ShapeDtypeStruct((b, DIL_HEADS, s, HEAD_DIM), F32),
                   jax.ShapeDtypeStruct((b, ng, s, LANES), F32)),
        grid=(b, ng),
        in_specs=[pl.BlockSpec((None, s, hw), col(0)), pl.BlockSpec((None, s, hw), col(1)),
                  pl.BlockSpec((None, s, hw), col(2)), pl.BlockSpec(bias.shape, lambda i, g: (0, 0, 0))],
        out_specs=(pl.BlockSpec((None, heads, s, HEAD_DIM), lambda i, g: (i, g, 0, 0)),
                   pl.BlockSpec((None, None, s, LANES), lambda i, g: (i, g, 0, 0))),
        compiler_params=_cparams(("parallel", "parallel")),
        name=name,
    )(z, z, z, bias)


def _dist_table(rel_bias, n_dist, dist_scale=1):
    tab = rel_bias[_t5_bucket(jnp.arange(n_dist) * dist_scale)].astype(F32)
    return jnp.concatenate([tab, jnp.full((1, tab.shape[1]), NEG_INF, F32)], axis=0)


def _toeplitz_kernel(w_ref, o_ref):
    t = o_ref.shape[0]
    x = jnp.broadcast_to(w_ref[...], (t, 2 * t))
    o_ref[...] = pltpu.roll(x, 0, 1, stride=1, stride_axis=0)[:, :t]


def _toeplitz_tiles(tab, d0s, t, lo, hi, mult=1):
    masked = tab.shape[0] - 1
    u = np.arange(2 * t)
    i_minus_j = np.where(u < t, -u, 2 * t - u)
    dist = np.asarray(d0s)[:, None] + mult * i_minus_j[None, :]
    idx = np.where((dist >= lo) & (dist <= hi) & (u != t)[None, :], dist, masked)
    w = tab[jnp.asarray(idx, jnp.int32)].transpose(2, 0, 1)
    nh, nc = w.shape[0], w.shape[1]
    tiles = pl.pallas_call(
        _toeplitz_kernel,
        out_shape=jax.ShapeDtypeStruct((nh * nc, t, t), F32),
        grid=(nh * nc,),
        in_specs=[pl.BlockSpec((None, 1, 2 * t), lambda i: (i, 0, 0))],
        out_specs=pl.BlockSpec((None, t, t), lambda i: (i, 0, 0)),
        compiler_params=_cparams(("parallel",)),
        name="toeplitz_tiles",
    )(w.reshape(nh * nc, 1, 2 * t))
    return tiles.reshape(nh, nc, t, t)


def _band_bias(rel_bias, tq, pb, npv, max_back, dist_scale):
    tab = _dist_table(rel_bias, max_back + 1, dist_scale) * LOG2E
    d0s = [(npv - n) * pb for n in range(npv)] + [0]
    tiles = _toeplitz_tiles(tab, d0s, tq, 0, max_back)
    parts = [tiles[:, n, :, :pb] for n in range(npv)] + [tiles[:, npv]]
    return jnp.concatenate(parts, axis=-1)


def _out_ab_kernel(oa_ref, ob_ref, wa_ref, wb_ref, x_ref, g_ref, o_ref):
    y = _dot(oa_ref[...].astype(BF16), wa_ref[...]) + _dot(ob_ref[...], wb_ref[...])
    o_ref[...] = x_ref[...] + g_ref[...] * y


def _out_ab(oa, ob, w, x2d, gate, *, seq, tm):
    m, d = x2d.shape
    ka = oa.shape[1]
    kb = ob.shape[1]
    tpb = seq // tm
    return pl.pallas_call(
        _out_ab_kernel,
        out_shape=jax.ShapeDtypeStruct((m, d), F32),
        grid=(m // tm,),
        in_specs=[pl.BlockSpec((tm, ka), lambda i: (i, 0)),
                  pl.BlockSpec((tm, kb), lambda i: (i, 0)),
                  pl.BlockSpec((ka, d), lambda i: (0, 0)),
                  pl.BlockSpec((kb, d), lambda i: (ka // kb, 0)),
                  pl.BlockSpec((tm, d), lambda i: (i, 0)),
                  pl.BlockSpec((None, 1, d), lambda i: (i // tpb, 0, 0))],
        out_specs=pl.BlockSpec((tm, d), lambda i: (i, 0)),
        compiler_params=_cparams(("parallel",)),
        name="out_proj_ab",
    )(oa, ob, w, w, x2d, gate)


def _out_c_kernel(o0_ref, o1_ref, o2_ref, l0_ref, l1_ref, l2_ref, w_ref, x_ref, g_ref, o_ref, mrg_scr):
    l0, l1, l2 = l0_ref[...], l1_ref[...], l2_ref[...]
    mx = jnp.maximum(jnp.maximum(l0, l1), l2)
    e0, e1, e2 = jnp.exp(l0 - mx), jnp.exp(l1 - mx), jnp.exp(l2 - mx)
    inv = 1.0 / (e0 + e1 + e2)
    w0, w1, w2 = e0 * inv, e1 * inv, e2 * inv
    for h in range(DIL_HEADS):
        sl = slice(h * HEAD_DIM, (h + 1) * HEAD_DIM)
        mg = (w0[:, h:h + 1] * o0_ref[:, sl] + w1[:, h:h + 1] * o1_ref[h]
              + w2[:, h:h + 1] * o2_ref[h])
        mrg_scr[:, sl] = mg.astype(BF16)
    o_ref[...] = x_ref[...] + g_ref[...] * _dot(mrg_scr[...], w_ref[...])


def _out_c(os_, lses, w, x2d, gate, *, seq, tm):
    m, d = x2d.shape
    kc = w.shape[0]
    tpb = seq // tm
    return pl.pallas_call(
        _out_c_kernel,
        out_shape=jax.ShapeDtypeStruct((m, d), F32),
        grid=(m // tm,),
        in_specs=[pl.BlockSpec((tm, kc), lambda i: (i, 0))]
        + [pl.BlockSpec((None, DIL_HEADS, tm, HEAD_DIM), lambda i: (i // tpb, 0, i % tpb, 0))] * 2
        + [pl.BlockSpec((tm, LANES), lambda i: (i, 0))] * 3
        + [pl.BlockSpec((kc, d), lambda i: (0, 0)),
           pl.BlockSpec((tm, d), lambda i: (i, 0)),
           pl.BlockSpec((None, 1, d), lambda i: (i // tpb, 0, 0))],
        out_specs=pl.BlockSpec((tm, d), lambda i: (i, 0)),
        scratch_shapes=[pltpu.VMEM((tm, kc), BF16)],
        compiler_params=_cparams(("parallel",)),
        name="out_proj_c",
    )(*os_, *lses, w, x2d, gate)


def _ffn_kernel(x_ref, g_ref, s_ref, wg_ref, wu_ref, wd_ref, gate_ref, o_ref, h_scr):
    f = pl.program_id(1)
    tm = x_ref.shape[0]

    @pl.when(f == 0)
    def _():
        n0 = tm // PERM_NORM_CHUNKS
        for r in range(PERM_NORM_CHUNKS):
            rows = slice(r * n0, (r + 1) * n0)
            x = x_ref[rows, :]
            ms = jnp.mean(x * x, axis=-1, keepdims=True)
            h_scr[rows, :] = (x * lax.rsqrt(ms + RMS_EPS) * g_ref[...] + s_ref[...]).astype(BF16)
        o_ref[...] = jnp.zeros(o_ref.shape, o_ref.dtype)

    h = h_scr[...]
    a = _dot(h, wg_ref[...].astype(BF16))
    b = _dot(h, wu_ref[...].astype(BF16))
    hid = (a * _sigmoid(a) * b).astype(BF16)
    o_ref[...] += _dot(hid, wd_ref[...].astype(BF16))

    @pl.when(f == pl.num_programs(1) - 1)
    def _():
        o_ref[...] = x_ref[...] + gate_ref[...] * o_ref[...]


def _ffn(x2d, geff, shift, wg, wu, wd, gate, *, seq, tm, tf):
    m, d = x2d.shape
    ff = wg.shape[1]
    tpb = seq // tm
    vec = pl.BlockSpec((None, 1, d), lambda i, f: (i // tpb, 0, 0))
    return pl.pallas_call(
        _ffn_kernel,
        out_shape=jax.ShapeDtypeStruct((m, d), F32),
        grid=(m // tm, ff // tf),
        in_specs=[pl.BlockSpec((tm, d), lambda i, f: (i, 0)), vec, vec,
                  pl.BlockSpec((d, tf), lambda i, f: (0, f)),
                  pl.BlockSpec((d, tf), lambda i, f: (0, f)),
                  pl.BlockSpec((tf, d), lambda i, f: (f, 0)),
                  vec],
        out_specs=pl.BlockSpec((tm, d), lambda i, f: (i, 0)),
        scratch_shapes=[pltpu.VMEM((tm, d), BF16)],
        compiler_params=_cparams(("parallel", "arbitrary")),
        name="ffn_swiglu",
    )(x2d, geff, shift, wg, wu, wd, gate)


def _moe_prep_kernel(x_ref, g_ref, s_ref, wr_ref, h_ref, route_ref):
    x = x_ref[...]
    ms = jnp.mean(x * x, axis=-1, keepdims=True)
    h = x * lax.rsqrt(ms + RMS_EPS) * g_ref[...] + s_ref[...]
    h_ref[...] = h
    logits = jnp.dot(h, wr_ref[...], precision=lax.Precision.HIGHEST, preferred_element_type=F32)
    lane = lax.broadcasted_iota(jnp.int32, logits.shape, 1).astype(F32)
    lg = jnp.where(lane < float(N_EXPERTS), logits, -3e38)
    m1 = jnp.max(lg, axis=-1, keepdims=True)
    i1 = jnp.min(jnp.where(lg == m1, lane, float(LANES)), axis=-1, keepdims=True)
    lg2 = jnp.where(lane == i1, -3e38, lg)
    m2 = jnp.max(lg2, axis=-1, keepdims=True)
    i2 = jnp.min(jnp.where(lg2 == m2, lane, float(LANES)), axis=-1, keepdims=True)
    e = jnp.exp(m2 - m1)
    inv = 1.0 / (1.0 + e)
    route = jnp.where(lane == 0.0, i1, jnp.where(lane == 1.0, i2, jnp.where(lane == 2.0, inv,
                      jnp.where(lane == 3.0, e * inv, 0.0))))
    route_ref[...] = route


def _moe_prep(x2d, geff, shift, w_router_pad, *, seq, tm):
    m, d = x2d.shape
    tpb = seq // tm
    vec = pl.BlockSpec((None, 1, d), lambda i: (i // tpb, 0, 0))
    return pl.pallas_call(
        _moe_prep_kernel,
        out_shape=(jax.ShapeDtypeStruct((m, d), F32), jax.ShapeDtypeStruct((m, LANES), F32)),
        grid=(m // tm,),
        in_specs=[pl.BlockSpec((tm, d), lambda i: (i, 0)), vec, vec,
                  pl.BlockSpec((d, LANES), lambda i: (0, 0))],
        out_specs=(pl.BlockSpec((tm, d), lambda i: (i, 0)), pl.BlockSpec((tm, LANES), lambda i: (i, 0))),
        compiler_params=_cparams(("parallel",)),
        name="moe_prep",
    )(x2d, geff, shift, w_router_pad)


def _row_copy(src_hbm, row, dst_vmem, slot, sem):
    return pltpu.make_async_copy(src_hbm.at[pl.ds(row, 1)], dst_vmem.at[pl.ds(slot, 1)], sem)


def _expert_kernel(te_ref, nu_ref, rows_ref, slot_ref, h_hbm, wg_ref, wu_ref, wd_ref, o_ref, xbuf, xs, sem, *,
                   tm, issue_steps):
    t = pl.program_id(0)
    f = pl.program_id(1)
    nf = pl.num_programs(1)
    n_used = nu_ref[0]
    live = t < n_used
    rows_per_step = tm // issue_steps

    @pl.when((t == 0) & (f == 0))
    def _():
        def start(r, c):
            _row_copy(h_hbm, slot_ref[0, r], xbuf, r, sem).start()
            return c
        lax.fori_loop(0, tm, start, 0, unroll=ROW_DMA_UNROLL)

    @pl.when(live & (f == 0))
    def _():
        def wait(r, c):
            _row_copy(h_hbm, 0, xbuf, r, sem).wait()
            return c
        lax.fori_loop(0, tm, wait, 0, unroll=ROW_DMA_UNROLL)
        xs[...] = xbuf[...].astype(BF16)
        o_ref[...] = jnp.zeros(o_ref.shape, o_ref.dtype)

    def compute(mr):
        x = xs[:mr, :]
        a = _dot(x, wg_ref[...].astype(BF16))
        b = _dot(x, wu_ref[...].astype(BF16))
        hid = (a * _sigmoid(a) * b).astype(BF16)
        o_ref[:mr, :] += _dot(hid, wd_ref[...].astype(BF16))

    prefetch = live & (t + 1 < n_used) & (f >= 1) & (f <= issue_steps)
    quarter = tm // MOE_ROW_SPLITS
    n_quarters = (rows_ref[t] + quarter - 1) // quarter

    for want_prefetch in (True, False):
        for nq in range(1, MOE_ROW_SPLITS + 1):
            cond = live & (prefetch if want_prefetch else jnp.logical_not(prefetch)) & (n_quarters == nq)

            @pl.when(cond)
            def _(want_prefetch=want_prefetch, nq=nq):
                if want_prefetch:
                    base = (f - 1) * rows_per_step
                    for j in range(rows_per_step):
                        _row_copy(h_hbm, slot_ref[t + 1, base + j], xbuf, base + j, sem).start()
                compute(nq * quarter)

    @pl.when(jnp.logical_not(live) & (f == nf - 1))
    def _():
        o_ref[...] = jnp.zeros(o_ref.shape, o_ref.dtype)


def _moe_experts(h2d, slot_tok, tile_e, n_used, tile_rows, wg, wu, wd, *, tm, tf, issue_steps):
    n_tiles = slot_tok.shape[0]
    d = h2d.shape[1]
    ff = wg.shape[2]
    nf = ff // tf
    assert tm % issue_steps == 0 and issue_steps < nf

    def f_blk(t, f, nu):
        return jnp.where(t < nu[0], f, nf - 1)

    return pl.pallas_call(
        functools.partial(_expert_kernel, tm=tm, issue_steps=issue_steps),
        out_shape=jax.ShapeDtypeStruct((n_tiles * tm, d), F32),
        grid_spec=pltpu.PrefetchScalarGridSpec(
            num_scalar_prefetch=4,
            grid=(n_tiles, nf),
            in_specs=[pl.BlockSpec(memory_space=pl.ANY),
                      pl.BlockSpec((None, d, tf), lambda t, f, te, nu, rw, sl: (te[t], 0, f_blk(t, f, nu))),
                      pl.BlockSpec((None, d, tf), lambda t, f, te, nu, rw, sl: (te[t], 0, f_blk(t, f, nu))),
                      pl.BlockSpec((None, tf, d), lambda t, f, te, nu, rw, sl: (te[t], f_blk(t, f, nu), 0))],
            out_specs=pl.BlockSpec((tm, d), lambda t, f, te, nu, rw, sl: (t, 0)),
            scratch_shapes=[pltpu.VMEM((tm, d), F32), pltpu.VMEM((tm, d), BF16),
                            pltpu.SemaphoreType.DMA(())]),
        compiler_params=_cparams(("arbitrary", "arbitrary")),
        name="moe_experts",
    )(tile_e, n_used, tile_rows, slot_tok, h2d, wg, wu, wd)


def _combine_kernel(d1_ref, d2_ref, yb_hbm, route_ref, x_ref, gate_ref, fg_ref, o_ref, b1, b2, sem, *, tmc):
    t = pl.program_id(0)

    def start(r, c):
        _row_copy(yb_hbm, d1_ref[t, r], b1, r, sem).start()
        _row_copy(yb_hbm, d2_ref[t, r], b2, r, sem).start()
        return c

    def wait(r, c):
        _row_copy(yb_hbm, 0, b1, r, sem).wait()
        _row_copy(yb_hbm, 0, b2, r, sem).wait()
        return c

    lax.fori_loop(0, tmc, start, 0, unroll=ROW_DMA_UNROLL)
    lax.fori_loop(0, tmc, wait, 0, unroll=ROW_DMA_UNROLL)
    route = route_ref[...]
    y = route[:, 2:3] * b1[...] + route[:, 3:4] * b2[...]
    xo = x_ref[...] + gate_ref[...] * y
    ms = jnp.mean(xo * xo, axis=-1, keepdims=True)
    o_ref[...] = xo * lax.rsqrt(ms + RMS_EPS) * fg_ref[...]


def _moe_combine(yb, dest1, dest2, route, x2d, gate, final_g, *, seq, tmc):
    m, d = x2d.shape
    tpb = seq // tmc
    return pl.pallas_call(
        functools.partial(_combine_kernel, tmc=tmc),
        out_shape=jax.ShapeDtypeStruct((m, d), F32),
        grid_spec=pltpu.PrefetchScalarGridSpec(
            num_scalar_prefetch=2,
            grid=(m // tmc,),
            in_specs=[pl.BlockSpec(memory_space=pl.ANY),
                      pl.BlockSpec((tmc, LANES), lambda t, a, b: (t, 0)),
                      pl.BlockSpec((tmc, d), lambda t, a, b: (t, 0)),
                      pl.BlockSpec((None, 1, d), lambda t, a, b: (t // tpb, 0, 0)),
                      pl.BlockSpec((1, d), lambda t, a, b: (0, 0))],
            out_specs=pl.BlockSpec((tmc, d), lambda t, a, b: (t, 0)),
            scratch_shapes=[pltpu.VMEM((tmc, d), F32), pltpu.VMEM((tmc, d), F32),
                            pltpu.SemaphoreType.DMA(())]),
        compiler_params=_cparams(("arbitrary",)),
        name="moe_combine",
    )(dest1, dest2, yb, route, x2d, gate, final_g)


def _moe_plan(route, *, tm):
    n = route.shape[0]
    a = n * MOE_TOP_K
    flat_e = route[:, :MOE_TOP_K].astype(jnp.int32).reshape(a)
    onehot = (flat_e[:, None] == jnp.arange(N_EXPERTS, dtype=jnp.int32)[None, :]).astype(jnp.int32)
    csum = jnp.cumsum(onehot, axis=0)
    pos = jnp.sum(onehot * (csum - 1), axis=1)
    counts = csum[-1]
    pcounts = (counts + tm - 1) // tm * tm
    pends = jnp.cumsum(pcounts)
    pstarts = pends - pcounts
    dest = (pstarts[flat_e] + pos).astype(jnp.int32)
    n_tiles = a // tm + N_EXPERTS
    tok = jnp.arange(a, dtype=jnp.int32) // MOE_TOP_K
    slot_tok = jnp.zeros((n_tiles * tm,), jnp.int32).at[dest].set(tok)
    n_used = (pends[-1] // tm).astype(jnp.int32)
    tile_e = jnp.minimum(jnp.searchsorted(pends, jnp.arange(n_tiles, dtype=jnp.int32) * tm, side='right'),
                         N_EXPERTS - 1).astype(jnp.int32)
    tile_e = jnp.where(jnp.arange(n_tiles) < n_used, tile_e, tile_e[jnp.maximum(n_used - 1, 0)])
    tile_rows = jnp.clip(counts[tile_e] - (jnp.arange(n_tiles, dtype=jnp.int32) * tm - pstarts[tile_e]), 0, tm)
    tile_rows = jnp.where(jnp.arange(n_tiles) < n_used, tile_rows, 0).astype(jnp.int32)
    return dest.reshape(n, MOE_TOP_K), slot_tok.reshape(n_tiles, tm), tile_e, n_used.reshape(1), tile_rows


def _tile(n, pref):
    t = min(n, pref)
    assert n % t == 0, (n, pref)
    return t


def kernel(x, c, positions, rel_bias, ada_w, ada_b, mix_norm_g, ffn_norm_g, ab_w_in, ab_w_out, nsa_cmp_pos_k, nsa_cmp_w1_k, nsa_cmp_w2_k, nsa_cmp_pos_v, nsa_cmp_w1_v, nsa_cmp_w2_v, mla_q_norm_g, mla_kv_norm_g, mla_w_uq, mla_w_ukv, ffn_w_gate, ffn_w_up, ffn_w_down, c_w_in, c_w_out, moe_w_router, moe_w_gate, moe_w_up, moe_w_down, final_norm_g):
    b, s, d = x.shape
    assert ada_w.shape[0] == 2 and d == D_MODEL and s % 256 == 0
    m = b * s
    x2d = x.reshape(m, d)
    tm_big = _tile(s, 1024)
    tm_mid = _tile(s, 512)
    t_att = _tile(s, 256)
    t_flash = _tile(s, 512)

    mod = _ada_mod(c, ada_w, ada_b)
    mods = mod.reshape(2, b, 6, 1, d)

    def layer_mod(i):
        sh_m, sc_m, g_m, sh_f, sc_f, g_f = (mods[i, :, j] for j in range(6))
        return (mix_norm_g[i][None, None, :] * (1.0 + sc_m), sh_m, g_m,
                ffn_norm_g[i][None, None, :] * (1.0 + sc_f), sh_f, g_f)

    geff_m, sh_m, g_m, geff_f, sh_f, g_f = layer_mod(0)
    w0 = ab_w_in[0]
    c_q, c_kv, c_g, c_ql, c_kvl = np.cumsum([NSA_HEADS * HEAD_DIM, 6 * NSA_KV_HEADS * HEAD_DIM,
                                             3 * NSA_HEADS, Q_LORA, KV_LORA]).tolist()
    zpad = lambda n: jnp.zeros((d, n), w0.dtype)
    w_in0 = jnp.concatenate([w0[:, :c_kv], w0[:, c_ql:c_kvl], w0[:, c_kvl:], w0[:, c_kv:c_g],
                             zpad(LANES - QK_ROPE - 3 * NSA_HEADS), w0[:, c_g:c_ql], zpad(LANES)],
                            axis=1).astype(BF16)
    assert w_in0.shape[1] == Z0_W
    z0 = _nmm(x2d, geff_m, sh_m, w_in0, seq=s, tm=tm_big, tn=Z0_W // 3, norm_cols=d, x_block=0,
              out_dtype=BF16, name="proj_in_ab")
    z0_3d = z0.reshape(b, s, Z0_W)

    inv_freq = ROPE_THETA ** (-jnp.arange(0, QK_ROPE, 2, dtype=F32) / QK_ROPE)
    ang = positions.astype(F32)[..., None] * inv_freq
    cos, sin = jnp.cos(ang).reshape(m, -1), jnp.sin(ang).reshape(m, -1)
    hr = QK_ROPE // 2
    zr = lambda n: jnp.zeros((m, n), F32)
    rope_tabs = (jnp.concatenate([cos, cos, zr(LANES - 2 * hr)], axis=1),
                 jnp.concatenate([-sin, zr(LANES - hr)], axis=1),
                 jnp.concatenate([zr(hr), sin, zr(LANES - 2 * hr)], axis=1))
    hw = MXU_DIM
    wq3 = mla_w_uq[0].reshape(Q_LORA, MLA_HEADS, QK_NOPE + QK_ROPE)
    wq = jnp.concatenate([wq3, jnp.zeros((Q_LORA, MLA_HEADS, hw - QK_NOPE - QK_ROPE), F32)], axis=2)
    wq = jnp.concatenate([wq.reshape(Q_LORA, MLA_HEADS * hw), jnp.zeros((LANES, MLA_HEADS * hw), F32)],
                         axis=0).astype(BF16)
    wkv3 = mla_w_ukv[0].reshape(KV_LORA, MLA_HEADS, QK_NOPE + V_DIM)
    wk_top = jnp.concatenate([wkv3[:, :, :QK_NOPE], jnp.zeros((KV_LORA, MLA_HEADS, hw - QK_NOPE), F32)],
                             axis=2).reshape(KV_LORA, MLA_HEADS * hw)
    pe_pass = jnp.zeros((LANES, hw), F32).at[jnp.arange(QK_ROPE), QK_NOPE + jnp.arange(QK_ROPE)].set(1.0)
    wk_bot = jnp.tile(pe_pass, (1, MLA_HEADS))
    wv = jnp.concatenate([wkv3[:, :, QK_NOPE:].reshape(KV_LORA, MLA_HEADS * V_DIM),
                          jnp.zeros((LANES, MLA_HEADS * V_DIM), F32)], axis=0)
    wkv = jnp.concatenate([jnp.concatenate([wk_top, wk_bot], axis=0), wv], axis=1).astype(BF16)
    ones_b = lambda g: jnp.broadcast_to(g[None, None, :], (b, 1, g.shape[0]))
    zeros_lat = jnp.zeros((b, 1, Q_LORA), F32)
    q_mla = _nmm(z0, ones_b(mla_q_norm_g[0]), zeros_lat, wq, seq=s, tm=tm_big, tn=1024, norm_cols=Q_LORA,
                 x_block=Z0_QLAT // LAT_BLOCK, out_dtype=BF16, rope=rope_tabs, rope_tiles=2,
                 name="mla_q_up")
    kv_mla = _nmm(z0, ones_b(mla_kv_norm_g[0]), zeros_lat, wkv, seq=s, tm=tm_big, tn=1024,
                  norm_cols=KV_LORA, x_block=Z0_KVLAT // LAT_BLOCK, out_dtype=BF16, rope=rope_tabs,
                  rope_tiles=2, name="mla_kv_up")
    o_mla = _flash(q_mla.reshape(b, s, -1), kv_mla.reshape(b, s, -1), kv_mla.reshape(b, s, -1),
                   t=t_flash, n_kvh=MLA_HEADS, r_n=1, dk=hw, q_blk=0, k_blk=0,
                   v_blk=(MLA_HEADS * hw) // (MLA_HEADS * V_DIM), scale=(QK_NOPE + QK_ROPE) ** -0.5,
                   out_dtype=BF16, name="mla_attn")

    n16 = s // CMP_STRIDE
    cmp_cols = z0_3d[:, :, Z0_KV:Z0_KV + 2 * NSA_KV_HEADS * HEAD_DIM]
    x16 = cmp_cols.reshape(b, n16, CMP_STRIDE, 2 * NSA_KV_HEADS, HEAD_DIM).transpose(0, 3, 1, 2, 4)
    x16 = x16.reshape(b, 2 * NSA_KV_HEADS, n16, CMP_STRIDE * HEAD_DIM)
    pos_kv = jnp.stack([nsa_cmp_pos_k[0], nsa_cmp_pos_v[0]]).reshape(2, 1, CMP_LEN * HEAD_DIM).astype(BF16)
    w1_kv = jnp.stack([nsa_cmp_w1_k[0], nsa_cmp_w1_v[0]]).astype(BF16)
    w2_kv = jnp.stack([nsa_cmp_w2_k[0], nsa_cmp_w2_v[0]]).astype(BF16)
    kvc = _nsa_compress(x16, pos_kv, w1_kv, w2_kv)

    n_cmp = (s - CMP_LEN) // CMP_STRIDE + 1
    n_slc = s // SLC_LEN
    ratio, span = SLC_LEN // CMP_STRIDE, CMP_LEN // CMP_STRIDE
    cm = np.zeros((n16, n_slc), np.float32)
    for j in range(n_slc):
        for mm in range(ratio):
            for nn in range(span):
                i = ratio * j + mm - nn
                if 0 <= i < n_cmp:
                    cm[i, j] += 1.0
    tab_s = _dist_table(rel_bias, s)
    bias_c = _toeplitz_tiles(tab_s, [r - (CMP_LEN - 1) for r in range(CMP_STRIDE)], n16, 0, s - 1,
                             mult=CMP_STRIDE)
    bias_c = bias_c.transpose(0, 2, 1, 3).reshape(NSA_HEADS, s, n16)
    o_nsa, sel = _nsa_cmp_attention(z0_3d, kvc, bias_c, jnp.asarray(cm), tq=t_flash)

    nq = s // t_flash
    nd = min(nq, -(-(T5_MAX_DIST + t_flash - 1) // t_flash) + 1)
    bias_d = _toeplitz_tiles(_dist_table(rel_bias, nd * t_flash) * LOG2E, [dd * t_flash for dd in range(nd)],
                             t_flash, 0, nd * t_flash - 1)
    kvw = NSA_KV_HEADS * HEAD_DIM
    o_nsa = _flash(z0_3d, z0_3d, z0_3d, t=t_flash, n_kvh=NSA_KV_HEADS, r_n=NSA_GROUP, dk=HEAD_DIM,
                   q_blk=0, k_blk=(Z0_KV + 2 * kvw) // kvw, v_blk=(Z0_KV + 3 * kvw) // kvw,
                   scale=HEAD_DIM ** -0.5, out_dtype=F32, bias=bias_d, sel=sel, misc=z0_3d,
                   misc_blk=Z0_MISC // LANES, prev=o_nsa, gate_branch=1, name="nsa_slc_attn")

    npv_w = -(-(WIN - 1) // t_att)
    bias_w = _band_bias(rel_bias, t_att, t_att, npv_w, WIN - 1, 1)
    (o_nsa,) = _band(z0_3d, z0_3d, bias_w, lead_grid=(b,), tq=t_att, pb=t_att, npv=npv_w,
                     n_kvh=NSA_KV_HEADS, r_n=NSA_GROUP,
                     qmap=lambda i, j: (i, 0), kmap=lambda i, j: (i, (Z0_KV + 4 * kvw) // kvw),
                     vmap=lambda i, j: (i, (Z0_KV + 5 * kvw) // kvw), omap=lambda i, j: (i, 0),
                     out_shape=jax.ShapeDtypeStruct((b, s, NSA_HEADS * HEAD_DIM), F32),
                     misc=z0_3d, miscmap=lambda i, j: (i, Z0_MISC // LANES), prev=o_nsa, gate_branch=2,
                     scale=HEAD_DIM ** -0.5, name="nsa_win_attn")

    x2d = _out_ab(o_nsa.reshape(m, -1), o_mla.reshape(m, -1), ab_w_out[0].astype(BF16), x2d, g_m,
                  seq=s, tm=tm_mid)

    x2d = _ffn(x2d, geff_f, sh_f, ffn_w_gate[0], ffn_w_up[0], ffn_w_down[0], g_f, seq=s, tm=tm_big, tf=256)

    geff_m, sh_m, g_m, geff_f, sh_f, g_f = layer_mod(1)
    cw = c_w_in.shape[2]
    dils = tuple(dil for _, dil in DIL_PATTERNS)
    z1 = _nmm_perm(x2d, geff_m, sh_m, c_w_in[0].astype(BF16), seq=s, tm=tm_big, tn=1024, dils=dils,
                   out_dtype=BF16, name="proj_in_c")
    hw_c = DIL_HEADS * HEAD_DIM
    z1_3d = z1.reshape(b, s, cw)
    os_, lses = [], []
    for gidx, (win, dil) in enumerate(DIL_PATTERNS):
        max_back = win // dil
        if dil == 1:
            tq = _tile(s, 256)
            pb = min(tq, 128)
            npv = -(-max_back // pb)
            bias_g = _band_bias(rel_bias, tq, pb, npv, max_back, dil)
            og, lg = _band(z1_3d, z1_3d, bias_g, lead_grid=(b,), tq=tq, pb=pb, npv=npv, n_kvh=DIL_HEADS,
                           r_n=1, qmap=lambda i, j, c0=gidx * 3: (i, c0),
                           kmap=lambda i, j, c0=gidx * 3: (i, c0 + 1),
                           vmap=lambda i, j, c0=gidx * 3: (i, c0 + 2),
                           omap=lambda i, j: (i, 0), lmap=lambda i, j: (i, 0),
                           out_shape=jax.ShapeDtypeStruct((b, s, hw_c), F32),
                           lse_shape=jax.ShapeDtypeStruct((b, s, LANES), F32),
                           scale=HEAD_DIM ** -0.5, name=f"dil_attn_{gidx}")
        elif s // dil <= 256:
            ls = s // dil
            bias_g = _toeplitz_tiles(_dist_table(rel_bias, max_back + 1, dil) * LOG2E, [0], ls, 0,
                                     max_back)[:, 0]
            og, lg = _dil_full_attention(z1_3d, bias_g, tile=tm_big, dil=dil, col0=gidx * 3, heads=2,
                                         name=f"dil_attn_{gidx}")
            lg = lg.sum(axis=1)
        else:
            n_res = tm_big // dil
            n_prev = max(1, max_back // n_res)
            bias_g = _band_bias(rel_bias, n_res, max_back // n_prev, n_prev, max_back, dil)
            og, lg = _dil_attention(z1_3d, bias_g, tile=tm_big, dil=dil, back=max_back, col0=gidx * 3,
                                    name=f"dil_attn_{gidx}")
        os_.append(og.reshape(m, hw_c) if dil == 1 else og)
        lses.append(lg.reshape(m, LANES))
    x2d = _out_c(os_, lses, c_w_out[0].astype(BF16), x2d, g_m, seq=s, tm=tm_mid)

    wr = jnp.concatenate([moe_w_router[0], jnp.zeros((d, LANES - N_EXPERTS), F32)], axis=1)
    h_moe, route = _moe_prep(x2d, geff_f, sh_f, wr, seq=s, tm=tm_mid)
    tm_e = 1024
    dest, slot_tok, tile_e, n_used, tile_rows = _moe_plan(route, tm=tm_e)
    yb = _moe_experts(h_moe, slot_tok, tile_e, n_used, tile_rows, moe_w_gate[0], moe_w_up[0], moe_w_down[0],
                      tm=tm_e, tf=256, issue_steps=16)
    tmc = _tile(s, 256)
    out = _moe_combine(yb, dest[:, 0].reshape(m // tmc, tmc), dest[:, 1].reshape(m // tmc, tmc), route,
                       x2d, g_f, final_norm_g.reshape(1, d), seq=s, tmc=tmc)
    return out.reshape(b, s, d)
```

```python
import functools
import math

import numpy as np
import jax
import jax.numpy as jnp
from jax import lax
from jax.experimental import pallas as pl
from jax.experimental.pallas import tpu as pltpu

F32 = jnp.float32
BF16 = jnp.bfloat16

D_MODEL = 2048
HEAD_DIM = 128
NEG_INF = -1e30
LOG2E = 1.4426950408889634
RMS_EPS = 1e-6
NUM_BUCKETS = 32
T5_MAX_DIST = 2048
NSA_HEADS = 8
NSA_KV_HEADS = 2
NSA_GROUP = NSA_HEADS // NSA_KV_HEADS
CMP_LEN = 32
CMP_STRIDE = 16
SLC_LEN = 64
N_SEL = 16
WIN = 512
FORCE_SCORE = 1e9
MLA_HEADS = 8
Q_LORA = 512
KV_LORA = 512
QK_NOPE = 128
QK_ROPE = 64
V_DIM = 128
ROPE_THETA = 10000.0
DIL_PATTERNS = ((128, 1), (512, 4), (2048, 16))
DIL_HEADS = 8
D_FF = 5632
N_EXPERTS = 8
MOE_TOP_K = 2
D_FF_EXPERT = 7168

LANES = 128
MXU_DIM = 256
VMEM_LIMIT_BYTES = 56 * 1024 * 1024
ROW_DMA_UNROLL = 8
MOE_ROW_SPLITS = 4
PERM_NORM_CHUNKS = 4

Z0_Q = 0
Z0_KV = Z0_Q + NSA_HEADS * HEAD_DIM
Z0_KVLAT = Z0_KV + 6 * NSA_KV_HEADS * HEAD_DIM
Z0_MISC = Z0_KVLAT + KV_LORA
Z0_QLAT = Z0_MISC + LANES
Z0_W = Z0_QLAT + Q_LORA + LANES
LAT_BLOCK = KV_LORA + LANES
GATE_LANE0 = QK_ROPE


def _cparams(sem, vmem=VMEM_LIMIT_BYTES):
    return pltpu.CompilerParams(dimension_semantics=sem, vmem_limit_bytes=vmem)


def _dot(a, b):
    return jnp.dot(a, b, preferred_element_type=F32)


def _dot_nt(a, b):
    return lax.dot_general(a, b, (((1,), (1,)), ((), ())), preferred_element_type=F32)


def _sigmoid(x):
    return 1.0 / (1.0 + jnp.exp(-x))


def _t5_bucket(dist):
    n = jnp.maximum(dist, 0)
    max_exact = NUM_BUCKETS // 2
    nf = jnp.maximum(n, 1).astype(F32)
    large = max_exact + (jnp.log(nf / max_exact) / math.log(T5_MAX_DIST / max_exact)
                         * (NUM_BUCKETS - max_exact)).astype(jnp.int32)
    large = jnp.minimum(large, NUM_BUCKETS - 1)
    return jnp.where(n < max_exact, n, large)


def _ada_kernel(c_ref, w_ref, b_ref, o_ref):
    c = c_ref[...]
    cs = c * _sigmoid(c)
    o_ref[...] = _dot(cs.astype(BF16), w_ref[...].astype(BF16)) + b_ref[...]


def _ada_mod(c, ada_w, ada_b):
    depth, d, n = ada_w.shape
    b = c.shape[0]
    bp = 8
    cpad = jnp.zeros((bp, d), F32).at[:b].set(c)
    tn = 1024
    out = pl.pallas_call(
        _ada_kernel,
        out_shape=jax.ShapeDtypeStruct((depth, bp, n), F32),
        grid=(depth, n // tn),
        in_specs=[pl.BlockSpec((bp, d), lambda l, j: (0, 0)),
                  pl.BlockSpec((None, d, tn), lambda l, j: (l, 0, j)),
                  pl.BlockSpec((None, 1, tn), lambda l, j: (l, 0, j))],
        out_specs=pl.BlockSpec((None, bp, tn), lambda l, j: (l, 0, j)),
        compiler_params=_cparams(("parallel", "parallel")),
        name="ada_mod",
    )(cpad, ada_w, ada_b.reshape(depth, 1, n))
    return out[:, :b]


def _nmm_kernel(x_ref, g_ref, s_ref, w_ref, *rest, norm_cols, rope_tiles, tn):
    if rope_tiles:
        c_ref, s1_ref, s2_ref, o_ref, h_scr = rest
    else:
        o_ref, h_scr = rest
    j = pl.program_id(1)

    @pl.when(j == 0)
    def _():
        x = x_ref[...].astype(F32)
        xn = x[:, :norm_cols]
        ms = jnp.mean(xn * xn, axis=-1, keepdims=True)
        hn = xn * lax.rsqrt(ms + RMS_EPS) * g_ref[...] + s_ref[...]
        h_scr[:, :norm_cols] = hn.astype(BF16)
        if norm_cols < x.shape[1]:
            h_scr[:, norm_cols:] = x[:, norm_cols:].astype(BF16)

    acc = _dot(h_scr[...], w_ref[...])
    o_ref[...] = acc.astype(o_ref.dtype)
    if rope_tiles:
        @pl.when(j < rope_tiles)
        def _():
            for hh in range(tn // MXU_DIM):
                lo = hh * MXU_DIM + LANES
                y = acc[:, lo:lo + LANES]
                y2 = (y * c_ref[...] + pltpu.roll(y, LANES - 32, 1) * s1_ref[...]
                      + pltpu.roll(y, 32, 1) * s2_ref[...])
                o_ref[:, lo:lo + LANES] = y2.astype(o_ref.dtype)


def _nmm(x2d, geff, shift, w, *, seq, tm, tn, norm_cols, x_block, out_dtype, rope=None,
         rope_tiles=0, name):
    m = x2d.shape[0]
    k, n = w.shape
    tpb = seq // tm
    in_specs = [pl.BlockSpec((tm, k), lambda i, j: (i, x_block)),
                pl.BlockSpec((None, 1, norm_cols), lambda i, j: (i // tpb, 0, 0)),
                pl.BlockSpec((None, 1, norm_cols), lambda i, j: (i // tpb, 0, 0)),
                pl.BlockSpec((k, tn), lambda i, j: (0, j),
                             **({"pipeline_mode": pl.Buffered(1)} if tn == n else {}))]
    args = [x2d, geff, shift, w]
    if rope_tiles:
        for t in rope:
            in_specs.append(pl.BlockSpec((tm, LANES), lambda i, j: (i, 0)))
            args.append(t)
    return pl.pallas_call(
        functools.partial(_nmm_kernel, norm_cols=norm_cols, rope_tiles=rope_tiles, tn=tn),
        out_shape=jax.ShapeDtypeStruct((m, n), out_dtype),
        grid=(m // tm, n // tn),
        in_specs=in_specs,
        out_specs=pl.BlockSpec((tm, tn), lambda i, j: (i, j)),
        scratch_shapes=[pltpu.VMEM((tm, k), BF16)],
        compiler_params=_cparams(("parallel", "arbitrary")),
        name=name,
    )(*args)


def _nmm_perm_kernel(x_ref, g_ref, s_ref, w_ref, o_ref, h_scr, xcol_scr, *, dils, tiles_per_group):
    j = pl.program_id(1)
    tm, k = x_ref.shape
    ncb = k // LANES

    @pl.when(j == 0)
    def _():
        n0 = tm // PERM_NORM_CHUNKS
        for r in range(PERM_NORM_CHUNKS):
            rows = slice(r * n0, (r + 1) * n0)
            x = x_ref[rows, :]
            ms = jnp.mean(x * x, axis=-1, keepdims=True)
            hn = x * lax.rsqrt(ms + RMS_EPS) * g_ref[...] + s_ref[...]
            for gi, dil in enumerate(dils):
                if dil == 1:
                    h_scr[gi, rows, :] = hn.astype(BF16)
            for cb in range(ncb):
                xcol_scr[cb, rows, :] = hn[:, cb * LANES:(cb + 1) * LANES]
        for gi, dil in enumerate(dils):
            if dil == 1:
                continue
            n = tm // dil
            for r in range(dil):
                for cb in range(ncb):
                    h_scr[gi, r * n:(r + 1) * n, cb * LANES:(cb + 1) * LANES] = (
                        xcol_scr[cb, pl.ds(r, n, stride=dil), :].astype(BF16))

    o_ref[...] = _dot(h_scr[j // tiles_per_group], w_ref[...]).astype(o_ref.dtype)


def _nmm_perm(x2d, geff, shift, w, *, seq, tm, tn, dils, out_dtype, name):
    m, k = x2d.shape
    n = w.shape[1]
    tpb = seq // tm
    tiles_per_group = n // len(dils) // tn
    return pl.pallas_call(
        functools.partial(_nmm_perm_kernel, dils=dils, tiles_per_group=tiles_per_group),
        out_shape=jax.ShapeDtypeStruct((m, n), out_dtype),
        grid=(m // tm, n // tn),
        in_specs=[pl.BlockSpec((tm, k), lambda i, j: (i, 0), pipeline_mode=pl.Buffered(1)),
                  pl.BlockSpec((None, 1, k), lambda i, j: (i // tpb, 0, 0)),
                  pl.BlockSpec((None, 1, k), lambda i, j: (i // tpb, 0, 0)),
                  pl.BlockSpec((k, tn), lambda i, j: (0, j))],
        out_specs=pl.BlockSpec((tm, tn), lambda i, j: (i, j)),
        scratch_shapes=[pltpu.VMEM((len(dils), tm, k), BF16), pltpu.VMEM((k // LANES, tm, LANES), F32)],
        compiler_params=_cparams(("parallel", "arbitrary")),
        name=name,
    )(x2d, geff, shift, w)


def _cmp_kernel(x_ref, pos_ref, w1_ref, w2_ref, o_ref):
    half = w1_ref.shape[0] // 2
    x = x_ref[...]
    n16 = x.shape[0]
    a = _dot(x, w1_ref[:half, :])
    b = _dot(x, w1_ref[half:, :])
    c = _dot(pos_ref[...], w1_ref[...])
    hid = a + pltpu.roll(b, n16 - 1, 0) + c
    hid = jax.nn.gelu(hid, approximate=True)
    o_ref[...] = _dot(hid.astype(BF16), w2_ref[...]).astype(o_ref.dtype)


def _nsa_compress(x16, pos, w1, w2):
    b, nkv, n16, kk = x16.shape
    g = NSA_KV_HEADS
    return pl.pallas_call(
        _cmp_kernel,
        out_shape=jax.ShapeDtypeStruct((b, nkv, n16, HEAD_DIM), BF16),
        grid=(b, nkv),
        in_specs=[pl.BlockSpec((None, None, n16, kk), lambda i, j: (i, j, 0, 0)),
                  pl.BlockSpec((None, 1, 2 * kk), lambda i, j: (j // g, 0, 0)),
                  pl.BlockSpec((None, 2 * kk, HEAD_DIM), lambda i, j: (j // g, 0, 0)),
                  pl.BlockSpec((None, HEAD_DIM, HEAD_DIM), lambda i, j: (j // g, 0, 0))],
        out_specs=pl.BlockSpec((None, None, n16, HEAD_DIM), lambda i, j: (i, j, 0, 0)),
        compiler_params=_cparams(("parallel", "parallel")),
        name="nsa_compress",
    )(x16, pos, w1, w2)


def _cmpattn_kernel(q_ref, kv_ref, bias_ref, cmat_ref, misc_ref, o_ref, sel_ref, *, tq, n_slc,
                    n_top, scale):
    qi = pl.program_id(1)
    g_n, r_n = NSA_KV_HEADS, NSA_GROUP
    ncp = kv_ref.shape[1]
    trow = qi * tq + lax.broadcasted_iota(jnp.int32, (tq, 1), 0)
    has_c = trow >= (CMP_LEN - 1)
    blk_t = lax.shift_right_logical(trow, int(math.log2(SLC_LEN))).astype(F32)
    jb = lax.broadcasted_iota(jnp.int32, (tq, n_slc), 1).astype(F32)
    forced = (jb == 0.0) | (jb == blk_t) | (jb == blk_t - 1.0)
    valid = jb <= blk_t
    misc = misc_ref[...].astype(F32)
    for g in range(g_n):
        kc = kv_ref[g]
        vc = kv_ref[g_n + g]
        imp = jnp.zeros((tq, ncp), F32)
        for r in range(r_n):
            h = g * r_n + r
            q = q_ref[:, h * HEAD_DIM:(h + 1) * HEAD_DIM]
            s = _dot_nt(q, kc) * scale + bias_ref[h]
            m = jnp.max(s, axis=-1, keepdims=True)
            e = jnp.exp(s - m)
            p = e * (1.0 / jnp.sum(e, axis=-1, keepdims=True))
            p = jnp.where(has_c, p, 0.0)
            imp = imp + p
            o = _dot(p.astype(BF16), vc)
            c0 = GATE_LANE0 + h * 3
            gate = _sigmoid(misc[:, c0:c0 + 1])
            o_ref[:, h * HEAD_DIM:(h + 1) * HEAD_DIM] = gate * o
        imp_s = jnp.dot(imp, cmat_ref[...], precision=lax.Precision.HIGHEST,
                        preferred_element_type=F32)
        score = jnp.where(forced, FORCE_SCORE, jnp.where(valid, imp_s, -1.0))
        sel = jnp.zeros((tq, n_slc), F32)
        for _ in range(n_top):
            mx = jnp.max(score, axis=-1, keepdims=True)
            first = jnp.min(jnp.where(score == mx, jb, float(n_slc)), axis=-1, keepdims=True)
            hit = jb == first
            sel = jnp.where(hit, jnp.where(mx > -0.5, 1.0, 0.0), sel)
            score = jnp.where(hit, -3e38, score)
        sel_ref[g] = sel.astype(sel_ref.dtype)


def _nsa_cmp_attention(z0, kvc, bias_c, cmat, *, tq):
    b, s, _ = z0.shape
    ncp = kvc.shape[2]
    n_slc = s // SLC_LEN
    n_top = min(N_SEL, n_slc)
    qw = NSA_HEADS * HEAD_DIM
    return pl.pallas_call(
        functools.partial(_cmpattn_kernel, tq=tq, n_slc=n_slc, n_top=n_top, scale=HEAD_DIM ** -0.5),
        out_shape=(jax.ShapeDtypeStruct((b, s, qw), F32),
                   jax.ShapeDtypeStruct((b, NSA_KV_HEADS, s, n_slc), BF16)),
        grid=(b, s // tq),
        in_specs=[pl.BlockSpec((None, tq, qw), lambda i, j: (i, j, Z0_Q // qw)),
                  pl.BlockSpec((None, 2 * NSA_KV_HEADS, ncp, HEAD_DIM), lambda i, j: (i, 0, 0, 0)),
                  pl.BlockSpec((NSA_HEADS, tq, ncp), lambda i, j: (0, j, 0)),
                  pl.BlockSpec((ncp, n_slc), lambda i, j: (0, 0)),
                  pl.BlockSpec((None, tq, LANES), lambda i, j: (i, j, Z0_MISC // LANES))],
        out_specs=(pl.BlockSpec((None, tq, qw), lambda i, j: (i, j, 0)),
                   pl.BlockSpec((None, NSA_KV_HEADS, tq, n_slc), lambda i, j: (i, 0, j, 0))),
        compiler_params=_cparams(("parallel", "parallel")),
        name="nsa_cmp_attn",
    )(z0, kvc, bias_c, cmat, z0)


def _flash_kernel(qi_ref, ki_ref, q_ref, k_ref, v_ref, *rest, t, n_kvh, r_n, dk, scale, has_bias,
                  has_sel, gate_branch):
    rest = list(rest)
    bias_ref = rest.pop(0) if has_bias else None
    sel_ref = rest.pop(0) if has_sel else None
    if gate_branch is not None:
        misc_ref = rest.pop(0)
        prev_ref = rest.pop(0)
    o_ref, qs_scr, m_scr, acc_scr = rest
    pidx = pl.program_id(1)
    qi = qi_ref[pidx]
    ki = ki_ref[pidx]
    reps = t // LANES

    @pl.when(ki == 0)
    def _():
        qs_scr[...] = (q_ref[...].astype(F32) * (scale * LOG2E)).astype(BF16)
        m_scr[...] = jnp.full(m_scr.shape, NEG_INF, F32)
        acc_scr[...] = jnp.zeros(acc_scr.shape, F32)

    def step(diag):
        if has_sel:
            per = t // SLC_LEN
            erow = lax.broadcasted_iota(jnp.int32, (sel_ref.shape[2], t), 0)
            ecol = lax.broadcasted_iota(jnp.int32, (sel_ref.shape[2], t), 1)
            expand = jnp.where(erow == ki * per + lax.shift_right_logical(ecol, int(math.log2(SLC_LEN))),
                               1.0, 0.0).astype(BF16)
        if diag:
            causal = (lax.broadcasted_iota(jnp.int32, (t, t), 0)
                      >= lax.broadcasted_iota(jnp.int32, (t, t), 1))
        ones = jnp.ones((t, HEAD_DIM), BF16)
        for kh in range(n_kvh):
            k = k_ref[:, kh * dk:(kh + 1) * dk]
            v_ext = jnp.concatenate([v_ref[:, kh * HEAD_DIM:(kh + 1) * HEAD_DIM], ones], axis=1)
            madd = None
            if has_sel:
                madd = jnp.where(_dot(sel_ref[kh], expand) > 0.5, 0.0, NEG_INF)
                if diag:
                    madd = jnp.where(causal, madd, NEG_INF)
            elif diag:
                madd = jnp.where(causal, 0.0, NEG_INF)
            for r in range(r_n):
                h = kh * r_n + r
                s = _dot_nt(qs_scr[:, h * dk:(h + 1) * dk], k)
                if has_bias:
                    s = s + bias_ref[h]
                if madd is not None:
                    s = s + madd
                m_prev = m_scr[h]
                m_new = jnp.maximum(m_prev, jnp.max(s, axis=-1, keepdims=True))
                alpha = jnp.exp2(m_prev - m_new)
                p = jnp.exp2(s - jnp.concatenate([m_new] * reps, axis=1))
                acc_scr[h] = (jnp.concatenate([alpha, alpha], axis=1) * acc_scr[h]
                              + _dot(p.astype(BF16), v_ext))
                m_scr[h] = m_new

    @pl.when(ki < qi)
    def _():
        step(False)

    @pl.when(ki == qi)
    def _():
        step(True)
        if gate_branch is not None:
            misc = misc_ref[...].astype(F32)
        for h in range(n_kvh * r_n):
            a = acc_scr[h]
            o = a[:, :HEAD_DIM] * (1.0 / a[:, HEAD_DIM:])
            sl = slice(h * HEAD_DIM, (h + 1) * HEAD_DIM)
            if gate_branch is not None:
                c0 = GATE_LANE0 + h * 3 + gate_branch
                o = prev_ref[:, sl] + _sigmoid(misc[:, c0:c0 + 1]) * o
            o_ref[:, sl] = o.astype(o_ref.dtype)


def _tri_pairs(nq):
    qi = np.concatenate([np.full(i + 1, i, np.int32) for i in range(nq)])
    ki = np.concatenate([np.arange(i + 1, dtype=np.int32) for i in range(nq)])
    return jnp.asarray(qi), jnp.asarray(ki)


def _flash(q_arr, k_arr, v_arr, *, t, n_kvh, r_n, dk, q_blk, k_blk, v_blk, scale, out_dtype,
           bias=None, sel=None, misc=None, misc_blk=0, prev=None, gate_branch=None, name):
    b, s, _ = q_arr.shape
    nq = s // t
    qi_a, ki_a = _tri_pairs(nq)
    nh = n_kvh * r_n
    ow = nh * HEAD_DIM
    in_specs = [pl.BlockSpec((None, t, nh * dk), lambda i, p, qa, ka: (i, qa[p], q_blk)),
                pl.BlockSpec((None, t, n_kvh * dk), lambda i, p, qa, ka: (i, ka[p], k_blk)),
                pl.BlockSpec((None, t, n_kvh * HEAD_DIM), lambda i, p, qa, ka: (i, ka[p], v_blk))]
    args = [q_arr, k_arr, v_arr]
    if bias is not None:
        nd = bias.shape[1]
        in_specs.append(pl.BlockSpec((nh, None, t, t),
                                     lambda i, p, qa, ka: (0, jnp.minimum(qa[p] - ka[p], nd - 1), 0, 0)))
        args.append(bias)
    if sel is not None:
        n_slc = sel.shape[-1]
        in_specs.append(pl.BlockSpec((None, n_kvh, t, n_slc), lambda i, p, qa, ka: (i, 0, qa[p], 0)))
        args.append(sel)
    io_alias = {}
    if gate_branch is not None:
        in_specs.append(pl.BlockSpec((None, t, LANES), lambda i, p, qa, ka: (i, qa[p], misc_blk)))
        args.append(misc)
        in_specs.append(pl.BlockSpec((None, t, ow), lambda i, p, qa, ka: (i, qa[p], 0)))
        args.append(prev)
        io_alias = {2 + len(args) - 1: 0}
    return pl.pallas_call(
        functools.partial(_flash_kernel, t=t, n_kvh=n_kvh, r_n=r_n, dk=dk, scale=scale,
                          has_bias=bias is not None, has_sel=sel is not None, gate_branch=gate_branch),
        out_shape=jax.ShapeDtypeStruct((b, s, ow), out_dtype),
        grid_spec=pltpu.PrefetchScalarGridSpec(
            num_scalar_prefetch=2,
            grid=(b, int(qi_a.shape[0])),
            in_specs=in_specs,
            out_specs=pl.BlockSpec((None, t, ow), lambda i, p, qa, ka: (i, qa[p], 0)),
            scratch_shapes=[pltpu.VMEM((t, nh * dk), BF16), pltpu.VMEM((nh, t, LANES), F32),
                            pltpu.VMEM((nh, t, 2 * HEAD_DIM), F32)]),
        input_output_aliases=io_alias,
        compiler_params=_cparams(("parallel", "arbitrary")),
        name=name,
    )(qi_a, ki_a, *args)


def _band_kernel(q_ref, *rest, tq, pb, npv, n_kvh, r_n, scale, gate_branch, want_lse, qi_axis):
    rest = list(rest)
    kp = [rest.pop(0) for _ in range(npv)]
    kc = rest.pop(0)
    vp = [rest.pop(0) for _ in range(npv)]
    vc = rest.pop(0)
    bias_ref = rest.pop(0)
    if gate_branch is not None:
        misc_ref = rest.pop(0)
        prev_ref = rest.pop(0)
    o_ref = rest.pop(0)
    lse_ref = rest.pop(0) if want_lse else None
    qi = pl.program_id(qi_axis)
    nblk = tq // pb
    kw_prev = npv * pb
    if gate_branch is not None:
        misc = misc_ref[...].astype(F32)
    if want_lse:
        lane = lax.broadcasted_iota(jnp.int32, (tq, LANES), 1)
        lse_tile = jnp.zeros((tq, LANES), F32)
    ones_p = jnp.ones((pb, HEAD_DIM), BF16)
    ones_c = jnp.ones((tq, HEAD_DIM), BF16)
    for kh in range(n_kvh):
        ksl = slice(kh * HEAD_DIM, (kh + 1) * HEAD_DIM)
        v_ext = [jnp.concatenate([vp[n][:, ksl], ones_p], axis=1) for n in range(npv)]
        v_ext.append(jnp.concatenate([vc[:, ksl], ones_c], axis=1))
        for r in range(r_n):
            h = kh * r_n + r
            hsl = slice(h * HEAD_DIM, (h + 1) * HEAD_DIM)
            q = (q_ref[:, hsl].astype(F32) * (scale * LOG2E)).astype(BF16)
            parts = []
            for n in range(npv):
                pen = jnp.where(qi * nblk - npv + n >= 0, 0.0, NEG_INF)
                parts.append(_dot_nt(q, kp[n][:, ksl]) + bias_ref[h, :, n * pb:(n + 1) * pb] + pen)
            parts.append(_dot_nt(q, kc[:, ksl]) + bias_ref[h, :, kw_prev:])
            m = parts[0].max(axis=-1, keepdims=True)
            for sp in parts[1:]:
                m = jnp.maximum(m, sp.max(axis=-1, keepdims=True))
            o_ext = jnp.zeros((tq, 2 * HEAD_DIM), F32)
            for n, sp in enumerate(parts):
                o_ext = o_ext + _dot(jnp.exp2(sp - m).astype(BF16), v_ext[n])
            o = o_ext[:, :HEAD_DIM] * (1.0 / o_ext[:, HEAD_DIM:])
            if gate_branch is not None:
                c0 = GATE_LANE0 + h * 3 + gate_branch
                o = prev_ref[:, hsl] + _sigmoid(misc[:, c0:c0 + 1]) * o
            o_ref[:, hsl] = o.astype(o_ref.dtype)
            if want_lse:
                lse = m * (1.0 / LOG2E) + jnp.log(o_ext[:, HEAD_DIM:HEAD_DIM + 1])
                lse_tile = jnp.where(lane == h, lse, lse_tile)
    if want_lse:
        lse_ref[...] = lse_tile


def _band(q_arr, kv_arr, bias, *, lead_grid, tq, pb, npv, n_kvh, r_n, qmap, kmap, vmap, omap,
          lmap=None, out_shape, lse_shape=None, misc=None, miscmap=None, prev=None, gate_branch=None,
          scale, name):
    nl = len(lead_grid)
    nq = q_arr.shape[1] // tq
    nblk = tq // pb
    nh = n_kvh * r_n
    qw = nh * HEAD_DIM
    kw = n_kvh * HEAD_DIM

    def rows_cur(fn):
        def im(*g):
            bb, cc = fn(*g)
            return (bb, g[nl], cc)
        return im

    def rows_prev(fn, n):
        def im(*g):
            bb, cc = fn(*g)
            return (bb, jnp.maximum(g[nl] * nblk - npv + n, 0), cc)
        return im

    in_specs = [pl.BlockSpec((None, tq, qw), rows_cur(qmap))]
    args = [q_arr]
    for fn in (kmap, vmap):
        for n in range(npv):
            in_specs.append(pl.BlockSpec((None, pb, kw), rows_prev(fn, n)))
            args.append(kv_arr)
        in_specs.append(pl.BlockSpec((None, tq, kw), rows_cur(fn)))
        args.append(kv_arr)
    in_specs.append(pl.BlockSpec(bias.shape, lambda *g: (0, 0, 0)))
    args.append(bias)
    io_alias = {}
    if gate_branch is not None:
        in_specs.append(pl.BlockSpec((None, tq, LANES), rows_cur(miscmap)))
        args.append(misc)
        in_specs.append(pl.BlockSpec((None, tq, qw), rows_cur(omap)))
        args.append(prev)
        io_alias = {len(args) - 1: 0}
    out_shapes = [out_shape]
    out_specs = [pl.BlockSpec((None, tq, qw), rows_cur(omap))]
    if lse_shape is not None:
        out_shapes.append(lse_shape)
        out_specs.append(pl.BlockSpec((None, tq, LANES), rows_cur(lmap)))
    res = pl.pallas_call(
        functools.partial(_band_kernel, tq=tq, pb=pb, npv=npv, n_kvh=n_kvh, r_n=r_n, scale=scale,
                          gate_branch=gate_branch, want_lse=lse_shape is not None, qi_axis=nl),
        out_shape=tuple(out_shapes),
        grid=tuple(lead_grid) + (nq,),
        in_specs=in_specs,
        out_specs=tuple(out_specs),
        input_output_aliases=io_alias,
        compiler_params=_cparams(("parallel",) * (nl + 1)),
        name=name,
    )(*args)
    return res


def _dil_kernel(q_ref, *rest, dil, n_prev, back, scale):
    rest = list(rest)
    kp = [rest.pop(0) for _ in range(n_prev)]
    kc = rest.pop(0)
    vp = [rest.pop(0) for _ in range(n_prev)]
    vc = rest.pop(0)
    bias_ref, o_ref, lse_ref = rest
    c = pl.program_id(1)
    tile = q_ref.shape[0]
    n = tile // dil
    pl_rows = back // n_prev
    lane = lax.broadcasted_iota(jnp.int32, (n, LANES), 1)
    ones_p = jnp.ones((pl_rows, HEAD_DIM), BF16)
    ones_c = jnp.ones((n, HEAD_DIM), BF16)
    pens = [jnp.where(c - n_prev + pi >= 0, 0.0, NEG_INF) for pi in range(n_prev)]
    for r in range(dil):
        cur = slice(r * n, (r + 1) * n)
        prev = slice((r + 1) * n - pl_rows, (r + 1) * n)
        lse_tile = jnp.zeros((n, LANES), F32)
        for h in range(DIL_HEADS):
            hsl = slice(h * HEAD_DIM, (h + 1) * HEAD_DIM)
            q = (q_ref[cur, hsl].astype(F32) * (scale * LOG2E)).astype(BF16)
            parts, vals = [], []
            for pi in range(n_prev):
                parts.append(_dot_nt(q, kp[pi][prev, hsl]) + bias_ref[h, :, pi * pl_rows:(pi + 1) * pl_rows]
                             + pens[pi])
                vals.append(jnp.concatenate([vp[pi][prev, hsl], ones_p], axis=1))
            parts.append(_dot_nt(q, kc[cur, hsl]) + bias_ref[h, :, back:])
            vals.append(jnp.concatenate([vc[cur, hsl], ones_c], axis=1))
            m = parts[0].max(axis=-1, keepdims=True)
            for sp in parts[1:]:
                m = jnp.maximum(m, sp.max(axis=-1, keepdims=True))
            o_ext = jnp.zeros((n, 2 * HEAD_DIM), F32)
            for sp, vv in zip(parts, vals):
                o_ext = o_ext + _dot(jnp.exp2(sp - m).astype(BF16), vv)
            o_ref[h, pl.ds(r, n, stride=dil), :] = o_ext[:, :HEAD_DIM] * (1.0 / o_ext[:, HEAD_DIM:])
            lse = m * (1.0 / LOG2E) + jnp.log(o_ext[:, HEAD_DIM:HEAD_DIM + 1])
            lse_tile = jnp.where(lane == h, lse, lse_tile)
        lse_ref[pl.ds(r, n, stride=dil), :] = lse_tile


def _dil_attention(z, bias, *, tile, dil, back, col0, name):
    b, s, _ = z.shape
    n = tile // dil
    n_prev = max(1, back // n)
    hw_c = DIL_HEADS * HEAD_DIM

    def prev_map(pi, cb):
        return lambda i, c: (i, jnp.maximum(c - n_prev + pi, 0), cb)

    in_specs = [pl.BlockSpec((None, tile, hw_c), lambda i, c: (i, c, col0))]
    for cb in (col0 + 1, col0 + 2):
        for pi in range(n_prev):
            in_specs.append(pl.BlockSpec((None, tile, hw_c), prev_map(pi, cb)))
        in_specs.append(pl.BlockSpec((None, tile, hw_c), lambda i, c, cb=cb: (i, c, cb)))
    in_specs.append(pl.BlockSpec(bias.shape, lambda i, c: (0, 0, 0)))
    return pl.pallas_call(
        functools.partial(_dil_kernel, dil=dil, n_prev=n_prev, back=back, scale=HEAD_DIM ** -0.5),
        out_shape=(jax.ShapeDtypeStruct((b, DIL_HEADS, s, HEAD_DIM), F32),
                   jax.ShapeDtypeStruct((b, s, LANES), F32)),
        grid=(b, s // tile),
        in_specs=in_specs,
        out_specs=(pl.BlockSpec((None, DIL_HEADS, tile, HEAD_DIM), lambda i, c: (i, 0, c, 0)),
                   pl.BlockSpec((None, tile, LANES), lambda i, c: (i, c, 0))),
        compiler_params=_cparams(("parallel", "parallel")),
        name=name,
    )(*([z] * (3 + 2 * n_prev)), bias)


def _dil_full_kernel(q_ref, k_ref, v_ref, bias_ref, o_ref, lse_ref, *, dil, tile, heads, scale):
    hp = pl.program_id(1)
    s = q_ref.shape[0]
    n = tile // dil
    ls = s // dil
    lane = lax.broadcasted_iota(jnp.int32, (ls, LANES), 1)
    ones = jnp.ones((ls, HEAD_DIM), BF16)
    for r in range(dil):
        rows = [slice(t * tile + r * n, t * tile + (r + 1) * n) for t in range(s // tile)]
        lse_tile = jnp.zeros((ls, LANES), F32)
        for hh in range(heads):
            h = hp * heads + hh
            hsl = slice(hh * HEAD_DIM, (hh + 1) * HEAD_DIM)
            q = jnp.concatenate([q_ref[rs, hsl] for rs in rows], axis=0)
            k = jnp.concatenate([k_ref[rs, hsl] for rs in rows], axis=0)
            v = jnp.concatenate([v_ref[rs, hsl] for rs in rows] , axis=0)
            q = (q.astype(F32) * (scale * LOG2E)).astype(BF16)
            sc = _dot_nt(q, k) + bias_ref[h]
            m = sc.max(axis=-1, keepdims=True)
            o_ext = _dot(jnp.exp2(sc - m).astype(BF16), jnp.concatenate([v, ones], axis=1))
            o_ref[hh, pl.ds(r, ls, stride=dil), :] = o_ext[:, :HEAD_DIM] * (1.0 / o_ext[:, HEAD_DIM:])
            lse = m * (1.0 / LOG2E) + jnp.log(o_ext[:, HEAD_DIM:HEAD_DIM + 1])
            lse_tile = jnp.where(lane == h, lse, lse_tile)
        lse_ref[pl.ds(r, ls, stride=dil), :] = lse_tile


def _dil_full_attention(z, bias, *, tile, dil, col0, heads, name):
    b, s, _ = z.shape
    hw = heads * HEAD_DIM
    ng = DIL_HEADS // heads
    cpb = (DIL_HEADS * HEAD_DIM) // hw

    def col(j):
        return lambda i, g: (i, 0, (col0 + j) * cpb + g)

    return pl.pallas_call(
        functools.partial(_dil_full_kernel, dil=dil, tile=tile, heads=heads, scale=HEAD_DIM ** -0.5),
        out_shape=(jax.ShapeDtypeStruct((b, DIL_HEADS, s, HEAD_DIM), F32),
                   jax.ShapeDtypeStruct((b, ng, s, LANES), F32)),
        grid=(b, ng),
        in_specs=[pl.BlockSpec((None, s, hw), col(0)), pl.BlockSpec((None, s, hw), col(1)),
                  pl.BlockSpec((None, s, hw), col(2)), pl.BlockSpec(bias.shape, lambda i, g: (0, 0, 0))],
        out_specs=(pl.BlockSpec((None, heads, s, HEAD_DIM), lambda i, g: (i, g, 0, 0)),
                   pl.BlockSpec((None, None, s, LANES), lambda i, g: (i, g, 0, 0))),
        compiler_params=_cparams(("parallel", "parallel")),
        name=name,
    )(z, z, z, bias)


def _dist_table(rel_bias, n_dist, dist_scale=1):
    tab = rel_bias[_t5_bucket(jnp.arange(n_dist) * dist_scale)].astype(F32)
    return jnp.concatenate([tab, jnp.full((1, tab.shape[1]), NEG_INF, F32)], axis=0)


def _toeplitz_kernel(w_ref, o_ref):
    t = o_ref.shape[0]
    x = jnp.broadcast_to(w_ref[...], (t, 2 * t))
    o_ref[...] = pltpu.roll(x, 0, 1, stride=1, stride_axis=0)[:, :t]


def _toeplitz_tiles(tab, d0s, t, lo, hi, mult=1):
    masked = tab.shape[0] - 1
    u = np.arange(2 * t)
    i_minus_j = np.where(u < t, -u, 2 * t - u)
    dist = np.asarray(d0s)[:, None] + mult * i_minus_j[None, :]
    idx = np.where((dist >= lo) & (dist <= hi) & (u != t)[None, :], dist, masked)
    w = tab[jnp.asarray(idx, jnp.int32)].transpose(2, 0, 1)
    nh, nc = w.shape[0], w.shape[1]
    tiles = pl.pallas_call(
        _toeplitz_kernel,
        out_shape=jax.ShapeDtypeStruct((nh * nc, t, t), F32),
        grid=(nh * nc,),
        in_specs=[pl.BlockSpec((None, 1, 2 * t), lambda i: (i, 0, 0))],
        out_specs=pl.BlockSpec((None, t, t), lambda i: (i, 0, 0)),
        compiler_params=_cparams(("parallel",)),
        name="toeplitz_tiles",
    )(w.reshape(nh * nc, 1, 2 * t))
    return tiles.reshape(nh, nc, t, t)


def _band_bias(rel_bias, tq, pb, npv, max_back, dist_scale):
    tab = _dist_table(rel_bias, max_back + 1, dist_scale) * LOG2E
    d0s = [(npv - n) * pb for n in range(npv)] + [0]
    tiles = _toeplitz_tiles(tab, d0s, tq, 0, max_back)
    parts = [tiles[:, n, :, :pb] for n in range(npv)] + [tiles[:, npv]]
    return jnp.concatenate(parts, axis=-1)


def _out_ab_kernel(oa_ref, ob_ref, wa_ref, wb_ref, x_ref, g_ref, o_ref):
    y = _dot(oa_ref[...].astype(BF16), wa_ref[...]) + _dot(ob_ref[...], wb_ref[...])
    o_ref[...] = x_ref[...] + g_ref[...] * y


def _out_ab(oa, ob, w, x2d, gate, *, seq, tm):
    m, d = x2d.shape
    ka = oa.shape[1]
    kb = ob.shape[1]
    tpb = seq // tm
    return pl.pallas_call(
        _out_ab_kernel,
        out_shape=jax.ShapeDtypeStruct((m, d), F32),
        grid=(m // tm,),
        in_specs=[pl.BlockSpec((tm, ka), lambda i: (i, 0)),
                  pl.BlockSpec((tm, kb), lambda i: (i, 0)),
                  pl.BlockSpec((ka, d), lambda i: (0, 0)),
                  pl.BlockSpec((kb, d), lambda i: (ka // kb, 0)),
                  pl.BlockSpec((tm, d), lambda i: (i, 0)),
                  pl.BlockSpec((None, 1, d), lambda i: (i // tpb, 0, 0))],
        out_specs=pl.BlockSpec((tm, d), lambda i: (i, 0)),
        compiler_params=_cparams(("parallel",)),
        name="out_proj_ab",
    )(oa, ob, w, w, x2d, gate)


def _out_c_kernel(o0_ref, o1_ref, o2_ref, l0_ref, l1_ref, l2_ref, w_ref, x_ref, g_ref, o_ref, mrg_scr):
    l0, l1, l2 = l0_ref[...], l1_ref[...], l2_ref[...]
    mx = jnp.maximum(jnp.maximum(l0, l1), l2)
    e0, e1, e2 = jnp.exp(l0 - mx), jnp.exp(l1 - mx), jnp.exp(l2 - mx)
    inv = 1.0 / (e0 + e1 + e2)
    w0, w1, w2 = e0 * inv, e1 * inv, e2 * inv
    for h in range(DIL_HEADS):
        sl = slice(h * HEAD_DIM, (h + 1) * HEAD_DIM)
        mg = (w0[:, h:h + 1] * o0_ref[:, sl] + w1[:, h:h + 1] * o1_ref[h]
              + w2[:, h:h + 1] * o2_ref[h])
        mrg_scr[:, sl] = mg.astype(BF16)
    o_ref[...] = x_ref[...] + g_ref[...] * _dot(mrg_scr[...], w_ref[...])


def _out_c(os_, lses, w, x2d, gate, *, seq, tm):
    m, d = x2d.shape
    kc = w.shape[0]
    tpb = seq // tm
    return pl.pallas_call(
        _out_c_kernel,
        out_shape=jax.ShapeDtypeStruct((m, d), F32),
        grid=(m // tm,),
        in_specs=[pl.BlockSpec((tm, kc), lambda i: (i, 0))]
        + [pl.BlockSpec((None, DIL_HEADS, tm, HEAD_DIM), lambda i: (i // tpb, 0, i % tpb, 0))] * 2
        + [pl.BlockSpec((tm, LANES), lambda i: (i, 0))] * 3
        + [pl.BlockSpec((kc, d), lambda i: (0, 0)),
           pl.BlockSpec((tm, d), lambda i: (i, 0)),
           pl.BlockSpec((None, 1, d), lambda i: (i // tpb, 0, 0))],
        out_specs=pl.BlockSpec((tm, d), lambda i: (i, 0)),
        scratch_shapes=[pltpu.VMEM((tm, kc), BF16)],
        compiler_params=_cparams(("parallel",)),
        name="out_proj_c",
    )(*os_, *lses, w, x2d, gate)


def _ffn_kernel(x_ref, g_ref, s_ref, wg_ref, wu_ref, wd_ref, gate_ref, o_ref, h_scr):
    f = pl.program_id(1)
    tm = x_ref.shape[0]

    @pl.when(f == 0)
    def _():
        n0 = tm // PERM_NORM_CHUNKS
        for r in range(PERM_NORM_CHUNKS):
            rows = slice(r * n0, (r + 1) * n0)
            x = x_ref[rows, :]
            ms = jnp.mean(x * x, axis=-1, keepdims=True)
            h_scr[rows, :] = (x * lax.rsqrt(ms + RMS_EPS) * g_ref[...] + s_ref[...]).astype(BF16)
        o_ref[...] = jnp.zeros(o_ref.shape, o_ref.dtype)

    h = h_scr[...]
    a = _dot(h, wg_ref[...].astype(BF16))
    b = _dot(h, wu_ref[...].astype(BF16))
    hid = (a * _sigmoid(a) * b).astype(BF16)
    o_ref[...] += _dot(hid, wd_ref[...].astype(BF16))

    @pl.when(f == pl.num_programs(1) - 1)
    def _():
        o_ref[...] = x_ref[...] + gate_ref[...] * o_ref[...]


def _ffn(x2d, geff, shift, wg, wu, wd, gate, *, seq, tm, tf):
    m, d = x2d.shape
    ff = wg.shape[1]
    tpb = seq // tm
    vec = pl.BlockSpec((None, 1, d), lambda i, f: (i // tpb, 0, 0))
    return pl.pallas_call(
        _ffn_kernel,
        out_shape=jax.ShapeDtypeStruct((m, d), F32),
        grid=(m // tm, ff // tf),
        in_specs=[pl.BlockSpec((tm, d), lambda i, f: (i, 0)), vec, vec,
                  pl.BlockSpec((d, tf), lambda i, f: (0, f)),
                  pl.BlockSpec((d, tf), lambda i, f: (0, f)),
                  pl.BlockSpec((tf, d), lambda i, f: (f, 0)),
                  vec],
        out_specs=pl.BlockSpec((tm, d), lambda i, f: (i, 0)),
        scratch_shapes=[pltpu.VMEM((tm, d), BF16)],
        compiler_params=_cparams(("parallel", "arbitrary")),
        name="ffn_swiglu",
    )(x2d, geff, shift, wg, wu, wd, gate)


def _moe_prep_kernel(x_ref, g_ref, s_ref, wr_ref, h_ref, route_ref):
    x = x_ref[...]
    ms = jnp.mean(x * x, axis=-1, keepdims=True)
    h = x * lax.rsqrt(ms + RMS_EPS) * g_ref[...] + s_ref[...]
    h_ref[...] = h
    logits = jnp.dot(h, wr_ref[...], precision=lax.Precision.HIGHEST, preferred_element_type=F32)
    lane = lax.broadcasted_iota(jnp.int32, logits.shape, 1).astype(F32)
    lg = jnp.where(lane < float(N_EXPERTS), logits, -3e38)
    m1 = jnp.max(lg, axis=-1, keepdims=True)
    i1 = jnp.min(jnp.where(lg == m1, lane, float(LANES)), axis=-1, keepdims=True)
    lg2 = jnp.where(lane == i1, -3e38, lg)
    m2 = jnp.max(lg2, axis=-1, keepdims=True)
    i2 = jnp.min(jnp.where(lg2 == m2, lane, float(LANES)), axis=-1, keepdims=True)
    e = jnp.exp(m2 - m1)
    inv = 1.0 / (1.0 + e)
    route = jnp.where(lane == 0.0, i1, jnp.where(lane == 1.0, i2, jnp.where(lane == 2.0, inv,
                      jnp.where(lane == 3.0, e * inv, 0.0))))
    route_ref[...] = route


def _moe_prep(x2d, geff, shift, w_router_pad, *, seq, tm):
    m, d = x2d.shape
    tpb = seq // tm
    vec = pl.BlockSpec((None, 1, d), lambda i: (i // tpb, 0, 0))
    return pl.pallas_call(
        _moe_prep_kernel,
        out_shape=(jax.ShapeDtypeStruct((m, d), F32), jax.ShapeDtypeStruct((m, LANES), F32)),
        grid=(m // tm,),
        in_specs=[pl.BlockSpec((tm, d), lambda i: (i, 0)), vec, vec,
                  pl.BlockSpec((d, LANES), lambda i: (0, 0))],
        out_specs=(pl.BlockSpec((tm, d), lambda i: (i, 0)), pl.BlockSpec((tm, LANES), lambda i: (i, 0))),
        compiler_params=_cparams(("parallel",)),
        name="moe_prep",
    )(x2d, geff, shift, w_router_pad)


def _row_copy(src_hbm, row, dst_vmem, slot, sem):
    return pltpu.make_async_copy(src_hbm.at[pl.ds(row, 1)], dst_vmem.at[pl.ds(slot, 1)], sem)


def _expert_kernel(te_ref, nu_ref, rows_ref, slot_ref, h_hbm, wg_ref, wu_ref, wd_ref, o_ref, xbuf, xs, sem, *,
                   tm, issue_steps):
    t = pl.program_id(0)
    f = pl.program_id(1)
    nf = pl.num_programs(1)
    n_used = nu_ref[0]
    live = t < n_used
    rows_per_step = tm // issue_steps

    @pl.when((t == 0) & (f == 0))
    def _():
        def start(r, c):
            _row_copy(h_hbm, slot_ref[0, r], xbuf, r, sem).start()
            return c
        lax.fori_loop(0, tm, start, 0, unroll=ROW_DMA_UNROLL)

    @pl.when(live & (f == 0))
    def _():
        def wait(r, c):
            _row_copy(h_hbm, 0, xbuf, r, sem).wait()
            return c
        lax.fori_loop(0, tm, wait, 0, unroll=ROW_DMA_UNROLL)
        xs[...] = xbuf[...].astype(BF16)
        o_ref[...] = jnp.zeros(o_ref.shape, o_ref.dtype)

    def compute(mr):
        x = xs[:mr, :]
        a = _dot(x, wg_ref[...].astype(BF16))
        b = _dot(x, wu_ref[...].astype(BF16))
        hid = (a * _sigmoid(a) * b).astype(BF16)
        o_ref[:mr, :] += _dot(hid, wd_ref[...].astype(BF16))

    prefetch = live & (t + 1 < n_used) & (f >= 1) & (f <= issue_steps)
    quarter = tm // MOE_ROW_SPLITS
    n_quarters = (rows_ref[t] + quarter - 1) // quarter

    for want_prefetch in (True, False):
        for nq in range(1, MOE_ROW_SPLITS + 1):
            cond = live & (prefetch if want_prefetch else jnp.logical_not(prefetch)) & (n_quarters == nq)

            @pl.when(cond)
            def _(want_prefetch=want_prefetch, nq=nq):
                if want_prefetch:
                    base = (f - 1) * rows_per_step
                    for j in range(rows_per_step):
                        _row_copy(h_hbm, slot_ref[t + 1, base + j], xbuf, base + j, sem).start()
                compute(nq * quarter)

    @pl.when(jnp.logical_not(live) & (f == nf - 1))
    def _():
        o_ref[...] = jnp.zeros(o_ref.shape, o_ref.dtype)


def _moe_experts(h2d, slot_tok, tile_e, n_used, tile_rows, wg, wu, wd, *, tm, tf, issue_steps):
    n_tiles = slot_tok.shape[0]
    d = h2d.shape[1]
    ff = wg.shape[2]
    nf = ff // tf
    assert tm % issue_steps == 0 and issue_steps < nf

    def f_blk(t, f, nu):
        return jnp.where(t < nu[0], f, nf - 1)

    return pl.pallas_call(
        functools.partial(_expert_kernel, tm=tm, issue_steps=issue_steps),
        out_shape=jax.ShapeDtypeStruct((n_tiles * tm, d), F32),
        grid_spec=pltpu.PrefetchScalarGridSpec(
            num_scalar_prefetch=4,
            grid=(n_tiles, nf),
            in_specs=[pl.BlockSpec(memory_space=pl.ANY),
                      pl.BlockSpec((None, d, tf), lambda t, f, te, nu, rw, sl: (te[t], 0, f_blk(t, f, nu))),
                      pl.BlockSpec((None, d, tf), lambda t, f, te, nu, rw, sl: (te[t], 0, f_blk(t, f, nu))),
                      pl.BlockSpec((None, tf, d), lambda t, f, te, nu, rw, sl: (te[t], f_blk(t, f, nu), 0))],
            out_specs=pl.BlockSpec((tm, d), lambda t, f, te, nu, rw, sl: (t, 0)),
            scratch_shapes=[pltpu.VMEM((tm, d), F32), pltpu.VMEM((tm, d), BF16),
                            pltpu.SemaphoreType.DMA(())]),
        compiler_params=_cparams(("arbitrary", "arbitrary")),
        name="moe_experts",
    )(tile_e, n_used, tile_rows, slot_tok, h2d, wg, wu, wd)


def _combine_kernel(d1_ref, d2_ref, yb_hbm, route_ref, x_ref, gate_ref, fg_ref, o_ref, b1, b2, sem, *, tmc):
    t = pl.program_id(0)

    def start(r, c):
        _row_copy(yb_hbm, d1_ref[t, r], b1, r, sem).start()
        _row_copy(yb_hbm, d2_ref[t, r], b2, r, sem).start()
        return c

    def wait(r, c):
        _row_copy(yb_hbm, 0, b1, r, sem).wait()
        _row_copy(yb_hbm, 0, b2, r, sem).wait()
        return c

    lax.fori_loop(0, tmc, start, 0, unroll=ROW_DMA_UNROLL)
    lax.fori_loop(0, tmc, wait, 0, unroll=ROW_DMA_UNROLL)
    route = route_ref[...]
    y = route[:, 2:3] * b1[...] + route[:, 3:4] * b2[...]
    xo = x_ref[...] + gate_ref[...] * y
    ms = jnp.mean(xo * xo, axis=-1, keepdims=True)
    o_ref[...] = xo * lax.rsqrt(ms + RMS_EPS) * fg_ref[...]


def _moe_combine(yb, dest1, dest2, route, x2d, gate, final_g, *, seq, tmc):
    m, d = x2d.shape
    tpb = seq // tmc
    return pl.pallas_call(
        functools.partial(_combine_kernel, tmc=tmc),
        out_shape=jax.ShapeDtypeStruct((m, d), F32),
        grid_spec=pltpu.PrefetchScalarGridSpec(
            num_scalar_prefetch=2,
            grid=(m // tmc,),
            in_specs=[pl.BlockSpec(memory_space=pl.ANY),
                      pl.BlockSpec((tmc, LANES), lambda t, a, b: (t, 0)),
                      pl.BlockSpec((tmc, d), lambda t, a, b: (t, 0)),
                      pl.BlockSpec((None, 1, d), lambda t, a, b: (t // tpb, 0, 0)),
                      pl.BlockSpec((1, d), lambda t, a, b: (0, 0))],
            out_specs=pl.BlockSpec((tmc, d), lambda t, a, b: (t, 0)),
            scratch_shapes=[pltpu.VMEM((tmc, d), F32), pltpu.VMEM((tmc, d), F32),
                            pltpu.SemaphoreType.DMA(())]),
        compiler_params=_cparams(("arbitrary",)),
        name="moe_combine",
    )(dest1, dest2, yb, route, x2d, gate, final_g)


def _moe_plan(route, *, tm):
    n = route.shape[0]
    a = n * MOE_TOP_K
    flat_e = route[:, :MOE_TOP_K].astype(jnp.int32).reshape(a)
    onehot = (flat_e[:, None] == jnp.arange(N_EXPERTS, dtype=jnp.int32)[None, :]).astype(jnp.int32)
    csum = jnp.cumsum(onehot, axis=0)
    pos = jnp.sum(onehot * (csum - 1), axis=1)
    counts = csum[-1]
    pcounts = (counts + tm - 1) // tm * tm
    pends = jnp.cumsum(pcounts)
    pstarts = pends - pcounts
    dest = (pstarts[flat_e] + pos).astype(jnp.int32)
    n_tiles = a // tm + N_EXPERTS
    tok = jnp.arange(a, dtype=jnp.int32) // MOE_TOP_K
    slot_tok = jnp.zeros((n_tiles * tm,), jnp.int32).at[dest].set(tok)
    n_used = (pends[-1] // tm).astype(jnp.int32)
    tile_e = jnp.minimum(jnp.searchsorted(pends, jnp.arange(n_tiles, dtype=jnp.int32) * tm, side='right'),
                         N_EXPERTS - 1).astype(jnp.int32)
    tile_e = jnp.where(jnp.arange(n_tiles) < n_used, tile_e, tile_e[jnp.maximum(n_used - 1, 0)])
    tile_rows = jnp.clip(counts[tile_e] - (jnp.arange(n_tiles, dtype=jnp.int32) * tm - pstarts[tile_e]), 0, tm)
    tile_rows = jnp.where(jnp.arange(n_tiles) < n_used, tile_rows, 0).astype(jnp.int32)
    return dest.reshape(n, MOE_TOP_K), slot_tok.reshape(n_tiles, tm), tile_e, n_used.reshape(1), tile_rows


def _tile(n, pref):
    t = min(n, pref)
    assert n % t == 0, (n, pref)
    return t


def kernel(x, c, positions, rel_bias, ada_w, ada_b, mix_norm_g, ffn_norm_g, ab_w_in, ab_w_out, nsa_cmp_pos_k, nsa_cmp_w1_k, nsa_cmp_w2_k, nsa_cmp_pos_v, nsa_cmp_w1_v, nsa_cmp_w2_v, mla_q_norm_g, mla_kv_norm_g, mla_w_uq, mla_w_ukv, ffn_w_gate, ffn_w_up, ffn_w_down, c_w_in, c_w_out, moe_w_router, moe_w_gate, moe_w_up, moe_w_down, final_norm_g):
    b, s, d = x.shape
    assert ada_w.shape[0] == 2 and d == D_MODEL and s % 256 == 0
    m = b * s
    x2d = x.reshape(m, d)
    tm_big = _tile(s, 1024)
    tm_mid = _tile(s, 512)
    t_att = _tile(s, 256)
    t_flash = _tile(s, 512)

    mod = _ada_mod(c, ada_w, ada_b)
    mods = mod.reshape(2, b, 6, 1, d)

    def layer_mod(i):
        sh_m, sc_m, g_m, sh_f, sc_f, g_f = (mods[i, :, j] for j in range(6))
        return (mix_norm_g[i][None, None, :] * (1.0 + sc_m), sh_m, g_m,
                ffn_norm_g[i][None, None, :] * (1.0 + sc_f), sh_f, g_f)

    geff_m, sh_m, g_m, geff_f, sh_f, g_f = layer_mod(0)
    w0 = ab_w_in[0]
    c_q, c_kv, c_g, c_ql, c_kvl = np.cumsum([NSA_HEADS * HEAD_DIM, 6 * NSA_KV_HEADS * HEAD_DIM,
                                             3 * NSA_HEADS, Q_LORA, KV_LORA]).tolist()
    zpad = lambda n: jnp.zeros((d, n), w0.dtype)
    w_in0 = jnp.concatenate([w0[:, :c_kv], w0[:, c_ql:c_kvl], w0[:, c_kvl:], w0[:, c_kv:c_g],
                             zpad(LANES - QK_ROPE - 3 * NSA_HEADS), w0[:, c_g:c_ql], zpad(LANES)],
                            axis=1).astype(BF16)
    assert w_in0.shape[1] == Z0_W
    z0 = _nmm(x2d, geff_m, sh_m, w_in0, seq=s, tm=tm_mid, tn=Z0_W, norm_cols=d, x_block=0,
              out_dtype=BF16, name="proj_in_ab")
    z0_3d = z0.reshape(b, s, Z0_W)

    inv_freq = ROPE_THETA ** (-jnp.arange(0, QK_ROPE, 2, dtype=F32) / QK_ROPE)
    ang = positions.astype(F32)[..., None] * inv_freq
    cos, sin = jnp.cos(ang).reshape(m, -1), jnp.sin(ang).reshape(m, -1)
    hr = QK_ROPE // 2
    zr = lambda n: jnp.zeros((m, n), F32)
    rope_tabs = (jnp.concatenate([cos, cos, zr(LANES - 2 * hr)], axis=1),
                 jnp.concatenate([-sin, zr(LANES - hr)], axis=1),
                 jnp.concatenate([zr(hr), sin, zr(LANES - 2 * hr)], axis=1))
    hw = MXU_DIM
    wq3 = mla_w_uq[0].reshape(Q_LORA, MLA_HEADS, QK_NOPE + QK_ROPE)
    wq = jnp.concatenate([wq3, jnp.zeros((Q_LORA, MLA_HEADS, hw - QK_NOPE - QK_ROPE), F32)], axis=2)
    wq = jnp.concatenate([wq.reshape(Q_LORA, MLA_HEADS * hw), jnp.zeros((LANES, MLA_HEADS * hw), F32)],
                         axis=0).astype(BF16)
    wkv3 = mla_w_ukv[0].reshape(KV_LORA, MLA_HEADS, QK_NOPE + V_DIM)
    wk_top = jnp.concatenate([wkv3[:, :, :QK_NOPE], jnp.zeros((KV_LORA, MLA_HEADS, hw - QK_NOPE), F32)],
                             axis=2).reshape(KV_LORA, MLA_HEADS * hw)
    pe_pass = jnp.zeros((LANES, hw), F32).at[jnp.arange(QK_ROPE), QK_NOPE + jnp.arange(QK_ROPE)].set(1.0)
    wk_bot = jnp.tile(pe_pass, (1, MLA_HEADS))
    wv = jnp.concatenate([wkv3[:, :, QK_NOPE:].reshape(KV_LORA, MLA_HEADS * V_DIM),
                          jnp.zeros((LANES, MLA_HEADS * V_DIM), F32)], axis=0)
    wkv = jnp.concatenate([jnp.concatenate([wk_top, wk_bot], axis=0), wv], axis=1).astype(BF16)
    ones_b = lambda g: jnp.broadcast_to(g[None, None, :], (b, 1, g.shape[0]))
    zeros_lat = jnp.zeros((b, 1, Q_LORA), F32)
    q_mla = _nmm(z0, ones_b(mla_q_norm_g[0]), zeros_lat, wq, seq=s, tm=tm_big, tn=1024, norm_cols=Q_LORA,
                 x_block=Z0_QLAT // LAT_BLOCK, out_dtype=BF16, rope=rope_tabs, rope_tiles=2,
                 name="mla_q_up")
    kv_mla = _nmm(z0, ones_b(mla_kv_norm_g[0]), zeros_lat, wkv, seq=s, tm=tm_big, tn=1024,
                  norm_cols=KV_LORA, x_block=Z0_KVLAT // LAT_BLOCK, out_dtype=BF16, rope=rope_tabs,
                  rope_tiles=2, name="mla_kv_up")
    o_mla = _flash(q_mla.reshape(b, s, -1), kv_mla.reshape(b, s, -1), kv_mla.reshape(b, s, -1),
                   t=t_flash, n_kvh=MLA_HEADS, r_n=1, dk=hw, q_blk=0, k_blk=0,
                   v_blk=(MLA_HEADS * hw) // (MLA_HEADS * V_DIM), scale=(QK_NOPE + QK_ROPE) ** -0.5,
                   out_dtype=BF16, name="mla_attn")

    n16 = s // CMP_STRIDE
    cmp_cols = z0_3d[:, :, Z0_KV:Z0_KV + 2 * NSA_KV_HEADS * HEAD_DIM]
    x16 = cmp_cols.reshape(b, n16, CMP_STRIDE, 2 * NSA_KV_HEADS, HEAD_DIM).transpose(0, 3, 1, 2, 4)
    x16 = x16.reshape(b, 2 * NSA_KV_HEADS, n16, CMP_STRIDE * HEAD_DIM)
    pos_kv = jnp.stack([nsa_cmp_pos_k[0], nsa_cmp_pos_v[0]]).reshape(2, 1, CMP_LEN * HEAD_DIM).astype(BF16)
    w1_kv = jnp.stack([nsa_cmp_w1_k[0], nsa_cmp_w1_v[0]]).astype(BF16)
    w2_kv = jnp.stack([nsa_cmp_w2_k[0], nsa_cmp_w2_v[0]]).astype(BF16)
    kvc = _nsa_compress(x16, pos_kv, w1_kv, w2_kv)

    n_cmp = (s - CMP_LEN) // CMP_STRIDE + 1
    n_slc = s // SLC_LEN
    ratio, span = SLC_LEN // CMP_STRIDE, CMP_LEN // CMP_STRIDE
    cm = np.zeros((n16, n_slc), np.float32)
    for j in range(n_slc):
        for mm in range(ratio):
            for nn in range(span):
                i = ratio * j + mm - nn
                if 0 <= i < n_cmp:
                    cm[i, j] += 1.0
    tab_s = _dist_table(rel_bias, s)
    bias_c = _toeplitz_tiles(tab_s, [r - (CMP_LEN - 1) for r in range(CMP_STRIDE)], n16, 0, s - 1,
                             mult=CMP_STRIDE)
    bias_c = bias_c.transpose(0, 2, 1, 3).reshape(NSA_HEADS, s, n16)
    o_nsa, sel = _nsa_cmp_attention(z0_3d, kvc, bias_c, jnp.asarray(cm), tq=t_flash)

    nq = s // t_flash
    nd = min(nq, -(-(T5_MAX_DIST + t_flash - 1) // t_flash) + 1)
    bias_d = _toeplitz_tiles(_dist_table(rel_bias, nd * t_flash) * LOG2E, [dd * t_flash for dd in range(nd)],
                             t_flash, 0, nd * t_flash - 1)
    kvw = NSA_KV_HEADS * HEAD_DIM
    o_nsa = _flash(z0_3d, z0_3d, z0_3d, t=t_flash, n_kvh=NSA_KV_HEADS, r_n=NSA_GROUP, dk=HEAD_DIM,
                   q_blk=0, k_blk=(Z0_KV + 2 * kvw) // kvw, v_blk=(Z0_KV + 3 * kvw) // kvw,
                   scale=HEAD_DIM ** -0.5, out_dtype=F32, bias=bias_d, sel=sel, misc=z0_3d,
                   misc_blk=Z0_MISC // LANES, prev=o_nsa, gate_branch=1, name="nsa_slc_attn")

    npv_w = -(-(WIN - 1) // t_att)
    bias_w = _band_bias(rel_bias, t_att, t_att, npv_w, WIN - 1, 1)
    (o_nsa,) = _band(z0_3d, z0_3d, bias_w, lead_grid=(b,), tq=t_att, pb=t_att, npv=npv_w,
                     n_kvh=NSA_KV_HEADS, r_n=NSA_GROUP,
                     qmap=lambda i, j: (i, 0), kmap=lambda i, j: (i, (Z0_KV + 4 * kvw) // kvw),
                     vmap=lambda i, j: (i, (Z0_KV + 5 * kvw) // kvw), omap=lambda i, j: (i, 0),
                     out_shape=jax.ShapeDtypeStruct((b, s, NSA_HEADS * HEAD_DIM), F32),
                     misc=z0_3d, miscmap=lambda i, j: (i, Z0_MISC // LANES), prev=o_nsa, gate_branch=2,
                     scale=HEAD_DIM ** -0.5, name="nsa_win_attn")

    x2d = _out_ab(o_nsa.reshape(m, -1), o_mla.reshape(m, -1), ab_w_out[0].astype(BF16), x2d, g_m,
                  seq=s, tm=tm_mid)

    x2d = _ffn(x2d, geff_f, sh_f, ffn_w_gate[0], ffn_w_up[0], ffn_w_down[0], g_f, seq=s, tm=tm_big, tf=256)

    geff_m, sh_m, g_m, geff_f, sh_f, g_f = layer_mod(1)
    cw = c_w_in.shape[2]
    dils = tuple(dil for _, dil in DIL_PATTERNS)
    z1 = _nmm_perm(x2d, geff_m, sh_m, c_w_in[0].astype(BF16), seq=s, tm=tm_big, tn=1024, dils=dils,
                   out_dtype=BF16, name="proj_in_c")
    hw_c = DIL_HEADS * HEAD_DIM
    z1_3d = z1.reshape(b, s, cw)
    os_, lses = [], []
    for gidx, (win, dil) in enumerate(DIL_PATTERNS):
        max_back = win // dil
        if dil == 1:
            tq = _tile(s, 256)
            pb = min(tq, 128)
            npv = -(-max_back // pb)
            bias_g = _band_bias(rel_bias, tq, pb, npv, max_back, dil)
            og, lg = _band(z1_3d, z1_3d, bias_g, lead_grid=(b,), tq=tq, pb=pb, npv=npv, n_kvh=DIL_HEADS,
                           r_n=1, qmap=lambda i, j, c0=gidx * 3: (i, c0),
                           kmap=lambda i, j, c0=gidx * 3: (i, c0 + 1),
                           vmap=lambda i, j, c0=gidx * 3: (i, c0 + 2),
                           omap=lambda i, j: (i, 0), lmap=lambda i, j: (i, 0),
                           out_shape=jax.ShapeDtypeStruct((b, s, hw_c), F32),
                           lse_shape=jax.ShapeDtypeStruct((b, s, LANES), F32),
                           scale=HEAD_DIM ** -0.5, name=f"dil_attn_{gidx}")
        elif s // dil <= 256:
            ls = s // dil
            bias_g = _toeplitz_tiles(_dist_table(rel_bias, max_back + 1, dil) * LOG2E, [0], ls, 0,
                                     max_back)[:, 0]
            og, lg = _dil_full_attention(z1_3d, bias_g, tile=tm_big, dil=dil, col0=gidx * 3, heads=2,
                                         name=f"dil_attn_{gidx}")
            lg = lg.sum(axis=1)
        else:
            n_res = tm_big // dil
            n_prev = max(1, max_back // n_res)
            bias_g = _band_bias(rel_bias, n_res, max_back // n_prev, n_prev, max_back, dil)
            og, lg = _dil_attention(z1_3d, bias_g, tile=tm_big, dil=dil, back=max_back, col0=gidx * 3,
                                    name=f"dil_attn_{gidx}")
        os_.append(og.reshape(m, hw_c) if dil == 1 else og)
        lses.append(lg.reshape(m, LANES))
    x2d = _out_c(os_, lses, c_w_out[0].astype(BF16), x2d, g_m, seq=s, tm=tm_mid)

    wr = jnp.concatenate([moe_w_router[0], jnp.zeros((d, LANES - N_EXPERTS), F32)], axis=1)
    h_moe, route = _moe_prep(x2d, geff_f, sh_f, wr, seq=s, tm=tm_mid)
    tm_e = 1024
    dest, slot_tok, tile_e, n_used, tile_rows = _moe_plan(route, tm=tm_e)
    yb = _moe_experts(h_moe, slot_tok, tile_e, n_used, tile_rows, moe_w_gate[0], moe_w_up[0], moe_w_down[0],
                      tm=tm_e, tf=256, issue_steps=16)
    tmc = _tile(s, 256)
    out = _moe_combine(yb, dest[:, 0].reshape(m // tmc, tmc), dest[:, 1].reshape(m // tmc, tmc), route,
                       x2d, g_f, final_norm_g.reshape(1, d), seq=s, tmc=tmc)
    return out.reshape(b, s, d)
```
